```python
import math
import jax, jax.numpy as jnp
from jax import lax
import numpy as np

D_MODEL = 2048
BATCH = 16
SEQ = 2048
DEPTH = 2
DEC_BATCH = 2
DEC_SEQ = 8192
PAST_LEN = 128

HEAD_DIM = 128
MIX_WIDTH = D_MODEL
A_HEADS = (D_MODEL // 2) // HEAD_DIM
A_KV_HEADS = A_HEADS // 4
A_Q = A_HEADS * HEAD_DIM
A_KV = A_KV_HEADS * HEAD_DIM
B_HEADS = (D_MODEL // 4) // HEAD_DIM
B_HALF = HEAD_DIM // 2
B_WIDTH = B_HEADS * HEAD_DIM
C_WIDTH = D_MODEL // 4
C_GROUPS = 4
C_GROUP_W = C_WIDTH // C_GROUPS
IN_WIDTH = A_Q + 2 * A_KV + 3 * B_WIDTH + C_WIDTH
GRID_W = 64
ROPE_THETA = 10000.0
NUM_BUCKETS = 32
MAX_DISTANCE = 128
Q_BLOCK = 128
N_GROUPS = 4
EXPERTS_PER_GROUP = 4
N_EXPERTS = N_GROUPS * EXPERTS_PER_GROUP
TOP_K = 2
D_FF_EXPERT = D_MODEL // 4
EPS = 1e-6

kernel_name = "hybrid_parallel_encoder"


def rms_norm(x, g):
    xf = x.astype(jnp.float32)
    y = xf * lax.rsqrt(jnp.mean(xf * xf, axis=-1, keepdims=True) + EPS)
    return (y * g.astype(jnp.float32)).astype(x.dtype)


def axial_rope_tables(n):
    n_rows = n // GRID_W
    rows = jnp.repeat(jnp.arange(n_rows), GRID_W).astype(jnp.float32)
    cols = jnp.tile(jnp.arange(GRID_W), n_rows).astype(jnp.float32)
    half = HEAD_DIM // 2
    inv = ROPE_THETA ** (-jnp.arange(0, half, 2, dtype=jnp.float32) / half)
    ang = jnp.concatenate([rows[:, None] * inv, cols[:, None] * inv], axis=-1)
    return jnp.cos(ang), jnp.sin(ang)


def apply_rope(x, cos, sin):
    xf = x.astype(jnp.float32).reshape(*x.shape[:-1], HEAD_DIM // 2, 2)
    x1, x2 = xf[..., 0], xf[..., 1]
    c = cos[None, :, None, :]
    s = sin[None, :, None, :]
    out = jnp.stack([x1 * c - x2 * s, x1 * s + x2 * c], axis=-1)
    return out.reshape(x.shape).astype(x.dtype)


def t5_bucket(rel):
    nb = NUM_BUCKETS // 2
    ret = (rel > 0).astype(jnp.int32) * nb
    n = jnp.abs(rel)
    max_exact = nb // 2
    nf = jnp.maximum(n, 1).astype(jnp.float32)
    large = max_exact + (jnp.log(nf / max_exact) / math.log(MAX_DISTANCE / max_exact)
                         * (nb - max_exact)).astype(jnp.int32)
    large = jnp.minimum(large, nb - 1)
    return ret + jnp.where(n < max_exact, n, large)


def gqa_attention(q, k, v):
    b, n, h, d = q.shape
    kvh = k.shape[2]
    grp = h // kvh
    nblk = n // Q_BLOCK
    qb = q.reshape(b, nblk, Q_BLOCK, kvh, grp, d).transpose(1, 0, 2, 3, 4, 5)
    scale = d ** -0.5

    def block(qi):
        s = jnp.einsum('bqkgd,bskd->bkgqs', qi, k, preferred_element_type=jnp.float32) * scale
        p = jax.nn.softmax(s, axis=-1).astype(v.dtype)
        return jnp.einsum('bkgqs,bskd->bqkgd', p, v)

    o = lax.map(block, qb)
    return o.transpose(1, 0, 2, 3, 4, 5).reshape(b, n, h * d)


def diff_attention(q, k, v, bias_table, lam):
    b, n, h, _, dh = q.shape
    nblk = n // Q_BLOCK
    qb = q.reshape(b, nblk, Q_BLOCK, h, 2, dh).transpose(1, 0, 2, 3, 4, 5)
    starts = jnp.arange(nblk, dtype=jnp.int32) * Q_BLOCK
    kpos = jnp.arange(n, dtype=jnp.int32)
    scale = dh ** -0.5

    def block(args):
        qi, s0 = args
        qpos = s0 + jnp.arange(Q_BLOCK, dtype=jnp.int32)
        bucket = t5_bucket(kpos[None, :] - qpos[:, None])
        bias = jnp.take(bias_table, bucket, axis=0).astype(jnp.float32).transpose(2, 0, 1)
        s = jnp.einsum('bqhcd,bshcd->bhcqs', qi, k, preferred_element_type=jnp.float32) * scale
        p = jax.nn.softmax(s + bias[None, :, None], axis=-1)
        pd = p[:, :, 0] - lam * p[:, :, 1]
        return jnp.einsum('bhqs,bshd->bqhd', pd.astype(v.dtype), v)

    o = lax.map(block, (qb, starts))
    return o.transpose(1, 0, 2, 3, 4).reshape(b, n, h, -1)


def fourier_mix(u, w_fourier):
    b, n, _ = u.shape
    ug = u.reshape(b, n, C_GROUPS, C_GROUP_W).astype(jnp.float32)
    f = jnp.fft.fft2(ug, axes=(1, 3), norm='ortho').real.astype(u.dtype)
    return jnp.einsum('bngc,gcd->bngd', f, w_fourier).reshape(b, n, C_WIDTH)


def hier_moe(h, w_group, w_expert, w1, w3, w2):
    t = h.shape[0]
    gl = jnp.einsum('td,dg->tg', h, w_group).astype(jnp.float32)
    gp = jax.nn.softmax(gl, axis=-1)
    g_sel = jnp.argmax(gl, axis=-1)
    g_w = jnp.take_along_axis(gp, g_sel[:, None], axis=-1)[:, 0]
    el = jnp.einsum('td,de->te', h, w_expert).astype(jnp.float32)
    el = el.reshape(t, N_GROUPS, EXPERTS_PER_GROUP)
    idx = jnp.broadcast_to(g_sel[:, None, None], (t, 1, EXPERTS_PER_GROUP))
    el_sel = jnp.take_along_axis(el, idx, axis=1)[:, 0]
    top_v, top_i = lax.top_k(el_sel, TOP_K)
    top_w = jax.nn.softmax(top_v, axis=-1) * g_w[:, None]
    expert_idx = g_sel[:, None] * EXPERTS_PER_GROUP + top_i
    combine = jnp.sum(jax.nn.one_hot(expert_idx, N_EXPERTS, dtype=jnp.float32)
                      * top_w[..., None], axis=1).astype(h.dtype)
    out = jnp.zeros_like(h)
    for e in range(N_EXPERTS):
        a = jax.nn.silu(h @ w1[e]) * (h @ w3[e])
        out = out + (a @ w2[e]) * combine[:, e:e + 1]
    return out


def lambda_init(layer_idx):
    return 0.8 - 0.6 * math.exp(-0.3 * layer_idx)


def encoder_layer(x, c, cos, sin, lam_init, rel_bias, w_ada, b_ada, g_norm_mix, w_in,
                  g_qa, g_ka, g_qb, g_kb, lam_qk, g_subln, w_fourier, w_out,
                  g_norm_ffn, w_group, w_expert, w1, w3, w2):
    b, n, d = x.shape
    mod = jnp.einsum('bd,de->be', jax.nn.silu(c), w_ada) + b_ada
    sh1, sc1, gt1, sh2, sc2, gt2 = jnp.split(mod[:, None, :], 6, axis=-1)

    hn = rms_norm(x, g_norm_mix) * (1 + sc1) + sh1
    proj = hn @ w_in
    cuts = [A_Q, A_Q + A_KV, A_Q + 2 * A_KV, A_Q + 2 * A_KV + B_WIDTH,
            A_Q + 2 * A_KV + 2 * B_WIDTH, A_Q + 2 * A_KV + 3 * B_WIDTH]
    qa, ka, va, qb, kb, vb, uc = jnp.split(proj, cuts, axis=-1)

    qa = apply_rope(rms_norm(qa.reshape(b, n, A_HEADS, HEAD_DIM), g_qa), cos, sin)
    ka = apply_rope(rms_norm(ka.reshape(b, n, A_KV_HEADS, HEAD_DIM), g_ka), cos, sin)
    va = va.reshape(b, n, A_KV_HEADS, HEAD_DIM)
    oa = gqa_attention(qa, ka, va)

    qb = rms_norm(qb.reshape(b, n, B_HEADS, 2, B_HALF), g_qb)
    kb = rms_norm(kb.reshape(b, n, B_HEADS, 2, B_HALF), g_kb)
    vb = vb.reshape(b, n, B_HEADS, HEAD_DIM)
    lq = lam_qk.astype(jnp.float32)
    lam = jnp.exp(jnp.sum(lq[0] * lq[1])) - jnp.exp(jnp.sum(lq[2] * lq[3])) + lam_init
    ob = diff_attention(qb, kb, vb, rel_bias, lam)
    ob = (rms_norm(ob, g_subln) * (1.0 - lam_init)).reshape(b, n, B_WIDTH)

    oc = fourier_mix(uc, w_fourier)

    mix = jnp.concatenate([oa, ob, oc], axis=-1) @ w_out
    x = x + gt1 * mix

    hn2 = rms_norm(x, g_norm_ffn) * (1 + sc2) + sh2
    ffn = hier_moe(hn2.reshape(b * n, d), w_group, w_expert, w1, w3, w2).reshape(b, n, d)
    return x + gt2 * ffn


def encoder_trunk(x, c, rel_bias, w_ada, b_ada, g_norm_mix, w_in, g_qa, g_ka, g_qb, g_kb,
                  lam_qk, g_subln, w_fourier, w_out, g_norm_ffn, w_group, w_expert, w1, w3, w2):
    cos, sin = axial_rope_tables(x.shape[1])
    for l in range(DEPTH):
        x = encoder_layer(x, c, cos, sin, lambda_init(l), rel_bias, w_ada[l], b_ada[l],
                          g_norm_mix[l], w_in[l], g_qa[l], g_ka[l], g_qb[l], g_kb[l],
                          lam_qk[l], g_subln[l], w_fourier[l], w_out[l], g_norm_ffn[l],
                          w_group[l], w_expert[l], w1[l], w3[l], w2[l])
    return x


def setup_inputs(seed: int = 0) -> dict:
    key = jax.random.key(seed)
    ks = jax.random.split(key, 26)
    D = D_MODEL

    def nrm(k, shape, scale):
        return jax.random.normal(k, shape, jnp.float32) * scale

    return {
        "x_prompt": nrm(ks[0], (BATCH, SEQ, D), 1.0),
        "x_sample": nrm(ks[1], (DEC_BATCH, DEC_SEQ, D), 1.0),
        "c_prompt": nrm(ks[2], (BATCH, D), 1.0),
        "c_sample": nrm(ks[3], (DEC_BATCH, D), 1.0),
        "rel_bias": nrm(ks[4], (NUM_BUCKETS, B_HEADS), 0.5),
        "w_ada": nrm(ks[5], (DEPTH, D, 6 * D), D ** -0.5),
        "b_ada": nrm(ks[6], (DEPTH, 6 * D), 0.02),
        "g_norm_mix": 1.0 + nrm(ks[7], (DEPTH, D), 0.02),
        "w_in": nrm(ks[8], (DEPTH, D, IN_WIDTH), D ** -0.5),
        "g_qa": 1.0 + nrm(ks[9], (DEPTH, HEAD_DIM), 0.02),
        "g_ka": 1.0 + nrm(ks[10], (DEPTH, HEAD_DIM), 0.02),
        "g_qb": 1.0 + nrm(ks[11], (DEPTH, B_HALF), 0.02),
        "g_kb": 1.0 + nrm(ks[12], (DEPTH, B_HALF), 0.02),
        "lam_qk": nrm(ks[13], (DEPTH, 4, B_HALF), 0.1),
        "g_subln": 1.0 + nrm(ks[14], (DEPTH, HEAD_DIM), 0.02),
        "w_fourier": nrm(ks[15], (DEPTH, C_GROUPS, C_GROUP_W, C_GROUP_W), C_GROUP_W ** -0.5),
        "w_out": nrm(ks[16], (DEPTH, MIX_WIDTH, D), MIX_WIDTH ** -0.5),
        "g_norm_ffn": 1.0 + nrm(ks[17], (DEPTH, D), 0.02),
        "w_group": nrm(ks[18], (DEPTH, D, N_GROUPS), D ** -0.5),
        "w_expert": nrm(ks[19], (DEPTH, D, N_EXPERTS), D ** -0.5),
        "w1": nrm(ks[20], (DEPTH, N_EXPERTS, D, D_FF_EXPERT), D ** -0.5),
        "w3": nrm(ks[21], (DEPTH, N_EXPERTS, D, D_FF_EXPERT), D ** -0.5),
        "w2": nrm(ks[22], (DEPTH, N_EXPERTS, D_FF_EXPERT, D), D_FF_EXPERT ** -0.5),
    }


def reference(x_prompt, x_sample, c_prompt, c_sample, rel_bias, w_ada, b_ada, g_norm_mix,
              w_in, g_qa, g_ka, g_qb, g_kb, lam_qk, g_subln, w_fourier, w_out, g_norm_ffn,
              w_group, w_expert, w1, w3, w2):
    y_prompt = encoder_trunk(x_prompt, c_prompt, rel_bias, w_ada, b_ada, g_norm_mix, w_in,
                             g_qa, g_ka, g_qb, g_kb, lam_qk, g_subln, w_fourier, w_out,
                             g_norm_ffn, w_group, w_expert, w1, w3, w2)
    y_sample = encoder_trunk(x_sample, c_sample, rel_bias, w_ada, b_ada, g_norm_mix, w_in,
                             g_qa, g_ka, g_qb, g_kb, lam_qk, g_subln, w_fourier, w_out,
                             g_norm_ffn, w_group, w_expert, w1, w3, w2)
    return (y_prompt, y_sample)
```

```python
import functools
import math

import jax
import jax.numpy as jnp
import numpy as np
from jax import lax
from jax.experimental import pallas as pl
from jax.experimental.pallas import tpu as pltpu

F32 = jnp.float32
BF16 = jnp.bfloat16

D_MODEL = 2048
HEAD_DIM = 128
A_HEADS = 8
A_KV_HEADS = 2
A_GROUP = A_HEADS // A_KV_HEADS
A_Q = A_HEADS * HEAD_DIM
A_KV = A_KV_HEADS * HEAD_DIM
B_HEADS = 4
B_HALF = HEAD_DIM // 2
B_WIDTH = B_HEADS * HEAD_DIM
C_WIDTH = 512
C_GROUPS = 4
C_GROUP_W = 128
IN_WIDTH = A_Q + 2 * A_KV + 3 * B_WIDTH + C_WIDTH
GRID_W = 64
ROPE_THETA = 10000.0
NUM_BUCKETS = 32
MAX_DISTANCE = 128
N_GROUPS = 4
EXPERTS_PER_GROUP = 4
N_EXPERTS = 16
D_FF_EXPERT = 512
EPS = 1e-6
LOG2E = 1.4426950408889634

_PAIRS = ((0, 1), (0, 2), (0, 3), (1, 2), (1, 3), (2, 3))
N_BUCKETS = N_GROUPS * len(_PAIRS)

V7X_VMEM_BYTES = 64 * 1024 * 1024
VMEM_LIMIT = V7X_VMEM_BYTES - 8 * 1024 * 1024
LANES = 128
NEG_BIG = -1e30


def _params(sem, **kw):
    return pltpu.CompilerParams(dimension_semantics=sem, vmem_limit_bytes=VMEM_LIMIT, **kw)


def _pick(total, pref):
    t = min(pref, total)
    while total % t:
        t //= 2
    return t


def _ada_kernel(c_ref, w_ref, b_ref, o_ref):
    h = jax.nn.silu(c_ref[...]).astype(BF16)
    o_ref[0] = jnp.dot(h, w_ref[0].astype(BF16), preferred_element_type=F32) + b_ref[0]


def _ada_call(c_all, w_ada, b_ada):
    depth, d, e = w_ada.shape
    bp = c_all.shape[0]
    tn = _pick(e, 1024)
    return pl.pallas_call(
        _ada_kernel,
        grid=(depth, e // tn),
        in_specs=[
            pl.BlockSpec((bp, d), lambda l, j: (0, 0)),
            pl.BlockSpec((1, d, tn), lambda l, j: (l, 0, j)),
            pl.BlockSpec((1, 1, tn), lambda l, j: (l, 0, j)),
        ],
        out_specs=pl.BlockSpec((1, bp, tn), lambda l, j: (l, 0, j)),
        out_shape=jax.ShapeDtypeStruct((depth, bp, e), F32),
        compiler_params=_params(("arbitrary", "arbitrary")),
        name="ada",
    )(c_all, w_ada, b_ada.reshape(depth, 1, e))


class _Layout:
    def __init__(self, b0, n0, b1, n1):
        self.b = (b0, b1)
        self.n = (n0, n1)
        self.t0 = b0 * n0
        self.t = b0 * n0 + b1 * n1

    def batch_of_tile(self, i, tm):
        tok = i * tm
        return jnp.where(tok < self.t0, tok // self.n[0], self.b[0] + (tok - self.t0) // self.n[1])

    def pos_block_of_tile(self, i, tm):
        tok = i * tm
        pos = jnp.where(tok < self.t0, tok % self.n[0], (tok - self.t0) % self.n[1])
        return pos // tm


def _head_norm(z, g):
    return z * lax.rsqrt(jnp.mean(z * z, axis=-1, keepdims=True) + EPS) * g


def _half_norm(z, g, lo):
    zz = z * z
    s_lo = jnp.sum(jnp.where(lo, zz, 0.0), axis=-1, keepdims=True)
    s_hi = jnp.sum(jnp.where(lo, 0.0, zz), axis=-1, keepdims=True)
    inv = jnp.where(lo, lax.rsqrt(s_lo / B_HALF + EPS), lax.rsqrt(s_hi / B_HALF + EPS))
    return z * inv * g


def _rope(z, c, s_signed, even):
    partner = jnp.where(even, pltpu.roll(z, LANES - 1, 1), pltpu.roll(z, 1, 1))
    return z * c + partner * s_signed


def _proj_kernel(x_ref, mod_ref, gn_ref, w_ref, gqa_ref, gka_ref, gqb_ref, gkb_ref, cos_ref, sin_ref,
                 dft_ref, qa_ref, ka_ref, va_ref, qb_ref, kb_ref, vb_ref, p_ref, q_ref):
    x = x_ref[...]
    y = x * lax.rsqrt(jnp.mean(x * x, axis=-1, keepdims=True) + EPS) * gn_ref[...]
    hn = (y * (1.0 + mod_ref[0, 1:2, :]) + mod_ref[0, 0:1, :]).astype(BF16)

    def seg(a, b):
        return jnp.dot(hn, w_ref[:, a:b], preferred_element_type=F32)

    tm = x.shape[0]
    lane = lax.broadcasted_iota(jnp.int32, (tm, LANES), 1)
    even = (lane % 2) == 0
    lo = lane < B_HALF
    cos = cos_ref[...]
    sin = sin_ref[...]

    scale_a = HEAD_DIM ** -0.5 * LOG2E
    z = seg(0, A_Q)
    for h in range(A_HEADS):
        sl = slice(h * HEAD_DIM, (h + 1) * HEAD_DIM)
        qa_ref[:, sl] = (_rope(_head_norm(z[:, sl], gqa_ref[...]), cos, sin, even) * scale_a).astype(BF16)
    off = A_Q
    z = seg(off, off + A_KV)
    for h in range(A_KV_HEADS):
        sl = slice(h * HEAD_DIM, (h + 1) * HEAD_DIM)
        ka_ref[:, sl] = _rope(_head_norm(z[:, sl], gka_ref[...]), cos, sin, even).astype(BF16)
    off += A_KV
    va_ref[...] = seg(off, off + A_KV).astype(BF16)
    off += A_KV

    scale_b = B_HALF ** -0.5 * LOG2E
    z = seg(off, off + B_WIDTH)
    for h in range(B_HEADS):
        sl = slice(h * HEAD_DIM, (h + 1) * HEAD_DIM)
        qb_ref[:, sl] = (_half_norm(z[:, sl], gqb_ref[...], lo) * scale_b).astype(BF16)
    off += B_WIDTH
    z = seg(off, off + B_WIDTH)
    for h in range(B_HEADS):
        sl = slice(h * HEAD_DIM, (h + 1) * HEAD_DIM)
        kb_ref[:, sl] = _half_norm(z[:, sl], gkb_ref[...], lo).astype(BF16)
    off += B_WIDTH
    vb_ref[...] = seg(off, off + B_WIDTH).astype(BF16)
    off += B_WIDTH

    z = seg(off, off + C_WIDTH).astype(BF16)
    for g in range(C_GROUPS):
        sl = slice(g * C_GROUP_W, (g + 1) * C_GROUP_W)
        pq = jnp.dot(z[:, sl], dft_ref[...], preferred_element_type=F32)
        p_ref[:, sl] = pq[:, :C_GROUP_W].astype(BF16)
        q_ref[:, sl] = pq[:, C_GROUP_W:].astype(BF16)


def _proj_call(lay, x, mod3, gn, w_in, gqa, gka, gqb, gkb, cos_e, sin_e, dft_c, tm):
    t, d = x.shape
    row = lambda i: (i, 0)
    const = lambda i: (0, 0)
    widths = (A_Q, A_KV, A_KV, B_WIDTH, B_WIDTH, B_WIDTH, C_WIDTH, C_WIDTH)
    return pl.pallas_call(
        _proj_kernel,
        grid=(t // tm,),
        in_specs=[
            pl.BlockSpec((tm, d), row),
            pl.BlockSpec((1, 6, d), lambda i: (lay.batch_of_tile(i, tm), 0, 0)),
            pl.BlockSpec((1, d), const),
            pl.BlockSpec((d, IN_WIDTH), const),
            pl.BlockSpec((1, LANES), const),
            pl.BlockSpec((1, LANES), const),
            pl.BlockSpec((1, LANES), const),
            pl.BlockSpec((1, LANES), const),
            pl.BlockSpec((tm, LANES), lambda i: (lay.pos_block_of_tile(i, tm), 0)),
            pl.BlockSpec((tm, LANES), lambda i: (lay.pos_block_of_tile(i, tm), 0)),
            pl.BlockSpec((C_GROUP_W, 2 * C_GROUP_W), const),
        ],
        out_specs=[pl.BlockSpec((tm, w), row) for w in widths],
        out_shape=[jax.ShapeDtypeStruct((t, w), BF16) for w in widths],
        compiler_params=_params(("arbitrary",)),
        name="proj",
    )(x, mod3, gn, w_in, gqa, gka, gqb, gkb, cos_e, sin_e, dft_c)


def _softmax_step(s, vc, m_ref, l_ref, acc_ref, shift):
    m_prev = m_ref[...]
    m_new = jnp.maximum(m_prev, jnp.max(s, axis=1, keepdims=True) + shift)
    alpha = jnp.exp2(m_prev - m_new)
    p = jnp.exp2(s - jnp.tile(m_new - shift, (1, s.shape[1] // LANES)))
    l_ref[...] = alpha * l_ref[...] + jnp.sum(p, axis=1, keepdims=True)
    acc_ref[...] = alpha * acc_ref[...] + jnp.dot(p.astype(BF16), vc, preferred_element_type=F32)
    m_ref[...] = m_new


_NT = (((1,), (1,)), ((), ()))


def _attn_a_kernel(q_ref, k_ref, v_ref, o_ref, qs_ref, m_ref, l_ref, acc_ref, *, tq, tk, n):
    for g in range(A_GROUP):
        qs_ref[g * tq:(g + 1) * tq, :] = q_ref[:, g * HEAD_DIM:(g + 1) * HEAD_DIM]
    m_ref[...] = jnp.full(m_ref.shape, NEG_BIG, F32)
    l_ref[...] = jnp.zeros(l_ref.shape, F32)
    acc_ref[...] = jnp.zeros(acc_ref.shape, F32)

    def body(j, carry):
        rows = pl.ds(pl.multiple_of(j * tk, tk), tk)
        s = lax.dot_general(qs_ref[...], k_ref[rows, :], _NT, preferred_element_type=F32)
        _softmax_step(s, v_ref[rows, :], m_ref, l_ref, acc_ref, 0.0)
        return carry

    lax.fori_loop(0, n // tk, body, 0)
    o = acc_ref[...] / l_ref[...]
    for g in range(A_GROUP):
        o_ref[:, g * HEAD_DIM:(g + 1) * HEAD_DIM] = o[g * tq:(g + 1) * tq].astype(o_ref.dtype)


def _attn_a_call(qa, ka, va, tok0, b, n, tq, tk):
    qblk0 = tok0 // tq
    kblk0 = tok0 // n
    nq = n // tq
    rows = A_GROUP * tq
    return pl.pallas_call(
        functools.partial(_attn_a_kernel, tq=tq, tk=tk, n=n),
        grid=(b, A_KV_HEADS, nq),
        in_specs=[
            pl.BlockSpec((tq, A_GROUP * HEAD_DIM), lambda bi, kv, qi: (qblk0 + bi * nq + qi, kv)),
            pl.BlockSpec((n, HEAD_DIM), lambda bi, kv, qi: (kblk0 + bi, kv)),
            pl.BlockSpec((n, HEAD_DIM), lambda bi, kv, qi: (kblk0 + bi, kv)),
        ],
        out_specs=pl.BlockSpec((tq, A_GROUP * HEAD_DIM), lambda bi, kv, qi: (bi * nq + qi, kv)),
        out_shape=jax.ShapeDtypeStruct((b * n, A_Q), BF16),
        scratch_shapes=[
            pltpu.VMEM((rows, HEAD_DIM), BF16),
            pltpu.VMEM((rows, LANES), F32),
            pltpu.VMEM((rows, LANES), F32),
            pltpu.VMEM((rows, HEAD_DIM), F32),
        ],
        compiler_params=_params(("arbitrary", "arbitrary", "arbitrary")),
        name="attn_a",
    )(qa, ka, va)


def _attn_b_kernel(far_ref, q_ref, k_ref, v_ref, band_ref, lam_ref, gs_ref, o_ref,
                   qs_ref, m_ref, l_ref, acc_ref, *, t, nchunks, lam_init):
    h = pl.program_id(1)
    i = pl.program_id(2)
    q = q_ref[...]
    lo = lax.broadcasted_iota(jnp.int32, q.shape, 1) < B_HALF
    zero = jnp.zeros_like(q)
    qs_ref[0:t, :] = jnp.where(lo, q, zero)
    qs_ref[t:2 * t, :] = jnp.where(lo, zero, q)
    m_ref[...] = jnp.full(m_ref.shape, NEG_BIG, F32)
    l_ref[...] = jnp.zeros(l_ref.shape, F32)
    acc_ref[...] = jnp.zeros(acc_ref.shape, F32)

    def scores(j):
        rows = pl.ds(pl.multiple_of(j * t, t), t)
        s = lax.dot_general(qs_ref[...], k_ref[rows, :], _NT, preferred_element_type=F32)
        return s, v_ref[rows, :]

    def far_step(shift):
        def body(j, carry):
            s, vc = scores(j)
            _softmax_step(s, vc, m_ref, l_ref, acc_ref, shift)
            return carry
        return body

    lax.fori_loop(0, jnp.maximum(i - 1, 0), far_step(far_ref[h, 0]), 0)
    lax.fori_loop(jnp.minimum(i + 2, nchunks), nchunks, far_step(far_ref[h, 1]), 0)

    for d in (-1, 0, 1):
        j = i + d

        @pl.when(jnp.logical_and(j >= 0, j < nchunks))
        def _():
            s, vc = scores(j)
            bias = band_ref[0, d + 1]
            s = s + jnp.concatenate([bias, bias], axis=0)
            _softmax_step(s, vc, m_ref, l_ref, acc_ref, 0.0)

    o = acc_ref[...] / l_ref[...]
    lq = lam_ref[...]
    lam = (jnp.exp(jnp.sum(lq[0:1] * lq[1:2], axis=-1, keepdims=True))
           - jnp.exp(jnp.sum(lq[2:3] * lq[3:4], axis=-1, keepdims=True)) + lam_init)
    ob = o[0:t] - lam * o[t:2 * t]
    o_ref[...] = (_head_norm(ob, gs_ref[...]) * (1.0 - lam_init)).astype(o_ref.dtype)


def _attn_b_call(far, qb, kb, vb, band, lam_qk, g_subln, tok0, b, n, t, lam_init):
    qblk0 = tok0 // t
    kblk0 = tok0 // n
    nq = n // t
    grid_spec = pltpu.PrefetchScalarGridSpec(
        num_scalar_prefetch=1,
        grid=(b, B_HEADS, nq),
        in_specs=[
            pl.BlockSpec((t, HEAD_DIM), lambda bi, h, qi, far: (qblk0 + bi * nq + qi, h)),
            pl.BlockSpec((n, HEAD_DIM), lambda bi, h, qi, far: (kblk0 + bi, h)),
            pl.BlockSpec((n, HEAD_DIM), lambda bi, h, qi, far: (kblk0 + bi, h)),
            pl.BlockSpec((1, 3, t, t), lambda bi, h, qi, far: (h, 0, 0, 0)),
            pl.BlockSpec((4, B_HALF), lambda bi, h, qi, far: (0, 0)),
            pl.BlockSpec((1, LANES), lambda bi, h, qi, far: (0, 0)),
        ],
        out_specs=pl.BlockSpec((t, HEAD_DIM), lambda bi, h, qi, far: (bi * nq + qi, h)),
        scratch_shapes=[
            pltpu.VMEM((2 * t, HEAD_DIM), BF16),
            pltpu.VMEM((2 * t, LANES), F32),
            pltpu.VMEM((2 * t, LANES), F32),
            pltpu.VMEM((2 * t, HEAD_DIM), F32),
        ],
    )
    return pl.pallas_call(
        functools.partial(_attn_b_kernel, t=t, nchunks=nq, lam_init=lam_init),
        grid_spec=grid_spec,
        out_shape=jax.ShapeDtypeStruct((b * n, B_WIDTH), BF16),
        compiler_params=_params(("arbitrary", "arbitrary", "arbitrary")),
        name="attn_b",
    )(far, qb, kb, vb, band, lam_qk, g_subln)


def _t5_bucket(rel):
    nb = NUM_BUCKETS // 2
    ret = (rel > 0).astype(jnp.int32) * nb
    n = jnp.abs(rel)
    max_exact = nb // 2
    nf = jnp.maximum(n, 1).astype(F32)
    large = max_exact + (jnp.log(nf / max_exact) / math.log(MAX_DISTANCE / max_exact)
                         * (nb - max_exact)).astype(jnp.int32)
    large = jnp.minimum(large, nb - 1)
    return ret + jnp.where(n < max_exact, n, large)


def _bias_tables(rel_bias, t, n_max):
    qi = jnp.arange(t, dtype=jnp.int32)[None, :, None]
    ki = jnp.arange(t, dtype=jnp.int32)[None, None, :]
    d = jnp.arange(-1, 2, dtype=jnp.int32)[:, None, None]
    band = jnp.take(rel_bias, _t5_bucket(d * t + ki - qi), axis=0).astype(F32)
    band = band.transpose(3, 0, 1, 2) * LOG2E
    far = jnp.take(rel_bias, _t5_bucket(jnp.array([-n_max, n_max], jnp.int32)), axis=0).astype(F32)
    return band, far.T * LOG2E


def _dft_gen_kernel(ck_ref, sk_ref, ca_ref, sa_ref, c_ref, s_ref):
    ck = ck_ref[0]
    sk = sk_ref[0]
    ca = ca_ref[...]
    sa = sa_ref[...]
    c_ref[...] = (ck * ca - sk * sa).astype(c_ref.dtype)
    s_ref[...] = (sk * ca + ck * sa).astype(s_ref.dtype)


def _dft_tables(n, tk):
    t = jnp.arange(n, dtype=jnp.int32)[None, :]
    k0 = (jnp.arange(n // tk, dtype=jnp.int32) * tk)[:, None]
    a = jnp.arange(tk, dtype=jnp.int32)[:, None]
    ang_k = ((k0 * t) % n).astype(F32) * (2.0 * math.pi / n)
    ang_a = ((a * t) % n).astype(F32) * (2.0 * math.pi / n)
    scale = n ** -0.5
    ck = (jnp.cos(ang_k) * scale).reshape(n // tk, 1, n)
    sk = (jnp.sin(ang_k) * scale).reshape(n // tk, 1, n)
    return pl.pallas_call(
        _dft_gen_kernel,
        grid=(n // tk,),
        in_specs=[
            pl.BlockSpec((1, 1, n), lambda i: (i, 0, 0)),
            pl.BlockSpec((1, 1, n), lambda i: (i, 0, 0)),
            pl.BlockSpec((tk, n), lambda i: (0, 0)),
            pl.BlockSpec((tk, n), lambda i: (0, 0)),
        ],
        out_specs=[pl.BlockSpec((tk, n), lambda i: (i, 0)), pl.BlockSpec((tk, n), lambda i: (i, 0))],
        out_shape=[jax.ShapeDtypeStruct((n, n), BF16), jax.ShapeDtypeStruct((n, n), BF16)],
        compiler_params=_params(("arbitrary",)),
        name="dft_gen",
    )(ck, sk, jnp.cos(ang_a), jnp.sin(ang_a))


def _fourier_kernel(c_ref, s_ref, p_ref, q_ref, w_ref, o_ref):
    f = (jnp.dot(c_ref[...], p_ref[...], preferred_element_type=F32)
         - jnp.dot(s_ref[...], q_ref[...], preferred_element_type=F32)).astype(BF16)
    for g in range(C_GROUPS):
        sl = slice(g * C_GROUP_W, (g + 1) * C_GROUP_W)
        o_ref[:, sl] = jnp.dot(f[:, sl], w_ref[g], preferred_element_type=F32).astype(o_ref.dtype)


def _fourier_call(cmat, smat, p, q, wf, tok0, b, n, tf):
    kblk0 = tok0 // n
    nk = n // tf
    return pl.pallas_call(
        _fourier_kernel,
        grid=(b, nk),
        in_specs=[
            pl.BlockSpec((tf, n), lambda bi, kt: (kt, 0)),
            pl.BlockSpec((tf, n), lambda bi, kt: (kt, 0)),
            pl.BlockSpec((n, C_WIDTH), lambda bi, kt: (kblk0 + bi, 0)),
            pl.BlockSpec((n, C_WIDTH), lambda bi, kt: (kblk0 + bi, 0)),
            pl.BlockSpec((C_GROUPS, C_GROUP_W, C_GROUP_W), lambda bi, kt: (0, 0, 0)),
        ],
        out_specs=pl.BlockSpec((tf, C_WIDTH), lambda bi, kt: (bi * nk + kt, 0)),
        out_shape=jax.ShapeDtypeStruct((b * n, C_WIDTH), BF16),
        compiler_params=_params(("arbitrary", "arbitrary")),
        name="fourier",
    )(cmat, smat, p, q, wf)


def _out_kernel(oa_ref, ob_ref, oc_ref, x_ref, mod_ref, w_ref, gn_ref, wr_ref,
                x1_ref, hn_ref, ids_ref, wts_ref):
    mix = (jnp.dot(oa_ref[...], w_ref[0:A_Q, :], preferred_element_type=F32)
           + jnp.dot(ob_ref[...], w_ref[A_Q:A_Q + B_WIDTH, :], preferred_element_type=F32)
           + jnp.dot(oc_ref[...], w_ref[A_Q + B_WIDTH:, :], preferred_element_type=F32))
    x1 = x_ref[...] + mod_ref[0, 2:3, :] * mix
    x1_ref[...] = x1
    y = x1 * lax.rsqrt(jnp.mean(x1 * x1, axis=-1, keepdims=True) + EPS) * gn_ref[...]
    hn = y * (1.0 + mod_ref[0, 4:5, :]) + mod_ref[0, 3:4, :]
    hn_ref[...] = hn

    logits = jnp.dot(hn.astype(BF16), wr_ref[...], preferred_element_type=F32)
    lane = lax.broadcasted_iota(jnp.int32, logits.shape, 1)
    big = jnp.int32(LANES)
    is_g = lane < N_GROUPS
    gmax = jnp.max(jnp.where(is_g, logits, -jnp.inf), axis=-1, keepdims=True)
    g_sel = jnp.min(jnp.where(jnp.logical_and(is_g, logits == gmax), lane, big), axis=-1, keepdims=True)
    g_w = 1.0 / jnp.sum(jnp.where(is_g, jnp.exp(logits - gmax), 0.0), axis=-1, keepdims=True)
    lo_lane = N_GROUPS + g_sel * EXPERTS_PER_GROUP
    in_g = jnp.logical_and(lane >= lo_lane, lane < lo_lane + EXPERTS_PER_GROUP)
    v0 = jnp.max(jnp.where(in_g, logits, -jnp.inf), axis=-1, keepdims=True)
    i0 = jnp.min(jnp.where(jnp.logical_and(in_g, logits == v0), lane, big), axis=-1, keepdims=True)
    rest = jnp.logical_and(in_g, lane != i0)
    v1 = jnp.max(jnp.where(rest, logits, -jnp.inf), axis=-1, keepdims=True)
    i1 = jnp.min(jnp.where(jnp.logical_and(rest, logits == v1), lane, big), axis=-1, keepdims=True)
    e1 = jnp.exp(v1 - v0)
    w0 = g_w / (1.0 + e1)
    w1 = g_w * e1 / (1.0 + e1)
    swap = i1 < i0
    ea = jnp.where(swap, i1, i0) - N_GROUPS
    eb = jnp.where(swap, i0, i1) - N_GROUPS
    wa = jnp.where(swap, w1, w0)
    wb = jnp.where(swap, w0, w1)
    ml = lax.broadcasted_iota(jnp.int32, ids_ref.shape, 1)
    ids_ref[...] = jnp.where(ml == 0, ea, jnp.where(ml == 1, eb, 0))
    wts_ref[...] = jnp.where(ml == 0, wa, jnp.where(ml == 1, wb, 0.0))


def _out_call(lay, oa, ob, oc, x, mod3, w_out, gn, w_router, tm):
    t, d = x.shape
    row = lambda i: (i, 0)
    const = lambda i: (0, 0)
    meta = 8
    return pl.pallas_call(
        _out_kernel,
        grid=(t // tm,),
        in_specs=[
            pl.BlockSpec((tm, A_Q), row),
            pl.BlockSpec((tm, B_WIDTH), row),
            pl.BlockSpec((tm, C_WIDTH), row),
            pl.BlockSpec((tm, d), row),
            pl.BlockSpec((1, 6, d), lambda i: (lay.batch_of_tile(i, tm), 0, 0)),
            pl.BlockSpec((d, d), const),
            pl.BlockSpec((1, d), const),
            pl.BlockSpec((d, LANES), const),
        ],
        out_specs=[pl.BlockSpec((tm, d), row), pl.BlockSpec((tm, d), row),
                   pl.BlockSpec((tm, meta), row), pl.BlockSpec((tm, meta), row)],
        out_shape=[jax.ShapeDtypeStruct((t, d), F32), jax.ShapeDtypeStruct((t, d), F32),
                   jax.ShapeDtypeStruct((t, meta), jnp.int32), jax.ShapeDtypeStruct((t, meta), F32)],
        compiler_params=_params(("arbitrary",)),
        name="out_proj",
    )(oa, ob, oc, x, mod3, w_out, gn, w_router)


def _route_plan(ids, tm):
    t = ids.shape[0]
    ea = ids[:, 0]
    eb = ids[:, 1]
    grp = ea // EXPERTS_PER_GROUP
    la = ea % EXPERTS_PER_GROUP
    lb = eb % EXPERTS_PER_GROUP
    pair = la * (2 * EXPERTS_PER_GROUP - 1 - la) // 2 + (lb - la - 1)
    bucket = grp * len(_PAIRS) + pair
    onehot = (bucket[:, None] == jnp.arange(N_BUCKETS, dtype=jnp.int32)[None, :]).astype(jnp.int32)
    csum = jnp.cumsum(onehot, axis=0)
    counts = csum[-1]
    rank = jnp.take_along_axis(csum, bucket[:, None], axis=1)[:, 0] - 1
    tiles = (counts + tm - 1) // tm
    tile_end = jnp.cumsum(tiles)
    tile_start = tile_end - tiles
    pos = jnp.take(tile_start, bucket) * tm + rank
    n_tiles = t // tm + N_BUCKETS
    src = jnp.zeros((n_tiles * tm,), jnp.int32).at[pos].set(jnp.arange(t, dtype=jnp.int32))
    tile_ids = jnp.arange(n_tiles, dtype=jnp.int32)
    used = tile_end[-1]
    tile_bucket = jnp.searchsorted(tile_end, jnp.minimum(tile_ids, used - 1), side="right").astype(jnp.int32)
    pairs = np.array(_PAIRS, np.int32)
    tgrp = tile_bucket // len(_PAIRS)
    tpair = tile_bucket % len(_PAIRS)
    tile_ea = tgrp * EXPERTS_PER_GROUP + jnp.take(jnp.asarray(pairs[:, 0]), tpair)
    tile_eb = tgrp * EXPERTS_PER_GROUP + jnp.take(jnp.asarray(pairs[:, 1]), tpair)
    rows_left = jnp.take(counts, tile_bucket) - (tile_ids - jnp.take(tile_start, tile_bucket)) * tm
    tile_rows = jnp.where(tile_ids < used, jnp.clip(rows_left, 0, tm), 0).astype(jnp.int32)
    return pos.astype(jnp.int32), src, tile_ea.astype(jnp.int32), tile_eb.astype(jnp.int32), tile_rows, used


def _row_copy(src_hbm, row, buf, slot, r, sem):
    return pltpu.make_async_copy(src_hbm.at[pl.ds(row, 1), :], buf.at[slot, pl.ds(r, 1), :], sem.at[slot])


def _gather_start(idx_ref, base, count, src_hbm, buf, slot, sem):
    def body(r, carry):
        _row_copy(src_hbm, idx_ref[base + r], buf, slot, r, sem).start()
        return carry
    lax.fori_loop(0, count, body, 0)


def _gather_wait(count, src_hbm, buf, slot, sem):
    def body(r, carry):
        _row_copy(src_hbm, 0, buf, slot, r, sem).wait()
        return carry
    lax.fori_loop(0, count, body, 0)


def _moe_kernel(src_ref, ea_ref, eb_ref, rows_ref, h_hbm, w1a_ref, w3a_ref, w2a_ref, w1b_ref, w3b_ref, w2b_ref,
                wts_ref, y_ref, buf, sem, *, tm):
    i = pl.program_id(0)
    nt = pl.num_programs(0)
    slot = i % 2

    def count(tile):
        return jnp.where(rows_ref[tile] > 0, tm, 0)

    @pl.when(i == 0)
    def _():
        _gather_start(src_ref, 0, count(0), h_hbm, buf, 0, sem)

    @pl.when(i + 1 < nt)
    def _():
        _gather_start(src_ref, (i + 1) * tm, count(i + 1), h_hbm, buf, 1 - slot, sem)

    _gather_wait(count(i), h_hbm, buf, slot, sem)

    @pl.when(rows_ref[i] == 0)
    def _():
        y_ref[...] = jnp.zeros(y_ref.shape, y_ref.dtype)

    @pl.when(rows_ref[i] > 0)
    def _():
        h = buf[slot].astype(BF16)

        def expert(w1_ref, w3_ref, w2_ref):
            a = (jax.nn.silu(jnp.dot(h, w1_ref[0], preferred_element_type=F32))
                 * jnp.dot(h, w3_ref[0], preferred_element_type=F32))
            return jnp.dot(a.astype(BF16), w2_ref[0], preferred_element_type=F32)

        w = wts_ref[...]
        y_ref[...] = (expert(w1a_ref, w3a_ref, w2a_ref) * w[:, 0:1]
                      + expert(w1b_ref, w3b_ref, w2b_ref) * w[:, 1:2])


def _moe_call(src, tile_ea, tile_eb, tile_rows, hn, w1, w3, w2, wts_sorted, tm):
    n_tiles = tile_ea.shape[0]
    d = hn.shape[1]
    f = w1.shape[2]
    wa = lambda i, src, ea, eb, rows: (ea[i], 0, 0)
    wb = lambda i, src, ea, eb, rows: (eb[i], 0, 0)
    grid_spec = pltpu.PrefetchScalarGridSpec(
        num_scalar_prefetch=4,
        grid=(n_tiles,),
        in_specs=[
            pl.BlockSpec(memory_space=pl.ANY),
            pl.BlockSpec((1, d, f), wa), pl.BlockSpec((1, d, f), wa), pl.BlockSpec((1, f, d), wa),
            pl.BlockSpec((1, d, f), wb), pl.BlockSpec((1, d, f), wb), pl.BlockSpec((1, f, d), wb),
            pl.BlockSpec((tm, 8), lambda i, *_: (i, 0)),
        ],
        out_specs=pl.BlockSpec((tm, d), lambda i, *_: (i, 0)),
        scratch_shapes=[pltpu.VMEM((2, tm, d), F32), pltpu.SemaphoreType.DMA((2,))],
    )
    return pl.pallas_call(
        functools.partial(_moe_kernel, tm=tm),
        grid_spec=grid_spec,
        out_shape=jax.ShapeDtypeStruct((n_tiles * tm, d), F32),
        compiler_params=_params(("arbitrary",)),
        name="moe",
    )(src, tile_ea, tile_eb, tile_rows, hn, w1, w3, w2, w1, w3, w2, wts_sorted)


def _combine_kernel(pos_ref, y_hbm, x1_ref, mod_ref, o_ref, buf, sem, *, tm):
    i = pl.program_id(0)
    nt = pl.num_programs(0)
    slot = i % 2

    @pl.when(i == 0)
    def _():
        _gather_start(pos_ref, 0, tm, y_hbm, buf, 0, sem)

    @pl.when(i + 1 < nt)
    def _():
        _gather_start(pos_ref, (i + 1) * tm, tm, y_hbm, buf, 1 - slot, sem)

    _gather_wait(tm, y_hbm, buf, slot, sem)
    o_ref[...] = x1_ref[...] + mod_ref[0, 5:6, :] * buf[slot]


def _combine_call(lay, pos, y_sorted, x1, mod3, tm):
    t, d = x1.shape
    grid_spec = pltpu.PrefetchScalarGridSpec(
        num_scalar_prefetch=1,
        grid=(t // tm,),
        in_specs=[
            pl.BlockSpec(memory_space=pl.ANY),
            pl.BlockSpec((tm, d), lambda i, pos: (i, 0)),
            pl.BlockSpec((1, 6, d), lambda i, pos: (lay.batch_of_tile(i, tm), 0, 0)),
        ],
        out_specs=pl.BlockSpec((tm, d), lambda i, pos: (i, 0)),
        scratch_shapes=[pltpu.VMEM((2, tm, d), F32), pltpu.SemaphoreType.DMA((2,))],
    )
    return pl.pallas_call(
        functools.partial(_combine_kernel, tm=tm),
        grid_spec=grid_spec,
        out_shape=jax.ShapeDtypeStruct((t, d), F32),
        compiler_params=_params(("arbitrary",)),
        name="combine",
    )(pos, y_sorted, x1, mod3)


def _rope_tables(n):
    n_rows = n // GRID_W
    rows = jnp.repeat(jnp.arange(n_rows), GRID_W).astype(F32)
    cols = jnp.tile(jnp.arange(GRID_W), n_rows).astype(F32)
    half = HEAD_DIM // 2
    inv = ROPE_THETA ** (-jnp.arange(0, half, 2, dtype=F32) / half)
    ang = jnp.concatenate([rows[:, None] * inv, cols[:, None] * inv], axis=-1)
    sign = jnp.tile(jnp.array([-1.0, 1.0], F32), half)
    return jnp.repeat(jnp.cos(ang), 2, axis=-1), jnp.repeat(jnp.sin(ang), 2, axis=-1) * sign


def _lambda_init(layer_idx):
    return 0.8 - 0.6 * math.exp(-0.3 * layer_idx)


def _tiled_gain(g):
    return jnp.tile(g, LANES // g.shape[-1]).reshape(1, LANES).astype(F32)


def kernel(x_prompt, x_sample, c_prompt, c_sample, rel_bias, w_ada, b_ada, g_norm_mix, w_in, g_qa, g_ka,
           g_qb, g_kb, lam_qk, g_subln, w_fourier, w_out, g_norm_ffn, w_group, w_expert, w1, w3, w2):
    b0, n0, d = x_prompt.shape
    b1, n1, _ = x_sample.shape
    depth = w_in.shape[0]
    lay = _Layout(b0, n0, b1, n1)
    trunks = ((0, b0, n0), (lay.t0, b1, n1))
    n_max = max(n0, n1)

    tm = _pick(math.gcd(n0, n1), 512)
    tq_a = _pick(math.gcd(n0, n1), 256)
    tk_a = _pick(math.gcd(n0, n1), 512)
    t_b = _pick(math.gcd(n0, n1), 512)
    tf = _pick(math.gcd(n0, n1), 128)
    tm_moe = 256

    x = jnp.concatenate([x_prompt.reshape(b0 * n0, d), x_sample.reshape(b1 * n1, d)], axis=0)
    nb = b0 + b1
    bp = -(-nb // 8) * 8
    c_all = jnp.zeros((bp, d), F32).at[:nb].set(jnp.concatenate([c_prompt, c_sample], axis=0))
    mod = _ada_call(c_all, w_ada, b_ada).reshape(depth, bp, 6, d)

    cos_e, sin_e = _rope_tables(n_max)
    band, far = _bias_tables(rel_bias, t_b, n_max)
    dft = {n: _dft_tables(n, _pick(n, 128)) for n in sorted({n0, n1})}
    cidx = jnp.arange(C_GROUP_W, dtype=jnp.int32)
    ang_c = ((cidx[:, None] * cidx[None, :]) % C_GROUP_W).astype(F32) * (2.0 * math.pi / C_GROUP_W)
    dft_c = (jnp.concatenate([jnp.cos(ang_c), jnp.sin(ang_c)], axis=1) * C_GROUP_W ** -0.5).astype(BF16)

    for l in range(depth):
        mod3 = mod[l]
        lam_init = _lambda_init(l)
        qa, ka, va, qb, kb, vb, p, q = _proj_call(
            lay, x, mod3, g_norm_mix[l].reshape(1, d), w_in[l].astype(BF16),
            _tiled_gain(g_qa[l]), _tiled_gain(g_ka[l]), _tiled_gain(g_qb[l]), _tiled_gain(g_kb[l]),
            cos_e, sin_e, dft_c, tm)

        wf = w_fourier[l].astype(BF16)
        oa, ob, oc = [], [], []
        for tok0, b, n in trunks:
            oa.append(_attn_a_call(qa, ka, va, tok0, b, n, tq_a, tk_a))
            ob.append(_attn_b_call(far, qb, kb, vb, band, lam_qk[l], _tiled_gain(g_subln[l]),
                                   tok0, b, n, t_b, lam_init))
            oc.append(_fourier_call(dft[n][0], dft[n][1], p, q, wf, tok0, b, n, tf))
        oa = jnp.concatenate(oa, axis=0)
        ob = jnp.concatenate(ob, axis=0)
        oc = jnp.concatenate(oc, axis=0)

        w_router = jnp.zeros((d, LANES), F32).at[:, :N_GROUPS].set(w_group[l])
        w_router = w_router.at[:, N_GROUPS:N_GROUPS + N_EXPERTS].set(w_expert[l]).astype(BF16)
        x1, hn2, ids, wts = _out_call(lay, oa, ob, oc, x, mod3, w_out[l].astype(BF16),
                                      g_norm_ffn[l].reshape(1, d), w_router, tm)

        pos, src, tile_ea, tile_eb, tile_rows, _ = _route_plan(ids, tm_moe)
        wts_sorted = jnp.zeros((src.shape[0], 8), F32).at[pos].set(wts)
        y_sorted = _moe_call(src, tile_ea, tile_eb, tile_rows, hn2, w1[l].astype(BF16), w3[l].astype(BF16),
                             w2[l].astype(BF16), wts_sorted, tm_moe)
        x = _combine_call(lay, pos, y_sorted, x1, mod3, tm)

    return (x[:lay.t0].reshape(b0, n0, d), x[lay.t0:].reshape(b1, n1, d))
```

```python
import functools
import math

import jax
import jax.numpy as jnp
import numpy as np
from jax import lax
from jax.experimental import pallas as pl
from jax.experimental.pallas import tpu as pltpu

F32 = jnp.float32
BF16 = jnp.bfloat16

D_MODEL = 2048
HEAD_DIM = 128
A_HEADS = 8
A_KV_HEADS = 2
A_GROUP = A_HEADS // A_KV_HEADS
A_Q = A_HEADS * HEAD_DIM
A_KV = A_KV_HEADS * HEAD_DIM
B_HEADS = 4
B_HALF = HEAD_DIM // 2
B_WIDTH = B_HEADS * HEAD_DIM
C_WIDTH = 512
C_GROUPS = 4
C_GROUP_W = 128
IN_WIDTH = A_Q + 2 * A_KV + 3 * B_WIDTH + C_WIDTH
GRID_W = 64
ROPE_THETA = 10000.0
NUM_BUCKETS = 32
MAX_DISTANCE = 128
N_GROUPS = 4
EXPERTS_PER_GROUP = 4
N_EXPERTS = 16
D_FF_EXPERT = 512
EPS = 1e-6
LOG2E = 1.4426950408889634

_PAIRS = ((0, 1), (0, 2), (0, 3), (1, 2), (1, 3), (2, 3))
N_BUCKETS = N_GROUPS * len(_PAIRS)

V7X_VMEM_BYTES = 64 * 1024 * 1024
VMEM_LIMIT = V7X_VMEM_BYTES - 8 * 1024 * 1024
LANES = 128
NEG_BIG = -1e30


def _params(sem, **kw):
    return pltpu.CompilerParams(dimension_semantics=sem, vmem_limit_bytes=VMEM_LIMIT, **kw)


def _pick(total, pref):
    t = min(pref, total)
    while total % t:
        t //= 2
    return t


def _ada_kernel(c_ref, w_ref, b_ref, o_ref):
    h = jax.nn.silu(c_ref[...]).astype(BF16)
    o_ref[0] = jnp.dot(h, w_ref[0].astype(BF16), preferred_element_type=F32) + b_ref[0]


def _ada_call(c_all, w_ada, b_ada):
    depth, d, e = w_ada.shape
    bp = c_all.shape[0]
    tn = _pick(e, 1024)
    return pl.pallas_call(
        _ada_kernel,
        grid=(depth, e // tn),
        in_specs=[
            pl.BlockSpec((bp, d), lambda l, j: (0, 0)),
            pl.BlockSpec((1, d, tn), lambda l, j: (l, 0, j)),
            pl.BlockSpec((1, 1, tn), lambda l, j: (l, 0, j)),
        ],
        out_specs=pl.BlockSpec((1, bp, tn), lambda l, j: (l, 0, j)),
        out_shape=jax.ShapeDtypeStruct((depth, bp, e), F32),
        compiler_params=_params(("arbitrary", "arbitrary")),
        name="ada",
    )(c_all, w_ada, b_ada.reshape(depth, 1, e))


class _Layout:
    def __init__(self, b0, n0, b1, n1):
        self.b = (b0, b1)
        self.n = (n0, n1)
        self.t0 = b0 * n0
        self.t = b0 * n0 + b1 * n1

    def batch_of_tile(self, i, tm):
        tok = i * tm
        return jnp.where(tok < self.t0, tok // self.n[0], self.b[0] + (tok - self.t0) // self.n[1])

    def pos_block_of_tile(self, i, tm):
        tok = i * tm
        pos = jnp.where(tok < self.t0, tok % self.n[0], (tok - self.t0) % self.n[1])
        return pos // tm


def _head_norm(z, g):
    return z * lax.rsqrt(jnp.mean(z * z, axis=-1, keepdims=True) + EPS) * g


def _half_norm(z, g, lo):
    zz = z * z
    s_lo = jnp.sum(jnp.where(lo, zz, 0.0), axis=-1, keepdims=True)
    s_hi = jnp.sum(jnp.where(lo, 0.0, zz), axis=-1, keepdims=True)
    inv = jnp.where(lo, lax.rsqrt(s_lo / B_HALF + EPS), lax.rsqrt(s_hi / B_HALF + EPS))
    return z * inv * g


def _rope(z, c, s_signed, even):
    partner = jnp.where(even, pltpu.roll(z, LANES - 1, 1), pltpu.roll(z, 1, 1))
    return z * c + partner * s_signed


def _proj_kernel(x_ref, mod_ref, gn_ref, w_ref, gqa_ref, gka_ref, gqb_ref, gkb_ref, cos_ref, sin_ref,
                 dft_ref, qa_ref, ka_ref, va_ref, qb_ref, kb_ref, vb_ref, p_ref, q_ref):
    x = x_ref[...]
    y = x * lax.rsqrt(jnp.mean(x * x, axis=-1, keepdims=True) + EPS) * gn_ref[...]
    hn = (y * (1.0 + mod_ref[0, 1:2, :]) + mod_ref[0, 0:1, :]).astype(BF16)

    def seg(a, b):
        return jnp.dot(hn, w_ref[:, a:b], preferred_element_type=F32)

    tm = x.shape[0]
    lane = lax.broadcasted_iota(jnp.int32, (tm, LANES), 1)
    even = (lane % 2) == 0
    lo = lane < B_HALF
    cos = cos_ref[...]
    sin = sin_ref[...]

    scale_a = HEAD_DIM ** -0.5 * LOG2E
    z = seg(0, A_Q)
    for h in range(A_HEADS):
        sl = slice(h * HEAD_DIM, (h + 1) * HEAD_DIM)
        qa_ref[:, sl] = (_rope(_head_norm(z[:, sl], gqa_ref[...]), cos, sin, even) * scale_a).astype(BF16)
    off = A_Q
    z = seg(off, off + A_KV)
    for h in range(A_KV_HEADS):
        sl = slice(h * HEAD_DIM, (h + 1) * HEAD_DIM)
        ka_ref[:, sl] = _rope(_head_norm(z[:, sl], gka_ref[...]), cos, sin, even).astype(BF16)
    off += A_KV
    va_ref[...] = seg(off, off + A_KV).astype(BF16)
    off += A_KV

    scale_b = B_HALF ** -0.5 * LOG2E
    z = seg(off, off + B_WIDTH)
    for h in range(B_HEADS):
        sl = slice(h * HEAD_DIM, (h + 1) * HEAD_DIM)
        qb_ref[:, sl] = (_half_norm(z[:, sl], gqb_ref[...], lo) * scale_b).astype(BF16)
    off += B_WIDTH
    z = seg(off, off + B_WIDTH)
    for h in range(B_HEADS):
        sl = slice(h * HEAD_DIM, (h + 1) * HEAD_DIM)
        kb_ref[:, sl] = _half_norm(z[:, sl], gkb_ref[...], lo).astype(BF16)
    off += B_WIDTH
    vb_ref[...] = seg(off, off + B_WIDTH).astype(BF16)
    off += B_WIDTH

    z = seg(off, off + C_WIDTH).astype(BF16)
    for g in range(C_GROUPS):
        sl = slice(g * C_GROUP_W, (g + 1) * C_GROUP_W)
        pq = jnp.dot(z[:, sl], dft_ref[...], preferred_element_type=F32)
        p_ref[:, sl] = pq[:, :C_GROUP_W].astype(BF16)
        q_ref[:, sl] = pq[:, C_GROUP_W:].astype(BF16)


def _proj_call(lay, x, mod3, gn, w_in, gqa, gka, gqb, gkb, cos_e, sin_e, dft_c, tm):
    t, d = x.shape
    row = lambda i: (i, 0)
    const = lambda i: (0, 0)
    widths = (A_Q, A_KV, A_KV, B_WIDTH, B_WIDTH, B_WIDTH, C_WIDTH, C_WIDTH)
    return pl.pallas_call(
        _proj_kernel,
        grid=(t // tm,),
        in_specs=[
            pl.BlockSpec((tm, d), row),
            pl.BlockSpec((1, 6, d), lambda i: (lay.batch_of_tile(i, tm), 0, 0)),
            pl.BlockSpec((1, d), const),
            pl.BlockSpec((d, IN_WIDTH), const),
            pl.BlockSpec((1, LANES), const),
            pl.BlockSpec((1, LANES), const),
            pl.BlockSpec((1, LANES), const),
            pl.BlockSpec((1, LANES), const),
            pl.BlockSpec((tm, LANES), lambda i: (lay.pos_block_of_tile(i, tm), 0)),
            pl.BlockSpec((tm, LANES), lambda i: (lay.pos_block_of_tile(i, tm), 0)),
            pl.BlockSpec((C_GROUP_W, 2 * C_GROUP_W), const),
        ],
        out_specs=[pl.BlockSpec((tm, w), row) for w in widths],
        out_shape=[jax.ShapeDtypeStruct((t, w), BF16) for w in widths],
        compiler_params=_params(("arbitrary",)),
        name="proj",
    )(x, mod3, gn, w_in, gqa, gka, gqb, gkb, cos_e, sin_e, dft_c)


def _softmax_step(s, vc, m_ref, l_ref, acc_ref):
    m_prev = m_ref[...]
    m_new = jnp.maximum(m_prev, jnp.max(s, axis=1, keepdims=True))
    alpha = jnp.exp2(m_prev - m_new)
    p = jnp.exp2(s - jnp.tile(m_new, (1, s.shape[1] // LANES)))
    l_ref[...] = alpha * l_ref[...] + jnp.sum(p, axis=1, keepdims=True)
    acc_ref[...] = alpha * acc_ref[...] + jnp.dot(p.astype(BF16), vc, preferred_element_type=F32)
    m_ref[...] = m_new


_NT = (((1,), (1,)), ((), ()))


def _attn_a_kernel(q_ref, k_ref, v_ref, o_ref, qs_ref, m_ref, l_ref, acc_ref, *, tq, tk, n):
    for g in range(A_GROUP):
        qs_ref[g * tq:(g + 1) * tq, :] = q_ref[:, g * HEAD_DIM:(g + 1) * HEAD_DIM]
    m_ref[...] = jnp.full(m_ref.shape, NEG_BIG, F32)
    l_ref[...] = jnp.zeros(l_ref.shape, F32)
    acc_ref[...] = jnp.zeros(acc_ref.shape, F32)

    def body(j, carry):
        rows = pl.ds(pl.multiple_of(j * tk, tk), tk)
        s = lax.dot_general(qs_ref[...], k_ref[rows, :], _NT, preferred_element_type=F32)
        _softmax_step(s, v_ref[rows, :], m_ref, l_ref, acc_ref)
        return carry

    lax.fori_loop(0, n // tk, body, 0, unroll=2)
    o = acc_ref[...] / l_ref[...]
    for g in range(A_GROUP):
        o_ref[:, g * HEAD_DIM:(g + 1) * HEAD_DIM] = o[g * tq:(g + 1) * tq].astype(o_ref.dtype)


def _attn_a_call(qa, ka, va, tok0, b, n, tq, tk):
    qblk0 = tok0 // tq
    kblk0 = tok0 // n
    nq = n // tq
    rows = A_GROUP * tq
    return pl.pallas_call(
        functools.partial(_attn_a_kernel, tq=tq, tk=tk, n=n),
        grid=(b, A_KV_HEADS, nq),
        in_specs=[
            pl.BlockSpec((tq, A_GROUP * HEAD_DIM), lambda bi, kv, qi: (qblk0 + bi * nq + qi, kv)),
            pl.BlockSpec((n, HEAD_DIM), lambda bi, kv, qi: (kblk0 + bi, kv)),
            pl.BlockSpec((n, HEAD_DIM), lambda bi, kv, qi: (kblk0 + bi, kv)),
        ],
        out_specs=pl.BlockSpec((tq, A_GROUP * HEAD_DIM), lambda bi, kv, qi: (bi * nq + qi, kv)),
        out_shape=jax.ShapeDtypeStruct((b * n, A_Q), BF16),
        scratch_shapes=[
            pltpu.VMEM((rows, HEAD_DIM), BF16),
            pltpu.VMEM((rows, LANES), F32),
            pltpu.VMEM((rows, LANES), F32),
            pltpu.VMEM((rows, HEAD_DIM), F32),
        ],
        compiler_params=_params(("arbitrary", "arbitrary", "arbitrary")),
        name="attn_a",
    )(qa, ka, va)


def _attn_b_kernel(q_ref, k_ref, v_ref, band_ref, lam_ref, gs_ref, o_ref,
                   qs_ref, m_ref, l_ref, acc_ref, *, t, nchunks, lam_init):
    i = pl.program_id(2)
    q = q_ref[...]
    lo = lax.broadcasted_iota(jnp.int32, q.shape, 1) < B_HALF
    zero = jnp.zeros_like(q)
    qs_ref[0:t, :] = jnp.where(lo, q, zero)
    qs_ref[t:2 * t, :] = jnp.where(lo, zero, q)
    m_ref[...] = jnp.full(m_ref.shape, NEG_BIG, F32)
    l_ref[...] = jnp.zeros(l_ref.shape, F32)
    acc_ref[...] = jnp.zeros(acc_ref.shape, F32)

    def body(j, carry):
        rows = pl.ds(pl.multiple_of(j * t, t), t)
        s = lax.dot_general(qs_ref[...], k_ref[rows, :], _NT, preferred_element_type=F32)
        bias = band_ref[0, jnp.clip(j - i, -_BAND_REACH, _BAND_REACH) + _BAND_REACH]
        s = (s.reshape(2, t, t) + bias[None]).reshape(2 * t, t)
        _softmax_step(s, v_ref[rows, :], m_ref, l_ref, acc_ref)
        return carry

    lax.fori_loop(0, nchunks, body, 0, unroll=2)
    o = acc_ref[...] / l_ref[...]
    lq = lam_ref[...]
    lam = (jnp.exp(jnp.sum(lq[0:1] * lq[1:2], axis=-1, keepdims=True))
           - jnp.exp(jnp.sum(lq[2:3] * lq[3:4], axis=-1, keepdims=True)) + lam_init)
    ob = o[0:t] - lam * o[t:2 * t]
    o_ref[...] = (_head_norm(ob, gs_ref[...]) * (1.0 - lam_init)).astype(o_ref.dtype)


def _attn_b_call(qb, kb, vb, band, lam_qk, g_subln, tok0, b, n, t, lam_init):
    qblk0 = tok0 // t
    kblk0 = tok0 // n
    nq = n // t
    return pl.pallas_call(
        functools.partial(_attn_b_kernel, t=t, nchunks=nq, lam_init=lam_init),
        grid=(b, B_HEADS, nq),
        in_specs=[
            pl.BlockSpec((t, HEAD_DIM), lambda bi, h, qi: (qblk0 + bi * nq + qi, h)),
            pl.BlockSpec((n, HEAD_DIM), lambda bi, h, qi: (kblk0 + bi, h)),
            pl.BlockSpec((n, HEAD_DIM), lambda bi, h, qi: (kblk0 + bi, h)),
            pl.BlockSpec((1, 2 * _BAND_REACH + 1, t, t), lambda bi, h, qi: (h, 0, 0, 0)),
            pl.BlockSpec((4, B_HALF), lambda bi, h, qi: (0, 0)),
            pl.BlockSpec((1, LANES), lambda bi, h, qi: (0, 0)),
        ],
        out_specs=pl.BlockSpec((t, HEAD_DIM), lambda bi, h, qi: (bi * nq + qi, h)),
        out_shape=jax.ShapeDtypeStruct((b * n, B_WIDTH), BF16),
        scratch_shapes=[
            pltpu.VMEM((2 * t, HEAD_DIM), BF16),
            pltpu.VMEM((2 * t, LANES), F32),
            pltpu.VMEM((2 * t, LANES), F32),
            pltpu.VMEM((2 * t, HEAD_DIM), F32),
        ],
        compiler_params=_params(("arbitrary", "arbitrary", "arbitrary")),
        name="attn_b",
    )(qb, kb, vb, band, lam_qk, g_subln)


def _t5_bucket(rel):
    nb = NUM_BUCKETS // 2
    ret = (rel > 0).astype(jnp.int32) * nb
    n = jnp.abs(rel)
    max_exact = nb // 2
    nf = jnp.maximum(n, 1).astype(F32)
    large = max_exact + (jnp.log(nf / max_exact) / math.log(MAX_DISTANCE / max_exact)
                         * (nb - max_exact)).astype(jnp.int32)
    large = jnp.minimum(large, nb - 1)
    return ret + jnp.where(n < max_exact, n, large)


_T5_SATURATION = math.ceil((NUM_BUCKETS // 4) * (MAX_DISTANCE / (NUM_BUCKETS // 4)) ** (
    (NUM_BUCKETS // 2 - 1 - NUM_BUCKETS // 4) / (NUM_BUCKETS // 2 - NUM_BUCKETS // 4)))
_BAND_REACH = 2


def _bias_tables(rel_bias, t):
    assert t + 1 >= _T5_SATURATION
    heads = rel_bias.shape[1]
    nd = 2 * _BAND_REACH + 1
    rel = ((jnp.arange(nd, dtype=jnp.int32)[:, None] - _BAND_REACH) * t
           + jnp.arange(-(t - 1), t, dtype=jnp.int32)[None, :])
    vec = jnp.take(rel_bias, _t5_bucket(rel), axis=0).astype(F32)
    vec = vec.transpose(2, 0, 1) * LOG2E
    skew = jnp.tile(vec, (1, 1, 2 * t))[:, :, :2 * t * t].reshape(heads, nd, t, 2 * t)
    return skew[:, :, ::-1, :t]


def _dft_gen_kernel(ck_ref, sk_ref, ca_ref, sa_ref, c_ref, s_ref):
    ck = ck_ref[0]
    sk = sk_ref[0]
    ca = ca_ref[...]
    sa = sa_ref[...]
    c_ref[...] = (ck * ca - sk * sa).astype(c_ref.dtype)
    s_ref[...] = (sk * ca + ck * sa).astype(s_ref.dtype)


def _dft_tables(n, tk):
    t = jnp.arange(n, dtype=jnp.int32)[None, :]
    k0 = (jnp.arange(n // tk, dtype=jnp.int32) * tk)[:, None]
    a = jnp.arange(tk, dtype=jnp.int32)[:, None]
    ang_k = ((k0 * t) % n).astype(F32) * (2.0 * math.pi / n)
    ang_a = ((a * t) % n).astype(F32) * (2.0 * math.pi / n)
    scale = n ** -0.5
    ck = (jnp.cos(ang_k) * scale).reshape(n // tk, 1, n)
    sk = (jnp.sin(ang_k) * scale).reshape(n // tk, 1, n)
    return pl.pallas_call(
        _dft_gen_kernel,
        grid=(n // tk,),
        in_specs=[
            pl.BlockSpec((1, 1, n), lambda i: (i, 0, 0)),
            pl.BlockSpec((1, 1, n), lambda i: (i, 0, 0)),
            pl.BlockSpec((tk, n), lambda i: (0, 0)),
            pl.BlockSpec((tk, n), lambda i: (0, 0)),
        ],
        out_specs=[pl.BlockSpec((tk, n), lambda i: (i, 0)), pl.BlockSpec((tk, n), lambda i: (i, 0))],
        out_shape=[jax.ShapeDtypeStruct((n, n), BF16), jax.ShapeDtypeStruct((n, n), BF16)],
        compiler_params=_params(("arbitrary",)),
        name="dft_gen",
    )(ck, sk, jnp.cos(ang_a), jnp.sin(ang_a))


def _fourier_kernel(c_ref, s_ref, p_ref, q_ref, w_ref, o_ref):
    f = (jnp.dot(c_ref[...], p_ref[...], preferred_element_type=F32)
         - jnp.dot(s_ref[...], q_ref[...], preferred_element_type=F32)).astype(BF16)
    for g in range(C_GROUPS):
        sl = slice(g * C_GROUP_W, (g + 1) * C_GROUP_W)
        o_ref[:, sl] = jnp.dot(f[:, sl], w_ref[g], preferred_element_type=F32).astype(o_ref.dtype)


def _fourier_call(cmat, smat, p, q, wf, tok0, b, n, tf):
    kblk0 = tok0 // n
    nk = n // tf
    return pl.pallas_call(
        _fourier_kernel,
        grid=(b, nk),
        in_specs=[
            pl.BlockSpec((tf, n), lambda bi, kt: (kt, 0)),
            pl.BlockSpec((tf, n), lambda bi, kt: (kt, 0)),
            pl.BlockSpec((n, C_WIDTH), lambda bi, kt: (kblk0 + bi, 0)),
            pl.BlockSpec((n, C_WIDTH), lambda bi, kt: (kblk0 + bi, 0)),
            pl.BlockSpec((C_GROUPS, C_GROUP_W, C_GROUP_W), lambda bi, kt: (0, 0, 0)),
        ],
        out_specs=pl.BlockSpec((tf, C_WIDTH), lambda bi, kt: (bi * nk + kt, 0)),
        out_shape=jax.ShapeDtypeStruct((b * n, C_WIDTH), BF16),
        compiler_params=_params(("arbitrary", "arbitrary")),
        name="fourier",
    )(cmat, smat, p, q, wf)


def _out_kernel(oa_ref, ob_ref, oc_ref, x_ref, mod_ref, w_ref, gn_ref, wr_ref,
                x1_ref, hn_ref, ids_ref, wts_ref):
    mix = (jnp.dot(oa_ref[...], w_ref[0:A_Q, :], preferred_element_type=F32)
           + jnp.dot(ob_ref[...], w_ref[A_Q:A_Q + B_WIDTH, :], preferred_element_type=F32)
           + jnp.dot(oc_ref[...], w_ref[A_Q + B_WIDTH:, :], preferred_element_type=F32))
    x1 = x_ref[...] + mod_ref[0, 2:3, :] * mix
    x1_ref[...] = x1
    y = x1 * lax.rsqrt(jnp.mean(x1 * x1, axis=-1, keepdims=True) + EPS) * gn_ref[...]
    hn = y * (1.0 + mod_ref[0, 4:5, :]) + mod_ref[0, 3:4, :]
    hn_ref[...] = hn

    logits = jnp.dot(hn.astype(BF16), wr_ref[...], preferred_element_type=F32)
    lane = lax.broadcasted_iota(jnp.int32, logits.shape, 1)
    big = jnp.int32(LANES)
    is_g = lane < N_GROUPS
    gmax = jnp.max(jnp.where(is_g, logits, -jnp.inf), axis=-1, keepdims=True)
    g_sel = jnp.min(jnp.where(jnp.logical_and(is_g, logits == gmax), lane, big), axis=-1, keepdims=True)
    g_w = 1.0 / jnp.sum(jnp.where(is_g, jnp.exp(logits - gmax), 0.0), axis=-1, keepdims=True)
    lo_lane = N_GROUPS + g_sel * EXPERTS_PER_GROUP
    in_g = jnp.logical_and(lane >= lo_lane, lane < lo_lane + EXPERTS_PER_GROUP)
    v0 = jnp.max(jnp.where(in_g, logits, -jnp.inf), axis=-1, keepdims=True)
    i0 = jnp.min(jnp.where(jnp.logical_and(in_g, logits == v0), lane, big), axis=-1, keepdims=True)
    rest = jnp.logical_and(in_g, lane != i0)
    v1 = jnp.max(jnp.where(rest, logits, -jnp.inf), axis=-1, keepdims=True)
    i1 = jnp.min(jnp.where(jnp.logical_and(rest, logits == v1), lane, big), axis=-1, keepdims=True)
    e1 = jnp.exp(v1 - v0)
    w0 = g_w / (1.0 + e1)
    w1 = g_w * e1 / (1.0 + e1)
    swap = i1 < i0
    ea = jnp.where(swap, i1, i0) - N_GROUPS
    eb = jnp.where(swap, i0, i1) - N_GROUPS
    wa = jnp.where(swap, w1, w0)
    wb = jnp.where(swap, w0, w1)
    ml = lax.broadcasted_iota(jnp.int32, ids_ref.shape, 1)
    ids_ref[...] = jnp.where(ml == 0, ea, jnp.where(ml == 1, eb, 0))
    wts_ref[...] = jnp.where(ml == 0, wa, jnp.where(ml == 1, wb, 0.0))


def _out_call(lay, oa, ob, oc, x, mod3, w_out, gn, w_router, tm):
    t, d = x.shape
    row = lambda i: (i, 0)
    const = lambda i: (0, 0)
    meta = 8
    return pl.pallas_call(
        _out_kernel,
        grid=(t // tm,),
        in_specs=[
            pl.BlockSpec((tm, A_Q), row),
            pl.BlockSpec((tm, B_WIDTH), row),
            pl.BlockSpec((tm, C_WIDTH), row),
            pl.BlockSpec((tm, d), row),
            pl.BlockSpec((1, 6, d), lambda i: (lay.batch_of_tile(i, tm), 0, 0)),
            pl.BlockSpec((d, d), const),
            pl.BlockSpec((1, d), const),
            pl.BlockSpec((d, LANES), const),
        ],
        out_specs=[pl.BlockSpec((tm, d), row), pl.BlockSpec((tm, d), row),
                   pl.BlockSpec((tm, meta), row), pl.BlockSpec((tm, meta), row)],
        out_shape=[jax.ShapeDtypeStruct((t, d), F32), jax.ShapeDtypeStruct((t, d), F32),
                   jax.ShapeDtypeStruct((t, meta), jnp.int32), jax.ShapeDtypeStruct((t, meta), F32)],
        compiler_params=_params(("arbitrary",)),
        name="out_proj",
    )(oa, ob, oc, x, mod3, w_out, gn, w_router)


def _route_plan(ids, tm):
    t = ids.shape[0]
    ea = ids[:, 0]
    eb = ids[:, 1]
    grp = ea // EXPERTS_PER_GROUP
    la = ea % EXPERTS_PER_GROUP
    lb = eb % EXPERTS_PER_GROUP
    pair = la * (2 * EXPERTS_PER_GROUP - 1 - la) // 2 + (lb - la - 1)
    bucket = grp * len(_PAIRS) + pair
    onehot = (bucket[:, None] == jnp.arange(N_BUCKETS, dtype=jnp.int32)[None, :]).astype(jnp.int32)
    csum = jnp.cumsum(onehot, axis=0)
    counts = csum[-1]
    rank = jnp.take_along_axis(csum, bucket[:, None], axis=1)[:, 0] - 1
    tiles = (counts + tm - 1) // tm
    tile_end = jnp.cumsum(tiles)
    tile_start = tile_end - tiles
    pos = jnp.take(tile_start, bucket) * tm + rank
    assert t % tm == 0
    n_tiles = t // tm + N_BUCKETS
    src = jnp.zeros((n_tiles * tm,), jnp.int32).at[pos].set(jnp.arange(t, dtype=jnp.int32))
    tile_ids = jnp.arange(n_tiles, dtype=jnp.int32)
    used = tile_end[-1]
    tile_bucket = jnp.searchsorted(tile_end, jnp.minimum(tile_ids, used - 1), side="right").astype(jnp.int32)
    pairs = np.array(_PAIRS, np.int32)
    tgrp = tile_bucket // len(_PAIRS)
    tpair = tile_bucket % len(_PAIRS)
    tile_ea = tgrp * EXPERTS_PER_GROUP + jnp.take(jnp.asarray(pairs[:, 0]), tpair)
    tile_eb = tgrp * EXPERTS_PER_GROUP + jnp.take(jnp.asarray(pairs[:, 1]), tpair)
    rows_left = jnp.take(counts, tile_bucket) - (tile_ids - jnp.take(tile_start, tile_bucket)) * tm
    tile_rows = jnp.where(tile_ids < used, jnp.clip(rows_left, 0, tm), 0).astype(jnp.int32)
    return pos.astype(jnp.int32), src, tile_ea.astype(jnp.int32), tile_eb.astype(jnp.int32), tile_rows, used


def _row_copy(src_hbm, row, buf, slot, r, sem):
    return pltpu.make_async_copy(src_hbm.at[pl.ds(row, 1), :], buf.at[slot, pl.ds(r, 1), :], sem.at[slot])


def _gather_start(idx_ref, base, rows, src_hbm, buf, slot, sem):
    for r in range(rows):
        _row_copy(src_hbm, idx_ref[base + r], buf, slot, r, sem).start()


def _gather_wait(rows, src_hbm, buf, slot, sem):
    pltpu.make_async_copy(src_hbm.at[pl.ds(0, rows), :], buf.at[slot], sem.at[slot]).wait()


def _moe_kernel(src_ref, ea_ref, eb_ref, rows_ref, h_hbm, w1a_ref, w3a_ref, w2a_ref, w1b_ref, w3b_ref, w2b_ref,
                wts_ref, y_ref, buf, sem, *, tm):
    i = pl.program_id(0)
    slot = i % 2
    used = rows_ref[i] > 0

    @pl.when(i == 0)
    def _():
        _gather_start(src_ref, 0, tm, h_hbm, buf, 0, sem)

    @pl.when(jnp.logical_or(i == 0, rows_ref[jnp.maximum(i - 1, 0)] > 0))
    def _():
        _gather_wait(tm, h_hbm, buf, slot, sem)

    @pl.when(jnp.logical_not(used))
    def _():
        y_ref[...] = jnp.zeros(y_ref.shape, y_ref.dtype)

    @pl.when(used)
    def _():
        _gather_start(src_ref, (i + 1) * tm, tm, h_hbm, buf, 1 - slot, sem)
        h = buf[slot].astype(BF16)

        def expert(w1_ref, w3_ref, w2_ref):
            a = (jax.nn.silu(jnp.dot(h, w1_ref[0], preferred_element_type=F32))
                 * jnp.dot(h, w3_ref[0], preferred_element_type=F32))
            return jnp.dot(a.astype(BF16), w2_ref[0], preferred_element_type=F32)

        w = wts_ref[...]
        y_ref[...] = (expert(w1a_ref, w3a_ref, w2a_ref) * w[:, 0:1]
                      + expert(w1b_ref, w3b_ref, w2b_ref) * w[:, 1:2])


def _moe_call(src, tile_ea, tile_eb, tile_rows, hn, w1, w3, w2, wts_sorted, tm):
    n_tiles = tile_ea.shape[0]
    d = hn.shape[1]
    f = w1.shape[2]
    wa = lambda i, src, ea, eb, rows: (ea[i], 0, 0)
    wb = lambda i, src, ea, eb, rows: (eb[i], 0, 0)
    grid_spec = pltpu.PrefetchScalarGridSpec(
        num_scalar_prefetch=4,
        grid=(n_tiles,),
        in_specs=[
            pl.BlockSpec(memory_space=pl.ANY),
            pl.BlockSpec((1, d, f), wa), pl.BlockSpec((1, d, f), wa), pl.BlockSpec((1, f, d), wa),
            pl.BlockSpec((1, d, f), wb), pl.BlockSpec((1, d, f), wb), pl.BlockSpec((1, f, d), wb),
            pl.BlockSpec((tm, 8), lambda i, *_: (i, 0)),
        ],
        out_specs=pl.BlockSpec((tm, d), lambda i, *_: (i, 0)),
        scratch_shapes=[pltpu.VMEM((2, tm, d), F32), pltpu.SemaphoreType.DMA((2,))],
    )
    return pl.pallas_call(
        functools.partial(_moe_kernel, tm=tm),
        grid_spec=grid_spec,
        out_shape=jax.ShapeDtypeStruct((n_tiles * tm, d), F32),
        compiler_params=_params(("arbitrary",)),
        name="moe",
    )(src, tile_ea, tile_eb, tile_rows, hn, w1, w3, w2, w1, w3, w2, wts_sorted)


def _combine_kernel(pos_ref, y_hbm, x1_ref, mod_ref, o_ref, buf, sem, *, tm):
    i = pl.program_id(0)
    nt = pl.num_programs(0)
    slot = i % 2

    @pl.when(i == 0)
    def _():
        _gather_start(pos_ref, 0, tm, y_hbm, buf, 0, sem)

    _gather_wait(tm, y_hbm, buf, slot, sem)

    @pl.when(i + 1 < nt)
    def _():
        _gather_start(pos_ref, (i + 1) * tm, tm, y_hbm, buf, 1 - slot, sem)

    o_ref[...] = x1_ref[...] + mod_ref[0, 5:6, :] * buf[slot]


def _combine_call(lay, pos, y_sorted, x1, mod3, tm):
    t, d = x1.shape
    grid_spec = pltpu.PrefetchScalarGridSpec(
        num_scalar_prefetch=1,
        grid=(t // tm,),
        in_specs=[
            pl.BlockSpec(memory_space=pl.ANY),
            pl.BlockSpec((tm, d), lambda i, pos: (i, 0)),
            pl.BlockSpec((1, 6, d), lambda i, pos: (lay.batch_of_tile(i, tm), 0, 0)),
        ],
        out_specs=pl.BlockSpec((tm, d), lambda i, pos: (i, 0)),
        scratch_shapes=[pltpu.VMEM((2, tm, d), F32), pltpu.SemaphoreType.DMA((2,))],
    )
    return pl.pallas_call(
        functools.partial(_combine_kernel, tm=tm),
        grid_spec=grid_spec,
        out_shape=jax.ShapeDtypeStruct((t, d), F32),
        compiler_params=_params(("arbitrary",)),
        name="combine",
    )(pos, y_sorted, x1, mod3)


def _rope_tables(n):
    n_rows = n // GRID_W
    rows = jnp.repeat(jnp.arange(n_rows), GRID_W).astype(F32)
    cols = jnp.tile(jnp.arange(GRID_W), n_rows).astype(F32)
    half = HEAD_DIM // 2
    inv = ROPE_THETA ** (-jnp.arange(0, half, 2, dtype=F32) / half)
    ang = jnp.concatenate([rows[:, None] * inv, cols[:, None] * inv], axis=-1)
    sign = jnp.tile(jnp.array([-1.0, 1.0], F32), half)
    return jnp.repeat(jnp.cos(ang), 2, axis=-1), jnp.repeat(jnp.sin(ang), 2, axis=-1) * sign


def _lambda_init(layer_idx):
    return 0.8 - 0.6 * math.exp(-0.3 * layer_idx)


def _tiled_gain(g):
    return jnp.tile(g, LANES // g.shape[-1]).reshape(1, LANES).astype(F32)


def kernel(x_prompt, x_sample, c_prompt, c_sample, rel_bias, w_ada, b_ada, g_norm_mix, w_in, g_qa, g_ka,
           g_qb, g_kb, lam_qk, g_subln, w_fourier, w_out, g_norm_ffn, w_group, w_expert, w1, w3, w2):
    b0, n0, d = x_prompt.shape
    b1, n1, _ = x_sample.shape
    depth = w_in.shape[0]
    lay = _Layout(b0, n0, b1, n1)
    trunks = ((0, b0, n0), (lay.t0, b1, n1))
    n_max = max(n0, n1)

    tm = _pick(math.gcd(n0, n1), 512)
    tq_a = _pick(math.gcd(n0, n1), 256)
    tk_a = _pick(math.gcd(n0, n1), 1024)
    t_b = _pick(math.gcd(n0, n1), 512)
    tf = _pick(math.gcd(n0, n1), 128)
    tm_moe = 256

    x = jnp.concatenate([x_prompt.reshape(b0 * n0, d), x_sample.reshape(b1 * n1, d)], axis=0)
    nb = b0 + b1
    bp = -(-nb // 8) * 8
    c_all = jnp.zeros((bp, d), F32).at[:nb].set(jnp.concatenate([c_prompt, c_sample], axis=0))
    mod = _ada_call(c_all, w_ada, b_ada).reshape(depth, bp, 6, d)

    cos_e, sin_e = _rope_tables(n_max)
    band = _bias_tables(rel_bias, t_b)
    dft = {n: _dft_tables(n, _pick(n, 128)) for n in sorted({n0, n1})}
    cidx = jnp.arange(C_GROUP_W, dtype=jnp.int32)
    ang_c = ((cidx[:, None] * cidx[None, :]) % C_GROUP_W).astype(F32) * (2.0 * math.pi / C_GROUP_W)
    dft_c = (jnp.concatenate([jnp.cos(ang_c), jnp.sin(ang_c)], axis=1) * C_GROUP_W ** -0.5).astype(BF16)

    for l in range(depth):
        mod3 = mod[l]
        lam_init = _lambda_init(l)
        qa, ka, va, qb, kb, vb, p, q = _proj_call(
            lay, x, mod3, g_norm_mix[l].reshape(1, d), w_in[l].astype(BF16),
            _tiled_gain(g_qa[l]), _tiled_gain(g_ka[l]), _tiled_gain(g_qb[l]), _tiled_gain(g_kb[l]),
            cos_e, sin_e, dft_c, tm)

        wf = w_fourier[l].astype(BF16)
        oa, ob, oc = [], [], []
        for tok0, b, n in trunks:
            oa.append(_attn_a_call(qa, ka, va, tok0, b, n, tq_a, tk_a))
            ob.append(_attn_b_call(qb, kb, vb, band, lam_qk[l], _tiled_gain(g_subln[l]),
                                   tok0, b, n, t_b, lam_init))
            oc.append(_fourier_call(dft[n][0], dft[n][1], p, q, wf, tok0, b, n, tf))
        oa = jnp.concatenate(oa, axis=0)
        ob = jnp.concatenate(ob, axis=0)
        oc = jnp.concatenate(oc, axis=0)

        w_router = jnp.zeros((d, LANES), F32).at[:, :N_GROUPS].set(w_group[l])
        w_router = w_router.at[:, N_GROUPS:N_GROUPS + N_EXPERTS].set(w_expert[l]).astype(BF16)
        x1, hn2, ids, wts = _out_call(lay, oa, ob, oc, x, mod3, w_out[l].astype(BF16),
                                      g_norm_ffn[l].reshape(1, d), w_router, tm)

        pos, src, tile_ea, tile_eb, tile_rows, _ = _route_plan(ids, tm_moe)
        wts_sorted = jnp.zeros((src.shape[0], 8), F32).at[pos].set(wts)
        y_sorted = _moe_call(src, tile_ea, tile_eb, tile_rows, hn2, w1[l].astype(BF16), w3[l].astype(BF16),
                             w2[l].astype(BF16), wts_sorted, tm_moe)
        x = _combine_call(lay, pos, y_sorted, x1, mod3, tm)

    return (x[:lay.t0].reshape(b0, n0, d), x[lay.t0:].reshape(b1, n1, d))
```

```python
import functools
import math

import jax
import jax.numpy as jnp
import numpy as np
from jax import lax
from jax.experimental import pallas as pl
from jax.experimental.pallas import tpu as pltpu

F32 = jnp.float32
BF16 = jnp.bfloat16

D_MODEL = 2048
HEAD_DIM = 128
A_HEADS = 8
A_KV_HEADS = 2
A_GROUP = A_HEADS // A_KV_HEADS
A_Q = A_HEADS * HEAD_DIM
A_KV = A_KV_HEADS * HEAD_DIM
B_HEADS = 4
B_HALF = HEAD_DIM // 2
B_WIDTH = B_HEADS * HEAD_DIM
C_WIDTH = 512
C_GROUPS = 4
C_GROUP_W = 128
IN_WIDTH = A_Q + 2 * A_KV + 3 * B_WIDTH + C_WIDTH
GRID_W = 64
ROPE_THETA = 10000.0
NUM_BUCKETS = 32
MAX_DISTANCE = 128
N_GROUPS = 4
EXPERTS_PER_GROUP = 4
N_EXPERTS = 16
D_FF_EXPERT = 512
EPS = 1e-6
LOG2E = 1.4426950408889634

_PAIRS = ((0, 1), (0, 2), (0, 3), (1, 2), (1, 3), (2, 3))
N_BUCKETS = N_GROUPS * len(_PAIRS)

V7X_VMEM_BYTES = 64 * 1024 * 1024
VMEM_LIMIT = V7X_VMEM_BYTES - 8 * 1024 * 1024
LANES = 128
NEG_BIG = -1e30


def _params(sem, **kw):
    return pltpu.CompilerParams(dimension_semantics=sem, vmem_limit_bytes=VMEM_LIMIT, **kw)

def _pick(total, pref):
    t = min(pref, total)
    while total % t:
        t //= 2
    return t


def _ada_kernel(c_ref, w_ref, b_ref, o_ref):
    h = jax.nn.silu(c_ref[...]).astype(BF16)
    o_ref[0] = jnp.dot(h, w_ref[0].astype(BF16), preferred_element_type=F32) + b_ref[0]


def _ada_call(c_all, w_ada, b_ada):
    depth, d, e = w_ada.shape
    bp = c_all.shape[0]
    tn = _pick(e, 1024)
    return pl.pallas_call(
        _ada_kernel,
        grid=(depth, e // tn),
        in_specs=[
            pl.BlockSpec((bp, d), lambda l, j: (0, 0)),
            pl.BlockSpec((1, d, tn), lambda l, j: (l, 0, j)),
            pl.BlockSpec((1, 1, tn), lambda l, j: (l, 0, j)),
        ],
        out_specs=pl.BlockSpec((1, bp, tn), lambda l, j: (l, 0, j)),
        out_shape=jax.ShapeDtypeStruct((depth, bp, e), F32),
        compiler_params=_params(("arbitrary", "arbitrary")),
        name="ada",
    )(c_all, w_ada, b_ada.reshape(depth, 1, e))


class _Layout:
    def __init__(self, b0, n0, b1, n1):
        self.b = (b0, b1)
        self.n = (n0, n1)
        self.t0 = b0 * n0
        self.t = b0 * n0 + b1 * n1

    def batch_of_tile(self, i, tm):
        tok = i * tm
        return jnp.where(tok < self.t0, tok // self.n[0], self.b[0] + (tok - self.t0) // self.n[1])

    def pos_block_of_tile(self, i, tm):
        tok = i * tm
        pos = jnp.where(tok < self.t0, tok % self.n[0], (tok - self.t0) % self.n[1])
        return pos // tm

    def split_specs(self, tm, d):
        nt0 = self.t0 // tm
        return [pl.BlockSpec((tm, d), lambda i, *_: (jnp.minimum(i, nt0 - 1), 0)),
                pl.BlockSpec((tm, d), lambda i, *_: (jnp.maximum(i - nt0, 0), 0))]

    def split_shapes(self, d, dtype):
        return [jax.ShapeDtypeStruct((self.t0, d), dtype), jax.ShapeDtypeStruct((self.t - self.t0, d), dtype)]


def _split_read(xp_ref, xs_ref, nt0):
    return jnp.where(pl.program_id(0) < nt0, xp_ref[...], xs_ref[...])


def _head_norm(z, g):
    return z * lax.rsqrt(jnp.mean(z * z, axis=-1, keepdims=True) + EPS) * g


def _half_norm(z, g, lo):
    zz = z * z
    s_lo = jnp.sum(jnp.where(lo, zz, 0.0), axis=-1, keepdims=True)
    s_hi = jnp.sum(jnp.where(lo, 0.0, zz), axis=-1, keepdims=True)
    inv = jnp.where(lo, lax.rsqrt(s_lo / B_HALF + EPS), lax.rsqrt(s_hi / B_HALF + EPS))
    return z * inv * g


def _rope(z, c, s_signed, even):
    partner = jnp.where(even, pltpu.roll(z, LANES - 1, 1), pltpu.roll(z, 1, 1))
    return z * c + partner * s_signed


def _proj_kernel(xp_ref, xs_ref, mod_ref, gn_ref, w_ref, gqa_ref, gka_ref, gqb_ref, gkb_ref, cos_ref, sin_ref,
                 dft_ref, qa_ref, ka_ref, va_ref, qb_ref, kb_ref, vb_ref, p_ref, q_ref, *, nt0):
    x = _split_read(xp_ref, xs_ref, nt0)
    y = x * lax.rsqrt(jnp.mean(x * x, axis=-1, keepdims=True) + EPS) * gn_ref[...]
    hn = (y * (1.0 + mod_ref[0, 1:2, :]) + mod_ref[0, 0:1, :]).astype(BF16)

    def seg(a, b):
        return jnp.dot(hn, w_ref[:, a:b], preferred_element_type=F32)

    tm = x.shape[0]
    lane = lax.broadcasted_iota(jnp.int32, (tm, LANES), 1)
    even = (lane % 2) == 0
    lo = lane < B_HALF
    cos = cos_ref[...]
    sin = sin_ref[...]

    scale_a = HEAD_DIM ** -0.5 * LOG2E
    z = seg(0, A_Q)
    for h in range(A_HEADS):
        sl = slice(h * HEAD_DIM, (h + 1) * HEAD_DIM)
        qa_ref[:, sl] = (_rope(_head_norm(z[:, sl], gqa_ref[...]), cos, sin, even) * scale_a).astype(BF16)
    off = A_Q
    z = seg(off, off + A_KV)
    for h in range(A_KV_HEADS):
        sl = slice(h * HEAD_DIM, (h + 1) * HEAD_DIM)
        ka_ref[:, sl] = _rope(_head_norm(z[:, sl], gka_ref[...]), cos, sin, even).astype(BF16)
    off += A_KV
    va_ref[...] = seg(off, off + A_KV).astype(BF16)
    off += A_KV

    scale_b = B_HALF ** -0.5 * LOG2E
    z = seg(off, off + B_WIDTH)
    for h in range(B_HEADS):
        sl = slice(h * HEAD_DIM, (h + 1) * HEAD_DIM)
        qb_ref[:, sl] = (_half_norm(z[:, sl], gqb_ref[...], lo) * scale_b).astype(BF16)
    off += B_WIDTH
    z = seg(off, off + B_WIDTH)
    for h in range(B_HEADS):
        sl = slice(h * HEAD_DIM, (h + 1) * HEAD_DIM)
        kb_ref[:, sl] = _half_norm(z[:, sl], gkb_ref[...], lo).astype(BF16)
    off += B_WIDTH
    vb_ref[...] = seg(off, off + B_WIDTH).astype(BF16)
    off += B_WIDTH

    z = seg(off, off + C_WIDTH).astype(BF16)
    for g in range(C_GROUPS):
        sl = slice(g * C_GROUP_W, (g + 1) * C_GROUP_W)
        pq = jnp.dot(z[:, sl], dft_ref[...], preferred_element_type=F32)
        p_ref[:, sl] = pq[:, :C_GROUP_W].astype(BF16)
        q_ref[:, sl] = pq[:, C_GROUP_W:].astype(BF16)


def _proj_call(lay, xp, xs, mod3, gn, w_in, gqa, gka, gqb, gkb, cos_e, sin_e, dft_c, tm):
    t, d = lay.t, xp.shape[1]
    row = lambda i: (i, 0)
    const = lambda i: (0, 0)
    widths = (A_Q, A_KV, A_KV, B_WIDTH, B_WIDTH, B_WIDTH, C_WIDTH, C_WIDTH)
    return pl.pallas_call(
        functools.partial(_proj_kernel, nt0=lay.t0 // tm),
        grid=(t // tm,),
        in_specs=lay.split_specs(tm, d) + [
            pl.BlockSpec((1, 6, d), lambda i: (lay.batch_of_tile(i, tm), 0, 0)),
            pl.BlockSpec((1, d), const),
            pl.BlockSpec((d, IN_WIDTH), const),
            pl.BlockSpec((1, LANES), const),
            pl.BlockSpec((1, LANES), const),
            pl.BlockSpec((1, LANES), const),
            pl.BlockSpec((1, LANES), const),
            pl.BlockSpec((tm, LANES), lambda i: (lay.pos_block_of_tile(i, tm), 0)),
            pl.BlockSpec((tm, LANES), lambda i: (lay.pos_block_of_tile(i, tm), 0)),
            pl.BlockSpec((C_GROUP_W, 2 * C_GROUP_W), const),
        ],
        out_specs=[pl.BlockSpec((tm, w), row) for w in widths],
        out_shape=[jax.ShapeDtypeStruct((t, w), BF16) for w in widths],
        compiler_params=_params(("arbitrary",)),
        name="proj",
    )(xp, xs, mod3, gn, w_in, gqa, gka, gqb, gkb, cos_e, sin_e, dft_c)


def _softmax_step(s, vc, m_ref, l_ref, acc_ref):
    m_prev = m_ref[...]
    m_new = jnp.maximum(m_prev, jnp.max(s, axis=1, keepdims=True))
    alpha = jnp.exp2(m_prev - m_new)
    p = jnp.exp2(s - jnp.tile(m_new, (1, s.shape[1] // LANES)))
    l_ref[...] = alpha * l_ref[...] + jnp.sum(p, axis=1, keepdims=True)
    acc_ref[...] = alpha * acc_ref[...] + jnp.dot(p.astype(BF16), vc, preferred_element_type=F32)
    m_ref[...] = m_new


_NT = (((1,), (1,)), ((), ()))


def _attn_a_kernel(q_ref, k_ref, v_ref, _merged_ref, o_ref, qs_ref, m_ref, l_ref, acc_ref, *, tq, tk, n):
    for g in range(A_GROUP):
        qs_ref[g * tq:(g + 1) * tq, :] = q_ref[:, g * HEAD_DIM:(g + 1) * HEAD_DIM]
    m_ref[...] = jnp.full(m_ref.shape, NEG_BIG, F32)
    l_ref[...] = jnp.zeros(l_ref.shape, F32)
    acc_ref[...] = jnp.zeros(acc_ref.shape, F32)

    def body(j, carry):
        rows = pl.ds(pl.multiple_of(j * tk, tk), tk)
        s = lax.dot_general(qs_ref[...], k_ref[rows, :], _NT, preferred_element_type=F32)
        _softmax_step(s, v_ref[rows, :], m_ref, l_ref, acc_ref)
        return carry

    lax.fori_loop(0, n // tk, body, 0, unroll=True)
    o = acc_ref[...] / l_ref[...]
    for g in range(A_GROUP):
        o_ref[:, g * HEAD_DIM:(g + 1) * HEAD_DIM] = o[g * tq:(g + 1) * tq].astype(o_ref.dtype)


def _attn_a_call(qa, ka, va, merged, tok0, b, n, tq, tk):
    qblk0 = tok0 // tq
    kblk0 = tok0 // n
    nq = n // tq
    rows = A_GROUP * tq
    return pl.pallas_call(
        functools.partial(_attn_a_kernel, tq=tq, tk=tk, n=n),
        grid=(b, A_KV_HEADS, nq),
        in_specs=[
            pl.BlockSpec((tq, A_GROUP * HEAD_DIM), lambda bi, kv, qi: (qblk0 + bi * nq + qi, kv)),
            pl.BlockSpec((n, HEAD_DIM), lambda bi, kv, qi: (kblk0 + bi, kv)),
            pl.BlockSpec((n, HEAD_DIM), lambda bi, kv, qi: (kblk0 + bi, kv)),
            pl.BlockSpec(memory_space=pl.ANY),
        ],
        out_specs=pl.BlockSpec((tq, A_GROUP * HEAD_DIM), lambda bi, kv, qi: (qblk0 + bi * nq + qi, kv)),
        out_shape=jax.ShapeDtypeStruct(merged.shape, merged.dtype),
        input_output_aliases={3: 0},
        scratch_shapes=[
            pltpu.VMEM((rows, HEAD_DIM), BF16),
            pltpu.VMEM((rows, LANES), F32),
            pltpu.VMEM((rows, LANES), F32),
            pltpu.VMEM((rows, HEAD_DIM), F32),
        ],
        compiler_params=_params(("arbitrary", "arbitrary", "arbitrary")),
        name="attn_a",
    )(qa, ka, va, merged)


def _attn_b_kernel(q_ref, k_ref, v_ref, band_ref, lam_ref, gs_ref, _merged_ref, o_ref,
                   qs_ref, m_ref, l_ref, acc_ref, *, t, nchunks, lam_init):
    i = pl.program_id(2)
    q = q_ref[...]
    lo = lax.broadcasted_iota(jnp.int32, q.shape, 1) < B_HALF
    zero = jnp.zeros_like(q)
    qs_ref[0:t, :] = jnp.where(lo, q, zero)
    qs_ref[t:2 * t, :] = jnp.where(lo, zero, q)
    m_ref[...] = jnp.full(m_ref.shape, NEG_BIG, F32)
    l_ref[...] = jnp.zeros(l_ref.shape, F32)
    acc_ref[...] = jnp.zeros(acc_ref.shape, F32)

    def body(j, carry):
        rows = pl.ds(pl.multiple_of(j * t, t), t)
        s = lax.dot_general(qs_ref[...], k_ref[rows, :], _NT, preferred_element_type=F32)
        bias = band_ref[0, jnp.clip(j - i, -_BAND_REACH, _BAND_REACH) + _BAND_REACH]
        s = (s.reshape(2, t, t) + bias[None]).reshape(2 * t, t)
        _softmax_step(s, v_ref[rows, :], m_ref, l_ref, acc_ref)
        return carry

    lax.fori_loop(0, nchunks, body, 0, unroll=min(nchunks, 8))
    o = acc_ref[...] / l_ref[...]
    lq = lam_ref[...]
    lam = (jnp.exp(jnp.sum(lq[0:1] * lq[1:2], axis=-1, keepdims=True))
           - jnp.exp(jnp.sum(lq[2:3] * lq[3:4], axis=-1, keepdims=True)) + lam_init)
    ob = o[0:t] - lam * o[t:2 * t]
    o_ref[...] = (_head_norm(ob, gs_ref[...]) * (1.0 - lam_init)).astype(o_ref.dtype)


def _attn_b_call(qb, kb, vb, band, lam_qk, g_subln, merged, tok0, b, n, t, lam_init):
    qblk0 = tok0 // t
    kblk0 = tok0 // n
    nq = n // t
    return pl.pallas_call(
        functools.partial(_attn_b_kernel, t=t, nchunks=nq, lam_init=lam_init),
        grid=(b, B_HEADS, nq),
        in_specs=[
            pl.BlockSpec((t, HEAD_DIM), lambda bi, h, qi: (qblk0 + bi * nq + qi, h)),
            pl.BlockSpec((n, HEAD_DIM), lambda bi, h, qi: (kblk0 + bi, h)),
            pl.BlockSpec((n, HEAD_DIM), lambda bi, h, qi: (kblk0 + bi, h)),
            pl.BlockSpec((1, 2 * _BAND_REACH + 1, t, t), lambda bi, h, qi: (h, 0, 0, 0)),
            pl.BlockSpec((4, B_HALF), lambda bi, h, qi: (0, 0)),
            pl.BlockSpec((1, LANES), lambda bi, h, qi: (0, 0)),
            pl.BlockSpec(memory_space=pl.ANY),
        ],
        out_specs=pl.BlockSpec((t, HEAD_DIM), lambda bi, h, qi: (qblk0 + bi * nq + qi, h)),
        out_shape=jax.ShapeDtypeStruct(merged.shape, merged.dtype),
        input_output_aliases={6: 0},
        scratch_shapes=[
            pltpu.VMEM((2 * t, HEAD_DIM), BF16),
            pltpu.VMEM((2 * t, LANES), F32),
            pltpu.VMEM((2 * t, LANES), F32),
            pltpu.VMEM((2 * t, HEAD_DIM), F32),
        ],
        compiler_params=_params(("arbitrary", "arbitrary", "arbitrary")),
        name="attn_b",
    )(qb, kb, vb, band, lam_qk, g_subln, merged)


def _t5_bucket(rel):
    nb = NUM_BUCKETS // 2
    ret = (rel > 0).astype(jnp.int32) * nb
    n = jnp.abs(rel)
    max_exact = nb // 2
    nf = jnp.maximum(n, 1).astype(F32)
    large = max_exact + (jnp.log(nf / max_exact) / math.log(MAX_DISTANCE / max_exact)
                         * (nb - max_exact)).astype(jnp.int32)
    large = jnp.minimum(large, nb - 1)
    return ret + jnp.where(n < max_exact, n, large)


_T5_SATURATION = math.ceil((NUM_BUCKETS // 4) * (MAX_DISTANCE / (NUM_BUCKETS // 4)) ** (
    (NUM_BUCKETS // 2 - 1 - NUM_BUCKETS // 4) / (NUM_BUCKETS // 2 - NUM_BUCKETS // 4)))
_BAND_REACH = 2


def _band_kernel(vec_ref, o_ref, *, t):
    x = jnp.broadcast_to(vec_ref[0], (t, 2 * t))
    o_ref[0, 0] = pltpu.roll(x, t + 1, 1, stride=1, stride_axis=0)[:, :t]


def _bias_tables(rel_bias, t):
    assert t + 1 >= _T5_SATURATION
    heads = rel_bias.shape[1]
    nd = 2 * _BAND_REACH + 1
    rel = ((jnp.arange(nd, dtype=jnp.int32)[:, None] - _BAND_REACH) * t
           + jnp.arange(-(t - 1), t + 1, dtype=jnp.int32)[None, :])
    vec = jnp.take(rel_bias, _t5_bucket(rel), axis=0).astype(F32)
    vec = (vec.transpose(2, 0, 1) * LOG2E).reshape(heads * nd, 1, 2 * t)
    return pl.pallas_call(
        functools.partial(_band_kernel, t=t),
        grid=(heads, nd),
        in_specs=[pl.BlockSpec((1, 1, 2 * t), lambda h, d: (h * nd + d, 0, 0))],
        out_specs=pl.BlockSpec((1, 1, t, t), lambda h, d: (h, d, 0, 0)),
        out_shape=jax.ShapeDtypeStruct((heads, nd, t, t), F32),
        compiler_params=_params(("arbitrary", "arbitrary")),
        name="band",
    )(vec)


def _dft_gen_kernel(ck_ref, sk_ref, ca_ref, sa_ref, c_ref, s_ref):
    ck = ck_ref[0]
    sk = sk_ref[0]
    ca = ca_ref[...]
    sa = sa_ref[...]
    c_ref[...] = (ck * ca - sk * sa).astype(c_ref.dtype)
    s_ref[...] = (sk * ca + ck * sa).astype(s_ref.dtype)


def _dft_tables(n, tk):
    t = jnp.arange(n, dtype=jnp.int32)[None, :]
    k0 = (jnp.arange(n // tk, dtype=jnp.int32) * tk)[:, None]
    a = jnp.arange(tk, dtype=jnp.int32)[:, None]
    ang_k = ((k0 * t) % n).astype(F32) * (2.0 * math.pi / n)
    ang_a = ((a * t) % n).astype(F32) * (2.0 * math.pi / n)
    scale = n ** -0.5
    ck = (jnp.cos(ang_k) * scale).reshape(n // tk, 1, n)
    sk = (jnp.sin(ang_k) * scale).reshape(n // tk, 1, n)
    return pl.pallas_call(
        _dft_gen_kernel,
        grid=(n // tk,),
        in_specs=[
            pl.BlockSpec((1, 1, n), lambda i: (i, 0, 0)),
            pl.BlockSpec((1, 1, n), lambda i: (i, 0, 0)),
            pl.BlockSpec((tk, n), lambda i: (0, 0)),
            pl.BlockSpec((tk, n), lambda i: (0, 0)),
        ],
        out_specs=[pl.BlockSpec((tk, n), lambda i: (i, 0)), pl.BlockSpec((tk, n), lambda i: (i, 0))],
        out_shape=[jax.ShapeDtypeStruct((n, n), BF16), jax.ShapeDtypeStruct((n, n), BF16)],
        compiler_params=_params(("arbitrary",)),
        name="dft_gen",
    )(ck, sk, jnp.cos(ang_a), jnp.sin(ang_a))


def _fourier_kernel(c_ref, s_ref, p_ref, q_ref, w_ref, _merged_ref, o_ref):
    f = (jnp.dot(c_ref[...], p_ref[...], preferred_element_type=F32)
         - jnp.dot(s_ref[...], q_ref[...], preferred_element_type=F32)).astype(BF16)
    for g in range(C_GROUPS):
        sl = slice(g * C_GROUP_W, (g + 1) * C_GROUP_W)
        o_ref[:, sl] = jnp.dot(f[:, sl], w_ref[g], preferred_element_type=F32).astype(o_ref.dtype)


def _fourier_call(cmat, smat, p, q, wf, merged, tok0, b, n, tf):
    kblk0 = tok0 // n
    oblk0 = tok0 // tf
    nk = n // tf
    return pl.pallas_call(
        _fourier_kernel,
        grid=(b, nk),
        in_specs=[
            pl.BlockSpec((tf, n), lambda bi, kt: (kt, 0)),
            pl.BlockSpec((tf, n), lambda bi, kt: (kt, 0)),
            pl.BlockSpec((n, C_WIDTH), lambda bi, kt: (kblk0 + bi, 0)),
            pl.BlockSpec((n, C_WIDTH), lambda bi, kt: (kblk0 + bi, 0)),
            pl.BlockSpec((C_GROUPS, C_GROUP_W, C_GROUP_W), lambda bi, kt: (0, 0, 0)),
            pl.BlockSpec(memory_space=pl.ANY),
        ],
        out_specs=pl.BlockSpec((tf, C_WIDTH), lambda bi, kt: (oblk0 + bi * nk + kt, 0)),
        out_shape=jax.ShapeDtypeStruct(merged.shape, merged.dtype),
        input_output_aliases={5: 0},
        compiler_params=_params(("arbitrary", "arbitrary")),
        name="fourier",
    )(cmat, smat, p, q, wf, merged)


def _out_kernel(oa_ref, ob_ref, oc_ref, xp_ref, xs_ref, mod_ref, w_ref, gn_ref, wr_ref,
                x1_ref, hn_ref, ids_ref, *, nt0):
    mix = (jnp.dot(oa_ref[...], w_ref[0:A_Q, :], preferred_element_type=F32)
           + jnp.dot(ob_ref[...], w_ref[A_Q:A_Q + B_WIDTH, :], preferred_element_type=F32)
           + jnp.dot(oc_ref[...], w_ref[A_Q + B_WIDTH:, :], preferred_element_type=F32))
    x1 = _split_read(xp_ref, xs_ref, nt0) + mod_ref[0, 2:3, :] * mix
    x1_ref[...] = x1
    y = x1 * lax.rsqrt(jnp.mean(x1 * x1, axis=-1, keepdims=True) + EPS) * gn_ref[...]
    hn = y * (1.0 + mod_ref[0, 4:5, :]) + mod_ref[0, 3:4, :]
    d = hn.shape[1]
    hn_ref[:, :d] = hn

    logits = jnp.dot(hn.astype(BF16), wr_ref[...], preferred_element_type=F32)
    lane = lax.broadcasted_iota(jnp.int32, logits.shape, 1)
    big = jnp.int32(LANES)
    is_g = lane < N_GROUPS
    gmax = jnp.max(jnp.where(is_g, logits, -jnp.inf), axis=-1, keepdims=True)
    g_sel = jnp.min(jnp.where(jnp.logical_and(is_g, logits == gmax), lane, big), axis=-1, keepdims=True)
    g_w = 1.0 / jnp.sum(jnp.where(is_g, jnp.exp(logits - gmax), 0.0), axis=-1, keepdims=True)
    lo_lane = N_GROUPS + g_sel * EXPERTS_PER_GROUP
    in_g = jnp.logical_and(lane >= lo_lane, lane < lo_lane + EXPERTS_PER_GROUP)
    v0 = jnp.max(jnp.where(in_g, logits, -jnp.inf), axis=-1, keepdims=True)
    i0 = jnp.min(jnp.where(jnp.logical_and(in_g, logits == v0), lane, big), axis=-1, keepdims=True)
    rest = jnp.logical_and(in_g, lane != i0)
    v1 = jnp.max(jnp.where(rest, logits, -jnp.inf), axis=-1, keepdims=True)
    i1 = jnp.min(jnp.where(jnp.logical_and(rest, logits == v1), lane, big), axis=-1, keepdims=True)
    e1 = jnp.exp(v1 - v0)
    w0 = g_w / (1.0 + e1)
    w1 = g_w * e1 / (1.0 + e1)
    swap = i1 < i0
    ea = jnp.where(swap, i1, i0) - N_GROUPS
    eb = jnp.where(swap, i0, i1) - N_GROUPS
    wa = jnp.where(swap, w1, w0)
    wb = jnp.where(swap, w0, w1)
    ml = lax.broadcasted_iota(jnp.int32, ids_ref.shape, 1)
    ids_ref[...] = jnp.where(ml == 0, ea, jnp.where(ml == 1, eb, 0))
    hn_ref[:, d:] = jnp.where(lane == 0, wa, jnp.where(lane == 1, wb, 0.0))


def _out_call(lay, oa, ob, oc, xp, xs, mod3, w_out, gn, w_router, tm):
    t, d = lay.t, xp.shape[1]
    row = lambda i: (i, 0)
    const = lambda i: (0, 0)
    meta = 8
    return pl.pallas_call(
        functools.partial(_out_kernel, nt0=lay.t0 // tm),
        grid=(t // tm,),
        in_specs=[
            pl.BlockSpec((tm, A_Q), row),
            pl.BlockSpec((tm, B_WIDTH), row),
            pl.BlockSpec((tm, C_WIDTH), row),
        ] + lay.split_specs(tm, d) + [
            pl.BlockSpec((1, 6, d), lambda i: (lay.batch_of_tile(i, tm), 0, 0)),
            pl.BlockSpec((d, d), const),
            pl.BlockSpec((1, d), const),
            pl.BlockSpec((d, LANES), const),
        ],
        out_specs=[pl.BlockSpec((tm, d), row), pl.BlockSpec((tm, d + LANES), row),
                   pl.BlockSpec((tm, meta), row)],
        out_shape=[jax.ShapeDtypeStruct((t, d), F32), jax.ShapeDtypeStruct((t, d + LANES), F32),
                   jax.ShapeDtypeStruct((t, meta), jnp.int32)],
        compiler_params=_params(("arbitrary",)),
        name="out_proj",
    )(oa, ob, oc, xp, xs, mod3, w_out, gn, w_router)


def _route_plan(ids, tm):
    t = ids.shape[0]
    ea = ids[:, 0]
    eb = ids[:, 1]
    grp = ea // EXPERTS_PER_GROUP
    la = ea % EXPERTS_PER_GROUP
    lb = eb % EXPERTS_PER_GROUP
    pair = la * (2 * EXPERTS_PER_GROUP - 1 - la) // 2 + (lb - la - 1)
    bucket = grp * len(_PAIRS) + pair
    onehot = (bucket[:, None] == jnp.arange(N_BUCKETS, dtype=jnp.int32)[None, :]).astype(jnp.int32)
    csum = jnp.cumsum(onehot, axis=0)
    counts = csum[-1]
    rank = jnp.take_along_axis(csum, bucket[:, None], axis=1)[:, 0] - 1
    tiles = (counts + tm - 1) // tm
    tile_end = jnp.cumsum(tiles)
    tile_start = tile_end - tiles
    pos = jnp.take(tile_start, bucket) * tm + rank
    assert t % tm == 0
    n_tiles = t // tm + N_BUCKETS
    src = jnp.zeros((n_tiles * tm,), jnp.int32).at[pos].set(jnp.arange(t, dtype=jnp.int32))
    tile_ids = jnp.arange(n_tiles, dtype=jnp.int32)
    used = tile_end[-1]
    tile_bucket = jnp.searchsorted(tile_end, jnp.minimum(tile_ids, used - 1), side="right").astype(jnp.int32)
    pairs = np.array(_PAIRS, np.int32)
    tgrp = tile_bucket // len(_PAIRS)
    tpair = tile_bucket % len(_PAIRS)
    tile_ea = tgrp * EXPERTS_PER_GROUP + jnp.take(jnp.asarray(pairs[:, 0]), tpair)
    tile_eb = tgrp * EXPERTS_PER_GROUP + jnp.take(jnp.asarray(pairs[:, 1]), tpair)
    rows_left = jnp.take(counts, tile_bucket) - (tile_ids - jnp.take(tile_start, tile_bucket)) * tm
    tile_rows = jnp.where(tile_ids < used, jnp.clip(rows_left, 0, tm), 0).astype(jnp.int32)
    return pos.astype(jnp.int32), src, tile_ea.astype(jnp.int32), tile_eb.astype(jnp.int32), tile_rows, used


def _row_copy(src_hbm, row, buf, slot, r, sem):
    return pltpu.make_async_copy(src_hbm.at[pl.ds(row, 1), :], buf.at[slot, pl.ds(r, 1), :], sem.at[slot])


def _gather_start(idx_ref, base, rows, src_hbm, buf, slot, sem):
    for r in range(rows):
        _row_copy(src_hbm, idx_ref[base + r], buf, slot, r, sem).start()


def _gather_wait(rows, src_hbm, buf, slot, sem):
    pltpu.make_async_copy(src_hbm.at[pl.ds(0, rows), :], buf.at[slot], sem.at[slot]).wait()


def _moe_kernel(src_ref, ea_ref, eb_ref, rows_ref, h_hbm, w1a_ref, w3a_ref, w2a_ref, w1b_ref, w3b_ref, w2b_ref,
                y_ref, buf, sem, *, tm):
    i = pl.program_id(0)
    slot = i % 2
    used = rows_ref[i] > 0
    d = y_ref.shape[1]

    @pl.when(i == 0)
    def _():
        _gather_start(src_ref, 0, tm, h_hbm, buf, 0, sem)

    @pl.when(jnp.logical_or(i == 0, rows_ref[jnp.maximum(i - 1, 0)] > 0))
    def _():
        _gather_wait(tm, h_hbm, buf, slot, sem)

    @pl.when(jnp.logical_not(used))
    def _():
        y_ref[...] = jnp.zeros(y_ref.shape, y_ref.dtype)

    @pl.when(used)
    def _():
        _gather_start(src_ref, (i + 1) * tm, tm, h_hbm, buf, 1 - slot, sem)
        h = buf[slot, :, :d].astype(BF16)

        def expert(w1_ref, w3_ref, w2_ref):
            a = (jax.nn.silu(jnp.dot(h, w1_ref[0], preferred_element_type=F32))
                 * jnp.dot(h, w3_ref[0], preferred_element_type=F32))
            return jnp.dot(a.astype(BF16), w2_ref[0], preferred_element_type=F32)

        w = buf[slot, :, d:]
        y_ref[...] = (expert(w1a_ref, w3a_ref, w2a_ref) * w[:, 0:1]
                      + expert(w1b_ref, w3b_ref, w2b_ref) * w[:, 1:2])


def _moe_call(src, tile_ea, tile_eb, tile_rows, hn, w1, w3, w2, tm):
    n_tiles = tile_ea.shape[0]
    d = w1.shape[1]
    f = w1.shape[2]
    wa = lambda i, src, ea, eb, rows: (ea[i], 0, 0)
    wb = lambda i, src, ea, eb, rows: (eb[i], 0, 0)
    grid_spec = pltpu.PrefetchScalarGridSpec(
        num_scalar_prefetch=4,
        grid=(n_tiles,),
        in_specs=[
            pl.BlockSpec(memory_space=pl.ANY),
            pl.BlockSpec((1, d, f), wa), pl.BlockSpec((1, d, f), wa), pl.BlockSpec((1, f, d), wa),
            pl.BlockSpec((1, d, f), wb), pl.BlockSpec((1, d, f), wb), pl.BlockSpec((1, f, d), wb),
        ],
        out_specs=pl.BlockSpec((tm, d), lambda i, *_: (i, 0)),
        scratch_shapes=[pltpu.VMEM((2, tm, hn.shape[1]), F32), pltpu.SemaphoreType.DMA((2,))],
    )
    return pl.pallas_call(
        functools.partial(_moe_kernel, tm=tm),
        grid_spec=grid_spec,
        out_shape=jax.ShapeDtypeStruct((n_tiles * tm, d), F32),
        compiler_params=_params(("arbitrary",)),
        name="moe",
    )(src, tile_ea, tile_eb, tile_rows, hn, w1, w3, w2, w1, w3, w2)


def _combine_kernel(pos_ref, y_hbm, x1_ref, mod_ref, op_ref, os_ref, buf, sem, *, tm, nt0):
    i = pl.program_id(0)
    nt = pl.num_programs(0)
    slot = i % 2

    @pl.when(i == 0)
    def _():
        _gather_start(pos_ref, 0, tm, y_hbm, buf, 0, sem)

    _gather_wait(tm, y_hbm, buf, slot, sem)

    @pl.when(i + 1 < nt)
    def _():
        _gather_start(pos_ref, (i + 1) * tm, tm, y_hbm, buf, 1 - slot, sem)

    out = x1_ref[...] + mod_ref[0, 5:6, :] * buf[slot]

    @pl.when(i < nt0)
    def _():
        op_ref[...] = out

    @pl.when(i >= nt0)
    def _():
        os_ref[...] = out


def _combine_call(lay, pos, y_sorted, x1, mod3, tm):
    t, d = x1.shape
    grid_spec = pltpu.PrefetchScalarGridSpec(
        num_scalar_prefetch=1,
        grid=(t // tm,),
        in_specs=[
            pl.BlockSpec(memory_space=pl.ANY),
            pl.BlockSpec((tm, d), lambda i, pos: (i, 0)),
            pl.BlockSpec((1, 6, d), lambda i, pos: (lay.batch_of_tile(i, tm), 0, 0)),
        ],
        out_specs=lay.split_specs(tm, d),
        scratch_shapes=[pltpu.VMEM((2, tm, d), F32), pltpu.SemaphoreType.DMA((2,))],
    )
    return pl.pallas_call(
        functools.partial(_combine_kernel, tm=tm, nt0=lay.t0 // tm),
        grid_spec=grid_spec,
        out_shape=lay.split_shapes(d, F32),
        compiler_params=_params(("arbitrary",)),
        name="combine",
    )(pos, y_sorted, x1, mod3)


def _rope_tables(n):
    n_rows = n // GRID_W
    rows = jnp.repeat(jnp.arange(n_rows), GRID_W).astype(F32)
    cols = jnp.tile(jnp.arange(GRID_W), n_rows).astype(F32)
    half = HEAD_DIM // 2
    inv = ROPE_THETA ** (-jnp.arange(0, half, 2, dtype=F32) / half)
    ang = jnp.concatenate([rows[:, None] * inv, cols[:, None] * inv], axis=-1)
    sign = jnp.tile(jnp.array([-1.0, 1.0], F32), half)
    return jnp.repeat(jnp.cos(ang), 2, axis=-1), jnp.repeat(jnp.sin(ang), 2, axis=-1) * sign


def _lambda_init(layer_idx):
    return 0.8 - 0.6 * math.exp(-0.3 * layer_idx)


def _tiled_gain(g):
    return jnp.tile(g, LANES // g.shape[-1]).reshape(1, LANES).astype(F32)


def kernel(x_prompt, x_sample, c_prompt, c_sample, rel_bias, w_ada, b_ada, g_norm_mix, w_in, g_qa, g_ka,
           g_qb, g_kb, lam_qk, g_subln, w_fourier, w_out, g_norm_ffn, w_group, w_expert, w1, w3, w2):
    b0, n0, d = x_prompt.shape
    b1, n1, _ = x_sample.shape
    depth = w_in.shape[0]
    lay = _Layout(b0, n0, b1, n1)
    trunks = ((0, b0, n0), (lay.t0, b1, n1))
    n_max = max(n0, n1)

    tm = _pick(math.gcd(n0, n1), 512)
    tq_a = _pick(math.gcd(n0, n1), 256)
    tk_a = _pick(math.gcd(n0, n1), 2048)
    t_b = _pick(math.gcd(n0, n1), 512)
    tf = _pick(math.gcd(n0, n1), 128)
    tm_moe = 256

    xp = x_prompt.reshape(b0 * n0, d)
    xs = x_sample.reshape(b1 * n1, d)
    nb = b0 + b1
    bp = -(-nb // 8) * 8
    c_all = jnp.zeros((bp, d), F32).at[:nb].set(jnp.concatenate([c_prompt, c_sample], axis=0))
    mod = _ada_call(c_all, w_ada, b_ada).reshape(depth, bp, 6, d)

    cos_e, sin_e = _rope_tables(n_max)
    band = _bias_tables(rel_bias, t_b)
    dft = {n: _dft_tables(n, _pick(n, 128)) for n in sorted({n0, n1})}
    cidx = jnp.arange(C_GROUP_W, dtype=jnp.int32)
    ang_c = ((cidx[:, None] * cidx[None, :]) % C_GROUP_W).astype(F32) * (2.0 * math.pi / C_GROUP_W)
    dft_c = (jnp.concatenate([jnp.cos(ang_c), jnp.sin(ang_c)], axis=1) * C_GROUP_W ** -0.5).astype(BF16)

    for l in range(depth):
        mod3 = mod[l]
        lam_init = _lambda_init(l)
        qa, ka, va, qb, kb, vb, p, q = _proj_call(
            lay, xp, xs, mod3, g_norm_mix[l].reshape(1, d), w_in[l].astype(BF16),
            _tiled_gain(g_qa[l]), _tiled_gain(g_ka[l]), _tiled_gain(g_qb[l]), _tiled_gain(g_kb[l]),
            cos_e, sin_e, dft_c, tm)

        wf = w_fourier[l].astype(BF16)
        oa = jnp.zeros((lay.t, A_Q), BF16)
        ob = jnp.zeros((lay.t, B_WIDTH), BF16)
        oc = jnp.zeros((lay.t, C_WIDTH), BF16)
        for tok0, b, n in trunks:
            oa = _attn_a_call(qa, ka, va, oa, tok0, b, n, tq_a, min(tk_a, max(n // 2, LANES)))
            ob = _attn_b_call(qb, kb, vb, band, lam_qk[l], _tiled_gain(g_subln[l]), ob,
                              tok0, b, n, t_b, lam_init)
            oc = _fourier_call(dft[n][0], dft[n][1], p, q, wf, oc, tok0, b, n, tf)

        w_router = jnp.zeros((d, LANES), F32).at[:, :N_GROUPS].set(w_group[l])
        w_router = w_router.at[:, N_GROUPS:N_GROUPS + N_EXPERTS].set(w_expert[l]).astype(BF16)
        x1, hn2, ids = _out_call(lay, oa, ob, oc, xp, xs, mod3, w_out[l].astype(BF16),
                                 g_norm_ffn[l].reshape(1, d), w_router, tm)

        pos, src, tile_ea, tile_eb, tile_rows, _ = _route_plan(ids, tm_moe)
        y_sorted = _moe_call(src, tile_ea, tile_eb, tile_rows, hn2, w1[l].astype(BF16), w3[l].astype(BF16),
                             w2[l].astype(BF16), tm_moe)
        xp, xs = _combine_call(lay, pos, y_sorted, x1, mod3, tm)

    return (xp.reshape(b0, n0, d), xs.reshape(b1, n1, d))
```

```python
import functools
import math

import jax
import jax.numpy as jnp
import numpy as np
from jax import lax
from jax.experimental import pallas as pl
from jax.experimental.pallas import tpu as pltpu

F32 = jnp.float32
BF16 = jnp.bfloat16

D_MODEL = 2048
HEAD_DIM = 128
A_HEADS = 8
A_KV_HEADS = 2
A_GROUP = A_HEADS // A_KV_HEADS
A_Q = A_HEADS * HEAD_DIM
A_KV = A_KV_HEADS * HEAD_DIM
B_HEADS = 4
B_HALF = HEAD_DIM // 2
B_WIDTH = B_HEADS * HEAD_DIM
C_WIDTH = 512
C_GROUPS = 4
C_GROUP_W = 128
IN_WIDTH = A_Q + 2 * A_KV + 3 * B_WIDTH + C_WIDTH
GRID_W = 64
ROPE_THETA = 10000.0
NUM_BUCKETS = 32
MAX_DISTANCE = 128
N_GROUPS = 4
EXPERTS_PER_GROUP = 4
N_EXPERTS = 16
D_FF_EXPERT = 512
EPS = 1e-6
LOG2E = 1.4426950408889634

_PAIRS = ((0, 1), (0, 2), (0, 3), (1, 2), (1, 3), (2, 3))
N_BUCKETS = N_GROUPS * len(_PAIRS)

V7X_VMEM_BYTES = 64 * 1024 * 1024
VMEM_LIMIT = V7X_VMEM_BYTES - 8 * 1024 * 1024
LANES = 128
NEG_BIG = -1e30


def _params(sem, **kw):
    return pltpu.CompilerParams(dimension_semantics=sem, vmem_limit_bytes=VMEM_LIMIT, **kw)

def _pick(total, pref):
    t = min(pref, total)
    while total % t:
        t //= 2
    return t


def _ada_kernel(c_ref, w_ref, b_ref, o_ref):
    h = jax.nn.silu(c_ref[...]).astype(BF16)
    o_ref[0] = jnp.dot(h, w_ref[0].astype(BF16), preferred_element_type=F32) + b_ref[0]


def _ada_call(c_all, w_ada, b_ada):
    depth, d, e = w_ada.shape
    bp = c_all.shape[0]
    tn = _pick(e, 1024)
    return pl.pallas_call(
        _ada_kernel,
        grid=(depth, e // tn),
        in_specs=[
            pl.BlockSpec((bp, d), lambda l, j: (0, 0)),
            pl.BlockSpec((1, d, tn), lambda l, j: (l, 0, j)),
            pl.BlockSpec((1, 1, tn), lambda l, j: (l, 0, j)),
        ],
        out_specs=pl.BlockSpec((1, bp, tn), lambda l, j: (l, 0, j)),
        out_shape=jax.ShapeDtypeStruct((depth, bp, e), F32),
        compiler_params=_params(("arbitrary", "arbitrary")),
        name="ada",
    )(c_all, w_ada, b_ada.reshape(depth, 1, e))


class _Layout:
    def __init__(self, b0, n0, b1, n1):
        self.b = (b0, b1)
        self.n = (n0, n1)
        self.t0 = b0 * n0
        self.t = b0 * n0 + b1 * n1

    def batch_of_tile(self, i, tm):
        tok = i * tm
        return jnp.where(tok < self.t0, tok // self.n[0], self.b[0] + (tok - self.t0) // self.n[1])

    def pos_block_of_tile(self, i, tm):
        tok = i * tm
        pos = jnp.where(tok < self.t0, tok % self.n[0], (tok - self.t0) % self.n[1])
        return pos // tm

    def split_specs(self, tm, d):
        nt0 = self.t0 // tm
        return [pl.BlockSpec((tm, d), lambda i, *_: (jnp.minimum(i, nt0 - 1), 0)),
                pl.BlockSpec((tm, d), lambda i, *_: (jnp.maximum(i - nt0, 0), 0))]

    def split_shapes(self, d, dtype):
        return [jax.ShapeDtypeStruct((self.t0, d), dtype), jax.ShapeDtypeStruct((self.t - self.t0, d), dtype)]


def _split_apply(xp_ref, xs_ref, nt0, body, *rest):
    i = pl.program_id(0)

    @pl.when(i < nt0)
    def _():
        body(xp_ref, *rest)

    @pl.when(i >= nt0)
    def _():
        body(xs_ref, *rest)


def _head_norm(z, g):
    return z * lax.rsqrt(jnp.mean(z * z, axis=-1, keepdims=True) + EPS) * g


def _half_norm(z, g, lo):
    zz = z * z
    s_lo = jnp.sum(jnp.where(lo, zz, 0.0), axis=-1, keepdims=True)
    s_hi = jnp.sum(jnp.where(lo, 0.0, zz), axis=-1, keepdims=True)
    inv = jnp.where(lo, lax.rsqrt(s_lo / B_HALF + EPS), lax.rsqrt(s_hi / B_HALF + EPS))
    return z * inv * g


def _rope(z, c, s_signed, even):
    partner = jnp.where(even, pltpu.roll(z, LANES - 1, 1), pltpu.roll(z, 1, 1))
    return z * c + partner * s_signed


def _proj_kernel(xp_ref, xs_ref, *rest, nt0):
    _split_apply(xp_ref, xs_ref, nt0, _proj_body, *rest)


def _proj_body(x_ref, mod_ref, gn_ref, w_ref, gqa_ref, gka_ref, gqb_ref, gkb_ref, cos_ref, sin_ref,
               dft_ref, qa_ref, ka_ref, va_ref, qb_ref, kb_ref, vb_ref, p_ref, q_ref):
    x = x_ref[...]
    y = x * lax.rsqrt(jnp.mean(x * x, axis=-1, keepdims=True) + EPS) * gn_ref[...]
    hn = (y * (1.0 + mod_ref[0, 1:2, :]) + mod_ref[0, 0:1, :]).astype(BF16)

    def seg(a, b):
        return jnp.dot(hn, w_ref[:, a:b], preferred_element_type=F32)

    tm = x.shape[0]
    lane = lax.broadcasted_iota(jnp.int32, (tm, LANES), 1)
    even = (lane % 2) == 0
    lo = lane < B_HALF
    cos = cos_ref[...]
    sin = sin_ref[...]

    scale_a = HEAD_DIM ** -0.5 * LOG2E
    z = seg(0, A_Q)
    for h in range(A_HEADS):
        sl = slice(h * HEAD_DIM, (h + 1) * HEAD_DIM)
        qa_ref[:, sl] = (_rope(_head_norm(z[:, sl], gqa_ref[...]), cos, sin, even) * scale_a).astype(BF16)
    off = A_Q
    z = seg(off, off + A_KV)
    for h in range(A_KV_HEADS):
        sl = slice(h * HEAD_DIM, (h + 1) * HEAD_DIM)
        ka_ref[:, sl] = _rope(_head_norm(z[:, sl], gka_ref[...]), cos, sin, even).astype(BF16)
    off += A_KV
    va_ref[...] = seg(off, off + A_KV).astype(BF16)
    off += A_KV

    scale_b = B_HALF ** -0.5 * LOG2E
    z = seg(off, off + B_WIDTH)
    for h in range(B_HEADS):
        sl = slice(h * HEAD_DIM, (h + 1) * HEAD_DIM)
        qb_ref[:, sl] = (_half_norm(z[:, sl], gqb_ref[...], lo) * scale_b).astype(BF16)
    off += B_WIDTH
    z = seg(off, off + B_WIDTH)
    for h in range(B_HEADS):
        sl = slice(h * HEAD_DIM, (h + 1) * HEAD_DIM)
        kb_ref[:, sl] = _half_norm(z[:, sl], gkb_ref[...], lo).astype(BF16)
    off += B_WIDTH
    vb_ref[...] = seg(off, off + B_WIDTH).astype(BF16)
    off += B_WIDTH

    z = seg(off, off + C_WIDTH).astype(BF16)
    for g in range(C_GROUPS):
        sl = slice(g * C_GROUP_W, (g + 1) * C_GROUP_W)
        pq = jnp.dot(z[:, sl], dft_ref[...], preferred_element_type=F32)
        p_ref[:, sl] = pq[:, :C_GROUP_W].astype(BF16)
        q_ref[:, sl] = pq[:, C_GROUP_W:].astype(BF16)


def _proj_call(lay, xp, xs, mod3, gn, w_in, gqa, gka, gqb, gkb, cos_e, sin_e, dft_c, tm):
    t, d = lay.t, xp.shape[1]
    row = lambda i: (i, 0)
    const = lambda i: (0, 0)
    widths = (A_Q, A_KV, A_KV, B_WIDTH, B_WIDTH, B_WIDTH, C_WIDTH, C_WIDTH)
    return pl.pallas_call(
        functools.partial(_proj_kernel, nt0=lay.t0 // tm),
        grid=(t // tm,),
        in_specs=lay.split_specs(tm, d) + [
            pl.BlockSpec((1, 6, d), lambda i: (lay.batch_of_tile(i, tm), 0, 0)),
            pl.BlockSpec((1, d), const),
            pl.BlockSpec((d, IN_WIDTH), const),
            pl.BlockSpec((1, LANES), const),
            pl.BlockSpec((1, LANES), const),
            pl.BlockSpec((1, LANES), const),
            pl.BlockSpec((1, LANES), const),
            pl.BlockSpec((tm, LANES), lambda i: (lay.pos_block_of_tile(i, tm), 0)),
            pl.BlockSpec((tm, LANES), lambda i: (lay.pos_block_of_tile(i, tm), 0)),
            pl.BlockSpec((C_GROUP_W, 2 * C_GROUP_W), const),
        ],
        out_specs=[pl.BlockSpec((tm, w), row) for w in widths],
        out_shape=[jax.ShapeDtypeStruct((t, w), BF16) for w in widths],
        compiler_params=_params(("arbitrary",)),
        name="proj",
    )(xp, xs, mod3, gn, w_in, gqa, gka, gqb, gkb, cos_e, sin_e, dft_c)


def _softmax_step(s, vc, m_ref, l_ref, acc_ref):
    m_prev = m_ref[...]
    m_new = jnp.maximum(m_prev, jnp.max(s, axis=1, keepdims=True))
    alpha = jnp.exp2(m_prev - m_new)
    p = jnp.exp2(s - jnp.tile(m_new, (1, s.shape[1] // LANES)))
    l_ref[...] = alpha * l_ref[...] + jnp.sum(p, axis=1, keepdims=True)
    acc_ref[...] = alpha * acc_ref[...] + jnp.dot(p.astype(BF16), vc, preferred_element_type=F32)
    m_ref[...] = m_new


_NT = (((1,), (1,)), ((), ()))


def _attn_a_kernel(q_ref, k_ref, v_ref, _merged_ref, o_ref, qs_ref, m_ref, l_ref, acc_ref, *, tq, tk, n):
    for g in range(A_GROUP):
        qs_ref[g * tq:(g + 1) * tq, :] = q_ref[:, g * HEAD_DIM:(g + 1) * HEAD_DIM]
    m_ref[...] = jnp.full(m_ref.shape, NEG_BIG, F32)
    l_ref[...] = jnp.zeros(l_ref.shape, F32)
    acc_ref[...] = jnp.zeros(acc_ref.shape, F32)

    def body(j, carry):
        rows = pl.ds(pl.multiple_of(j * tk, tk), tk)
        s = lax.dot_general(qs_ref[...], k_ref[rows, :], _NT, preferred_element_type=F32)
        _softmax_step(s, v_ref[rows, :], m_ref, l_ref, acc_ref)
        return carry

    lax.fori_loop(0, n // tk, body, 0, unroll=True)
    o = acc_ref[...] / l_ref[...]
    for g in range(A_GROUP):
        o_ref[:, g * HEAD_DIM:(g + 1) * HEAD_DIM] = o[g * tq:(g + 1) * tq].astype(o_ref.dtype)


def _attn_a_call(qa, ka, va, merged, tok0, b, n, tq, tk):
    qblk0 = tok0 // tq
    kblk0 = tok0 // n
    nq = n // tq
    rows = A_GROUP * tq
    return pl.pallas_call(
        functools.partial(_attn_a_kernel, tq=tq, tk=tk, n=n),
        grid=(b, A_KV_HEADS, nq),
        in_specs=[
            pl.BlockSpec((tq, A_GROUP * HEAD_DIM), lambda bi, kv, qi: (qblk0 + bi * nq + qi, kv)),
            pl.BlockSpec((n, HEAD_DIM), lambda bi, kv, qi: (kblk0 + bi, kv)),
            pl.BlockSpec((n, HEAD_DIM), lambda bi, kv, qi: (kblk0 + bi, kv)),
            pl.BlockSpec(memory_space=pl.ANY),
        ],
        out_specs=pl.BlockSpec((tq, A_GROUP * HEAD_DIM), lambda bi, kv, qi: (qblk0 + bi * nq + qi, kv)),
        out_shape=jax.ShapeDtypeStruct(merged.shape, merged.dtype),
        input_output_aliases={3: 0},
        scratch_shapes=[
            pltpu.VMEM((rows, HEAD_DIM), BF16),
            pltpu.VMEM((rows, LANES), F32),
            pltpu.VMEM((rows, LANES), F32),
            pltpu.VMEM((rows, HEAD_DIM), F32),
        ],
        compiler_params=_params(("arbitrary", "arbitrary", "arbitrary")),
        name="attn_a",
    )(qa, ka, va, merged)


def _attn_b_kernel(q_ref, k_ref, v_ref, band_ref, lam_ref, gs_ref, _merged_ref, o_ref,
                   qs_ref, m_ref, l_ref, acc_ref, *, t, ck, nchunks, lam_init):
    i = pl.program_id(2)
    c = ck // t
    q = q_ref[...]
    lo = lax.broadcasted_iota(jnp.int32, q.shape, 1) < B_HALF
    zero = jnp.zeros_like(q)
    qs_ref[0:t, :] = jnp.where(lo, q, zero)
    qs_ref[t:2 * t, :] = jnp.where(lo, zero, q)
    m_ref[...] = jnp.full(m_ref.shape, NEG_BIG, F32)
    l_ref[...] = jnp.zeros(l_ref.shape, F32)
    acc_ref[...] = jnp.zeros(acc_ref.shape, F32)

    def body(j, carry):
        rows = pl.ds(pl.multiple_of(j * ck, ck), ck)
        s = lax.dot_general(qs_ref[...], k_ref[rows, :], _NT, preferred_element_type=F32)
        lo_off, hi_off = _band_offsets(c)
        bias = band_ref[0, jnp.clip(c * j - i, lo_off, hi_off) - lo_off]
        s = (s.reshape(2, t, ck) + bias[None]).reshape(2 * t, ck)
        _softmax_step(s, v_ref[rows, :], m_ref, l_ref, acc_ref)
        return carry

    lax.fori_loop(0, nchunks, body, 0, unroll=True)
    o = acc_ref[...] / l_ref[...]
    lq = lam_ref[...]
    lam = (jnp.exp(jnp.sum(lq[0:1] * lq[1:2], axis=-1, keepdims=True))
           - jnp.exp(jnp.sum(lq[2:3] * lq[3:4], axis=-1, keepdims=True)) + lam_init)
    ob = o[0:t] - lam * o[t:2 * t]
    o_ref[...] = (_head_norm(ob, gs_ref[...]) * (1.0 - lam_init)).astype(o_ref.dtype)


def _attn_b_call(qb, kb, vb, band, lam_qk, g_subln, merged, tok0, b, n, t, lam_init):
    qblk0 = tok0 // t
    kblk0 = tok0 // n
    nq = n // t
    ck = band.shape[3]
    return pl.pallas_call(
        functools.partial(_attn_b_kernel, t=t, ck=ck, nchunks=n // ck, lam_init=lam_init),
        grid=(b, B_HEADS, nq),
        in_specs=[
            pl.BlockSpec((t, HEAD_DIM), lambda bi, h, qi: (qblk0 + bi * nq + qi, h)),
            pl.BlockSpec((n, HEAD_DIM), lambda bi, h, qi: (kblk0 + bi, h)),
            pl.BlockSpec((n, HEAD_DIM), lambda bi, h, qi: (kblk0 + bi, h)),
            pl.BlockSpec((1,) + band.shape[1:], lambda bi, h, qi: (h, 0, 0, 0), pipeline_mode=pl.Buffered(1)),
            pl.BlockSpec((4, B_HALF), lambda bi, h, qi: (0, 0)),
            pl.BlockSpec((1, LANES), lambda bi, h, qi: (0, 0)),
            pl.BlockSpec(memory_space=pl.ANY),
        ],
        out_specs=pl.BlockSpec((t, HEAD_DIM), lambda bi, h, qi: (qblk0 + bi * nq + qi, A_Q // HEAD_DIM + h)),
        out_shape=jax.ShapeDtypeStruct(merged.shape, merged.dtype),
        input_output_aliases={6: 0},
        scratch_shapes=[
            pltpu.VMEM((2 * t, HEAD_DIM), BF16),
            pltpu.VMEM((2 * t, LANES), F32),
            pltpu.VMEM((2 * t, LANES), F32),
            pltpu.VMEM((2 * t, HEAD_DIM), F32),
        ],
        compiler_params=_params(("arbitrary", "arbitrary", "arbitrary")),
        name="attn_b",
    )(qb, kb, vb, band, lam_qk, g_subln, merged)


def _t5_bucket(rel):
    nb = NUM_BUCKETS // 2
    ret = (rel > 0).astype(jnp.int32) * nb
    n = jnp.abs(rel)
    max_exact = nb // 2
    nf = jnp.maximum(n, 1).astype(F32)
    large = max_exact + (jnp.log(nf / max_exact) / math.log(MAX_DISTANCE / max_exact)
                         * (nb - max_exact)).astype(jnp.int32)
    large = jnp.minimum(large, nb - 1)
    return ret + jnp.where(n < max_exact, n, large)


_T5_SATURATION = math.ceil((NUM_BUCKETS // 4) * (MAX_DISTANCE / (NUM_BUCKETS // 4)) ** (
    (NUM_BUCKETS // 2 - 1 - NUM_BUCKETS // 4) / (NUM_BUCKETS // 2 - NUM_BUCKETS // 4)))


def _band_offsets(c):
    return -(c + 1), 2


def _band_kernel(vec_ref, o_ref, *, t, ck):
    width = vec_ref.shape[2]
    x = jnp.broadcast_to(vec_ref[0], (t, width))
    o_ref[0, 0] = pltpu.roll(x, width - t + 1, 1, stride=1, stride_axis=0)[:, :ck]


def _bias_tables(rel_bias, t, c):
    assert t + 1 >= _T5_SATURATION
    heads = rel_bias.shape[1]
    lo_off, hi_off = _band_offsets(c)
    nd = hi_off - lo_off + 1
    ck = c * t
    width = (c + 2) * t
    rel = ((jnp.arange(nd, dtype=jnp.int32)[:, None] + lo_off) * t
           + jnp.arange(-(t - 1), width - (t - 1), dtype=jnp.int32)[None, :])
    vec = jnp.take(rel_bias, _t5_bucket(rel), axis=0).astype(F32)
    vec = (vec.transpose(2, 0, 1) * LOG2E).reshape(heads * nd, 1, width)
    return pl.pallas_call(
        functools.partial(_band_kernel, t=t, ck=ck),
        grid=(heads, nd),
        in_specs=[pl.BlockSpec((1, 1, width), lambda h, d: (h * nd + d, 0, 0))],
        out_specs=pl.BlockSpec((1, 1, t, ck), lambda h, d: (h, d, 0, 0)),
        out_shape=jax.ShapeDtypeStruct((heads, nd, t, ck), F32),
        compiler_params=_params(("arbitrary", "arbitrary")),
        name="band",
    )(vec)


def _dft_gen_kernel(ck_ref, sk_ref, ca_ref, sa_ref, c_ref, s_ref):
    ck = ck_ref[0]
    sk = sk_ref[0]
    ca = ca_ref[...]
    sa = sa_ref[...]
    c_ref[...] = (ck * ca - sk * sa).astype(c_ref.dtype)
    s_ref[...] = (sk * ca + ck * sa).astype(s_ref.dtype)


def _dft_tables(n, tk):
    t = jnp.arange(n, dtype=jnp.int32)[None, :]
    k0 = (jnp.arange(n // tk, dtype=jnp.int32) * tk)[:, None]
    a = jnp.arange(tk, dtype=jnp.int32)[:, None]
    ang_k = ((k0 * t) % n).astype(F32) * (2.0 * math.pi / n)
    ang_a = ((a * t) % n).astype(F32) * (2.0 * math.pi / n)
    scale = n ** -0.5
    ck = (jnp.cos(ang_k) * scale).reshape(n // tk, 1, n)
    sk = (jnp.sin(ang_k) * scale).reshape(n // tk, 1, n)
    return pl.pallas_call(
        _dft_gen_kernel,
        grid=(n // tk,),
        in_specs=[
            pl.BlockSpec((1, 1, n), lambda i: (i, 0, 0)),
            pl.BlockSpec((1, 1, n), lambda i: (i, 0, 0)),
            pl.BlockSpec((tk, n), lambda i: (0, 0)),
            pl.BlockSpec((tk, n), lambda i: (0, 0)),
        ],
        out_specs=[pl.BlockSpec((tk, n), lambda i: (i, 0)), pl.BlockSpec((tk, n), lambda i: (i, 0))],
        out_shape=[jax.ShapeDtypeStruct((n, n), BF16), jax.ShapeDtypeStruct((n, n), BF16)],
        compiler_params=_params(("arbitrary",)),
        name="dft_gen",
    )(ck, sk, jnp.cos(ang_a), jnp.sin(ang_a))


def _fourier_kernel(c_ref, s_ref, p_ref, q_ref, w_ref, _merged_ref, o_ref):
    f = (jnp.dot(c_ref[...], p_ref[...], preferred_element_type=F32)
         - jnp.dot(s_ref[...], q_ref[...], preferred_element_type=F32)).astype(BF16)
    for g in range(C_GROUPS):
        sl = slice(g * C_GROUP_W, (g + 1) * C_GROUP_W)
        o_ref[:, sl] = jnp.dot(f[:, sl], w_ref[g], preferred_element_type=F32).astype(o_ref.dtype)


def _fourier_call(cmat, smat, p, q, wf, merged, tok0, b, n, tf):
    kblk0 = tok0 // n
    oblk0 = tok0 // tf
    nk = n // tf
    return pl.pallas_call(
        _fourier_kernel,
        grid=(b, nk),
        in_specs=[
            pl.BlockSpec((tf, n), lambda bi, kt: (kt, 0)),
            pl.BlockSpec((tf, n), lambda bi, kt: (kt, 0)),
            pl.BlockSpec((n, C_WIDTH), lambda bi, kt: (kblk0 + bi, 0)),
            pl.BlockSpec((n, C_WIDTH), lambda bi, kt: (kblk0 + bi, 0)),
            pl.BlockSpec((C_GROUPS, C_GROUP_W, C_GROUP_W), lambda bi, kt: (0, 0, 0)),
            pl.BlockSpec(memory_space=pl.ANY),
        ],
        out_specs=pl.BlockSpec((tf, C_WIDTH), lambda bi, kt: (oblk0 + bi * nk + kt, (A_Q + B_WIDTH) // C_WIDTH)),
        out_shape=jax.ShapeDtypeStruct(merged.shape, merged.dtype),
        input_output_aliases={5: 0},
        compiler_params=_params(("arbitrary", "arbitrary")),
        name="fourier",
    )(cmat, smat, p, q, wf, merged)


def _out_kernel(xp_ref, xs_ref, *rest, nt0):
    _split_apply(xp_ref, xs_ref, nt0, _out_body, *rest)


def _out_body(x_ref, mix_ref, mod_ref, w_ref, gn_ref, wr_ref, x1_ref, hn_ref, ids_ref):
    mix = jnp.dot(mix_ref[...], w_ref[...], preferred_element_type=F32)
    x1 = x_ref[...] + mod_ref[0, 2:3, :] * mix
    x1_ref[...] = x1
    y = x1 * lax.rsqrt(jnp.mean(x1 * x1, axis=-1, keepdims=True) + EPS) * gn_ref[...]
    hn = y * (1.0 + mod_ref[0, 4:5, :]) + mod_ref[0, 3:4, :]
    d = hn.shape[1]
    hn_ref[:, :d] = hn

    logits = jnp.dot(hn.astype(BF16), wr_ref[...], preferred_element_type=F32)
    lane = lax.broadcasted_iota(jnp.int32, logits.shape, 1)
    big = jnp.int32(LANES)
    is_g = lane < N_GROUPS
    gmax = jnp.max(jnp.where(is_g, logits, -jnp.inf), axis=-1, keepdims=True)
    g_sel = jnp.min(jnp.where(jnp.logical_and(is_g, logits == gmax), lane, big), axis=-1, keepdims=True)
    g_w = 1.0 / jnp.sum(jnp.where(is_g, jnp.exp(logits - gmax), 0.0), axis=-1, keepdims=True)
    lo_lane = N_GROUPS + g_sel * EXPERTS_PER_GROUP
    in_g = jnp.logical_and(lane >= lo_lane, lane < lo_lane + EXPERTS_PER_GROUP)
    v0 = jnp.max(jnp.where(in_g, logits, -jnp.inf), axis=-1, keepdims=True)
    i0 = jnp.min(jnp.where(jnp.logical_and(in_g, logits == v0), lane, big), axis=-1, keepdims=True)
    rest = jnp.logical_and(in_g, lane != i0)
    v1 = jnp.max(jnp.where(rest, logits, -jnp.inf), axis=-1, keepdims=True)
    i1 = jnp.min(jnp.where(jnp.logical_and(rest, logits == v1), lane, big), axis=-1, keepdims=True)
    e1 = jnp.exp(v1 - v0)
    w0 = g_w / (1.0 + e1)
    w1 = g_w * e1 / (1.0 + e1)
    swap = i1 < i0
    ea = jnp.where(swap, i1, i0) - N_GROUPS
    eb = jnp.where(swap, i0, i1) - N_GROUPS
    wa = jnp.where(swap, w1, w0)
    wb = jnp.where(swap, w0, w1)
    ml = lax.broadcasted_iota(jnp.int32, ids_ref.shape, 1)
    ids_ref[...] = jnp.where(ml == 0, ea, jnp.where(ml == 1, eb, 0))
    hn_ref[:, d:] = jnp.where(lane == 0, wa, jnp.where(lane == 1, wb, 0.0))


def _out_call(lay, mix, xp, xs, mod3, w_out, gn, w_router, tm):
    t, d = lay.t, xp.shape[1]
    row = lambda i: (i, 0)
    const = lambda i: (0, 0)
    meta = 8
    return pl.pallas_call(
        functools.partial(_out_kernel, nt0=lay.t0 // tm),
        grid=(t // tm,),
        in_specs=lay.split_specs(tm, d) + [
            pl.BlockSpec((tm, mix.shape[1]), row),
            pl.BlockSpec((1, 6, d), lambda i: (lay.batch_of_tile(i, tm), 0, 0)),
            pl.BlockSpec((d, d), const),
            pl.BlockSpec((1, d), const),
            pl.BlockSpec((d, LANES), const),
        ],
        out_specs=[pl.BlockSpec((tm, d), row), pl.BlockSpec((tm, d + LANES), row),
                   pl.BlockSpec((tm, meta), row)],
        out_shape=[jax.ShapeDtypeStruct((t, d), F32), jax.ShapeDtypeStruct((t, d + LANES), F32),
                   jax.ShapeDtypeStruct((t, meta), jnp.int32)],
        compiler_params=_params(("arbitrary",)),
        name="out_proj",
    )(xp, xs, mix, mod3, w_out, gn, w_router)


def _bucket_onehot(ids_ref):
    ids = ids_ref[...]
    ea = ids[:, 0:1]
    eb = ids[:, 1:2]
    la = ea % EXPERTS_PER_GROUP
    lb = eb % EXPERTS_PER_GROUP
    pair = la * (2 * EXPERTS_PER_GROUP - 1 - la) // 2 + (lb - la - 1)
    bucket = (ea // EXPERTS_PER_GROUP) * len(_PAIRS) + pair
    lane = lax.broadcasted_iota(jnp.int32, (ids.shape[0], LANES), 1)
    return lane == bucket


def _positions_kernel(ids_ref, pos_ref, counts_ref, run_ref, base_ref, start_ref, *, tm):
    p = pl.program_id(0)
    i = pl.program_id(1)
    tt = ids_ref.shape[0]
    onehot = _bucket_onehot(ids_ref)

    @pl.when(jnp.logical_and(p == 0, i == 0))
    def _():
        run_ref[...] = jnp.zeros(run_ref.shape, F32)

    @pl.when(p == 0)
    def _():
        base_ref[pl.ds(i, 1), :] = run_ref[...]
        run_ref[...] += jnp.sum(jnp.where(onehot, 1.0, 0.0), axis=0, keepdims=True)

    @pl.when(jnp.logical_and(p == 1, i == 0))
    def _():
        counts = run_ref[...]
        counts_ref[...] = counts
        tiles = jnp.floor((counts + (tm - 1)) * (1.0 / tm)).astype(BF16)
        a = lax.broadcasted_iota(jnp.int32, (LANES, LANES), 0)
        b = lax.broadcasted_iota(jnp.int32, (LANES, LANES), 1)
        before = jnp.where(a < b, 1.0, 0.0).astype(BF16)
        start_ref[...] = jnp.dot(tiles, before, preferred_element_type=F32) * tm

    @pl.when(p == 1)
    def _():
        r = lax.broadcasted_iota(jnp.int32, (tt, tt), 0)
        c = lax.broadcasted_iota(jnp.int32, (tt, tt), 1)
        earlier = jnp.where(c < r, 1.0, 0.0).astype(BF16)
        oh = jnp.where(onehot, 1.0, 0.0)
        rank = jnp.dot(earlier, oh.astype(BF16), preferred_element_type=F32)
        val = start_ref[...] + base_ref[pl.ds(i, 1), :] + rank
        pos = jnp.sum(oh * val, axis=1, keepdims=True)
        pos_ref[...] = jnp.broadcast_to(pos, pos_ref.shape).astype(jnp.int32)


def _positions_call(ids, tm, tt):
    t, meta = ids.shape
    nt = t // tt
    assert t // tm + 1 <= 256 and tm & (tm - 1) == 0
    pos, counts = pl.pallas_call(
        functools.partial(_positions_kernel, tm=tm),
        grid=(2, nt),
        in_specs=[pl.BlockSpec((tt, meta), lambda p, i: (i, 0))],
        out_specs=[pl.BlockSpec((tt, meta), lambda p, i: (i * p, 0)),
                   pl.BlockSpec((1, LANES), lambda p, i: (0, 0))],
        out_shape=[jax.ShapeDtypeStruct((t, meta), jnp.int32), jax.ShapeDtypeStruct((1, LANES), F32)],
        scratch_shapes=[pltpu.VMEM((1, LANES), F32), pltpu.VMEM((nt, LANES), F32), pltpu.VMEM((1, LANES), F32)],
        compiler_params=_params(("arbitrary", "arbitrary")),
        name="positions",
    )(ids)
    return pos[:, 0], counts[0, :N_BUCKETS].astype(jnp.int32)


def _route_plan(ids, tm, tt):
    t = ids.shape[0]
    pos, counts = _positions_call(ids, tm, tt)
    tiles = (counts + tm - 1) // tm
    tile_end = jnp.cumsum(tiles)
    tile_start = tile_end - tiles
    assert t % tm == 0
    n_tiles = t // tm + N_BUCKETS
    src = jnp.zeros((n_tiles * tm,), jnp.int32).at[pos].set(jnp.arange(t, dtype=jnp.int32))
    tile_ids = jnp.arange(n_tiles, dtype=jnp.int32)
    used = tile_end[-1]
    tile_bucket = jnp.searchsorted(tile_end, jnp.minimum(tile_ids, used - 1), side="right").astype(jnp.int32)
    pairs = np.array(_PAIRS, np.int32)
    tgrp = tile_bucket // len(_PAIRS)
    tpair = tile_bucket % len(_PAIRS)
    tile_ea = tgrp * EXPERTS_PER_GROUP + jnp.take(jnp.asarray(pairs[:, 0]), tpair)
    tile_eb = tgrp * EXPERTS_PER_GROUP + jnp.take(jnp.asarray(pairs[:, 1]), tpair)
    rows_left = jnp.take(counts, tile_bucket) - (tile_ids - jnp.take(tile_start, tile_bucket)) * tm
    tile_rows = jnp.where(tile_ids < used, jnp.clip(rows_left, 0, tm), 0).astype(jnp.int32)
    return pos.astype(jnp.int32), src, tile_ea.astype(jnp.int32), tile_eb.astype(jnp.int32), tile_rows, used


def _row_copy(src_hbm, row, buf, slot, r, sem):
    return pltpu.make_async_copy(src_hbm.at[pl.ds(row, 1), :], buf.at[slot, pl.ds(r, 1), :], sem.at[slot])


def _gather_start(idx_ref, base, rows, src_hbm, buf, slot, sem):
    for r in range(rows):
        _row_copy(src_hbm, idx_ref[base + r], buf, slot, r, sem).start()


def _gather_wait(rows, src_hbm, buf, slot, sem):
    pltpu.make_async_copy(src_hbm.at[pl.ds(0, rows), :], buf.at[slot], sem.at[slot]).wait()


def _moe_kernel(src_ref, ea_ref, eb_ref, rows_ref, h_hbm, w1a_ref, w3a_ref, w2a_ref, w1b_ref, w3b_ref, w2b_ref,
                y_ref, buf, sem, *, tm):
    i = pl.program_id(0)
    slot = i % 2
    used = rows_ref[i] > 0
    d = y_ref.shape[1]

    @pl.when(i == 0)
    def _():
        _gather_start(src_ref, 0, tm, h_hbm, buf, 0, sem)

    @pl.when(jnp.logical_or(i == 0, rows_ref[jnp.maximum(i - 1, 0)] > 0))
    def _():
        _gather_wait(tm, h_hbm, buf, slot, sem)

    @pl.when(jnp.logical_not(used))
    def _():
        y_ref[...] = jnp.zeros(y_ref.shape, y_ref.dtype)

    @pl.when(used)
    def _():
        _gather_start(src_ref, (i + 1) * tm, tm, h_hbm, buf, 1 - slot, sem)
        h = buf[slot, :, :d].astype(BF16)

        def expert(w1_ref, w3_ref, w2_ref):
            a = (jax.nn.silu(jnp.dot(h, w1_ref[0], preferred_element_type=F32))
                 * jnp.dot(h, w3_ref[0], preferred_element_type=F32))
            return jnp.dot(a.astype(BF16), w2_ref[0], preferred_element_type=F32)

        w = buf[slot, :, d:]
        y_ref[...] = (expert(w1a_ref, w3a_ref, w2a_ref) * w[:, 0:1]
                      + expert(w1b_ref, w3b_ref, w2b_ref) * w[:, 1:2])


def _moe_call(src, tile_ea, tile_eb, tile_rows, hn, w1, w3, w2, tm):
    n_tiles = tile_ea.shape[0]
    d = w1.shape[1]
    f = w1.shape[2]
    wa = lambda i, src, ea, eb, rows: (ea[i], 0, 0)
    wb = lambda i, src, ea, eb, rows: (eb[i], 0, 0)
    grid_spec = pltpu.PrefetchScalarGridSpec(
        num_scalar_prefetch=4,
        grid=(n_tiles,),
        in_specs=[
            pl.BlockSpec(memory_space=pl.ANY),
            pl.BlockSpec((1, d, f), wa), pl.BlockSpec((1, d, f), wa), pl.BlockSpec((1, f, d), wa),
            pl.BlockSpec((1, d, f), wb), pl.BlockSpec((1, d, f), wb), pl.BlockSpec((1, f, d), wb),
        ],
        out_specs=pl.BlockSpec((tm, d), lambda i, *_: (i, 0)),
        scratch_shapes=[pltpu.VMEM((2, tm, hn.shape[1]), F32), pltpu.SemaphoreType.DMA((2,))],
    )
    return pl.pallas_call(
        functools.partial(_moe_kernel, tm=tm),
        grid_spec=grid_spec,
        out_shape=jax.ShapeDtypeStruct((n_tiles * tm, d), F32),
        compiler_params=_params(("arbitrary",)),
        name="moe",
    )(src, tile_ea, tile_eb, tile_rows, hn, w1, w3, w2, w1, w3, w2)


def _combine_kernel(pos_ref, y_hbm, x1_ref, mod_ref, op_ref, os_ref, buf, sem, *, tm, nt0):
    i = pl.program_id(0)
    nt = pl.num_programs(0)
    slot = i % 2

    @pl.when(i == 0)
    def _():
        _gather_start(pos_ref, 0, tm, y_hbm, buf, 0, sem)

    _gather_wait(tm, y_hbm, buf, slot, sem)

    @pl.when(i + 1 < nt)
    def _():
        _gather_start(pos_ref, (i + 1) * tm, tm, y_hbm, buf, 1 - slot, sem)

    out = x1_ref[...] + mod_ref[0, 5:6, :] * buf[slot]

    @pl.when(i < nt0)
    def _():
        op_ref[...] = out

    @pl.when(i >= nt0)
    def _():
        os_ref[...] = out


def _combine_call(lay, pos, y_sorted, x1, mod3, tm):
    t, d = x1.shape
    grid_spec = pltpu.PrefetchScalarGridSpec(
        num_scalar_prefetch=1,
        grid=(t // tm,),
        in_specs=[
            pl.BlockSpec(memory_space=pl.ANY),
            pl.BlockSpec((tm, d), lambda i, pos: (i, 0)),
            pl.BlockSpec((1, 6, d), lambda i, pos: (lay.batch_of_tile(i, tm), 0, 0)),
        ],
        out_specs=lay.split_specs(tm, d),
        scratch_shapes=[pltpu.VMEM((2, tm, d), F32), pltpu.SemaphoreType.DMA((2,))],
    )
    return pl.pallas_call(
        functools.partial(_combine_kernel, tm=tm, nt0=lay.t0 // tm),
        grid_spec=grid_spec,
        out_shape=lay.split_shapes(d, F32),
        compiler_params=_params(("arbitrary",)),
        name="combine",
    )(pos, y_sorted, x1, mod3)


def _rope_tables(n):
    n_rows = n // GRID_W
    rows = jnp.repeat(jnp.arange(n_rows), GRID_W).astype(F32)
    cols = jnp.tile(jnp.arange(GRID_W), n_rows).astype(F32)
    half = HEAD_DIM // 2
    inv = ROPE_THETA ** (-jnp.arange(0, half, 2, dtype=F32) / half)
    ang = jnp.concatenate([rows[:, None] * inv, cols[:, None] * inv], axis=-1)
    sign = jnp.tile(jnp.array([-1.0, 1.0], F32), half)
    return jnp.repeat(jnp.cos(ang), 2, axis=-1), jnp.repeat(jnp.sin(ang), 2, axis=-1) * sign


def _lambda_init(layer_idx):
    return 0.8 - 0.6 * math.exp(-0.3 * layer_idx)


def _tiled_gain(g):
    return jnp.tile(g, LANES // g.shape[-1]).reshape(1, LANES).astype(F32)


def kernel(x_prompt, x_sample, c_prompt, c_sample, rel_bias, w_ada, b_ada, g_norm_mix, w_in, g_qa, g_ka,
           g_qb, g_kb, lam_qk, g_subln, w_fourier, w_out, g_norm_ffn, w_group, w_expert, w1, w3, w2):
    b0, n0, d = x_prompt.shape
    b1, n1, _ = x_sample.shape
    depth = w_in.shape[0]
    lay = _Layout(b0, n0, b1, n1)
    trunks = ((0, b0, n0), (lay.t0, b1, n1))
    n_max = max(n0, n1)

    tm = _pick(math.gcd(n0, n1), 512)
    tq_a = _pick(math.gcd(n0, n1), 256)
    tk_a = _pick(math.gcd(n0, n1), 2048)
    t_b = _pick(math.gcd(n0, n1), 512)
    tf = _pick(math.gcd(n0, n1), 128)
    tm_moe = 256

    xp = x_prompt.reshape(b0 * n0, d)
    xs = x_sample.reshape(b1 * n1, d)
    nb = b0 + b1
    bp = -(-nb // 8) * 8
    c_all = jnp.zeros((bp, d), F32).at[:nb].set(jnp.concatenate([c_prompt, c_sample], axis=0))
    mod = _ada_call(c_all, w_ada, b_ada).reshape(depth, bp, 6, d)

    cos_e, sin_e = _rope_tables(n_max)
    chunk_ratio = {n: min(2, n // t_b) for n in (n0, n1)}
    bands = {c: _bias_tables(rel_bias, t_b, c) for c in sorted(set(chunk_ratio.values()))}
    dft = {n: _dft_tables(n, _pick(n, 128)) for n in sorted({n0, n1})}
    cidx = jnp.arange(C_GROUP_W, dtype=jnp.int32)
    ang_c = ((cidx[:, None] * cidx[None, :]) % C_GROUP_W).astype(F32) * (2.0 * math.pi / C_GROUP_W)
    dft_c = (jnp.concatenate([jnp.cos(ang_c), jnp.sin(ang_c)], axis=1) * C_GROUP_W ** -0.5).astype(BF16)

    for l in range(depth):
        mod3 = mod[l]
        lam_init = _lambda_init(l)
        qa, ka, va, qb, kb, vb, p, q = _proj_call(
            lay, xp, xs, mod3, g_norm_mix[l].reshape(1, d), w_in[l].astype(BF16),
            _tiled_gain(g_qa[l]), _tiled_gain(g_ka[l]), _tiled_gain(g_qb[l]), _tiled_gain(g_kb[l]),
            cos_e, sin_e, dft_c, tm)

        wf = w_fourier[l].astype(BF16)
        mix = jnp.zeros((lay.t, A_Q + B_WIDTH + C_WIDTH), BF16)
        for tok0, b, n in trunks:
            mix = _attn_a_call(qa, ka, va, mix, tok0, b, n, tq_a, min(tk_a, max(n // 2, LANES)))
            mix = _attn_b_call(qb, kb, vb, bands[chunk_ratio[n]], lam_qk[l], _tiled_gain(g_subln[l]), mix,
                               tok0, b, n, t_b, lam_init)
            mix = _fourier_call(dft[n][0], dft[n][1], p, q, wf, mix, tok0, b, n, tf)

        w_router = jnp.zeros((d, LANES), F32).at[:, :N_GROUPS].set(w_group[l])
        w_router = w_router.at[:, N_GROUPS:N_GROUPS + N_EXPERTS].set(w_expert[l]).astype(BF16)
        x1, hn2, ids = _out_call(lay, mix, xp, xs, mod3, w_out[l].astype(BF16),
                                 g_norm_ffn[l].reshape(1, d), w_router, tm)

        pos, src, tile_ea, tile_eb, tile_rows, _ = _route_plan(ids, tm_moe, tm)
        y_sorted = _moe_call(src, tile_ea, tile_eb, tile_rows, hn2, w1[l].astype(BF16), w3[l].astype(BF16),
                             w2[l].astype(BF16), tm_moe)
        xp, xs = _combine_call(lay, pos, y_sorted, x1, mod3, tm)

    return (xp.reshape(b0, n0, d), xs.reshape(b1, n1, d))
```

```python
import functools
import math

import jax
import jax.numpy as jnp
import numpy as np
from jax import lax
from jax.experimental import pallas as pl
from jax.experimental.pallas import tpu as pltpu

F32 = jnp.float32
BF16 = jnp.bfloat16

D_MODEL = 2048
HEAD_DIM = 128
A_HEADS = 8
A_KV_HEADS = 2
A_GROUP = A_HEADS // A_KV_HEADS
A_Q = A_HEADS * HEAD_DIM
A_KV = A_KV_HEADS * HEAD_DIM
B_HEADS = 4
B_HALF = HEAD_DIM // 2
B_WIDTH = B_HEADS * HEAD_DIM
C_WIDTH = 512
C_GROUPS = 4
C_GROUP_W = 128
IN_WIDTH = A_Q + 2 * A_KV + 3 * B_WIDTH + C_WIDTH
GRID_W = 64
ROPE_THETA = 10000.0
NUM_BUCKETS = 32
MAX_DISTANCE = 128
N_GROUPS = 4
EXPERTS_PER_GROUP = 4
N_EXPERTS = 16
D_FF_EXPERT = 512
EPS = 1e-6
LOG2E = 1.4426950408889634

_PAIRS = ((0, 1), (0, 2), (0, 3), (1, 2), (1, 3), (2, 3))
N_BUCKETS = N_GROUPS * len(_PAIRS)

V7X_VMEM_BYTES = 64 * 1024 * 1024
VMEM_LIMIT = V7X_VMEM_BYTES - 8 * 1024 * 1024
LANES = 128
NEG_BIG = -1e30


def _params(sem, **kw):
    return pltpu.CompilerParams(dimension_semantics=sem, vmem_limit_bytes=VMEM_LIMIT, **kw)

def _pick(total, pref):
    t = min(pref, total)
    while total % t:
        t //= 2
    return t


def _ada_kernel(c_ref, w_ref, b_ref, o_ref):
    h = jax.nn.silu(c_ref[...]).astype(BF16)
    o_ref[0] = jnp.dot(h, w_ref[0].astype(BF16), preferred_element_type=F32) + b_ref[0]


def _ada_call(c_all, w_ada, b_ada):
    depth, d, e = w_ada.shape
    bp = c_all.shape[0]
    tn = _pick(e, 1024)
    return pl.pallas_call(
        _ada_kernel,
        grid=(depth, e // tn),
        in_specs=[
            pl.BlockSpec((bp, d), lambda l, j: (0, 0)),
            pl.BlockSpec((1, d, tn), lambda l, j: (l, 0, j)),
            pl.BlockSpec((1, 1, tn), lambda l, j: (l, 0, j)),
        ],
        out_specs=pl.BlockSpec((1, bp, tn), lambda l, j: (l, 0, j)),
        out_shape=jax.ShapeDtypeStruct((depth, bp, e), F32),
        compiler_params=_params(("arbitrary", "arbitrary")),
        name="ada",
    )(c_all, w_ada, b_ada.reshape(depth, 1, e))


class _Layout:
    def __init__(self, b0, n0, b1, n1):
        self.b = (b0, b1)
        self.n = (n0, n1)
        self.t0 = b0 * n0
        self.t = b0 * n0 + b1 * n1

    def batch_of_tile(self, i, tm):
        tok = i * tm
        return jnp.where(tok < self.t0, tok // self.n[0], self.b[0] + (tok - self.t0) // self.n[1])

    def pos_block_of_tile(self, i, tm):
        tok = i * tm
        pos = jnp.where(tok < self.t0, tok % self.n[0], (tok - self.t0) % self.n[1])
        return pos // tm

    def split_specs(self, tm, d):
        nt0 = self.t0 // tm
        return [pl.BlockSpec((tm, d), lambda i, *_: (jnp.minimum(i, nt0 - 1), 0)),
                pl.BlockSpec((tm, d), lambda i, *_: (jnp.maximum(i - nt0, 0), 0))]

    def split_shapes(self, d, dtype):
        return [jax.ShapeDtypeStruct((self.t0, d), dtype), jax.ShapeDtypeStruct((self.t - self.t0, d), dtype)]


def _split_apply(xp_ref, xs_ref, nt0, body, *rest):
    i = pl.program_id(0)

    @pl.when(i < nt0)
    def _():
        body(xp_ref, *rest)

    @pl.when(i >= nt0)
    def _():
        body(xs_ref, *rest)


def _head_norm(z, g):
    return z * lax.rsqrt(jnp.mean(z * z, axis=-1, keepdims=True) + EPS) * g


def _half_norm(z, g, lo):
    zz = z * z
    s_lo = jnp.sum(jnp.where(lo, zz, 0.0), axis=-1, keepdims=True)
    s_hi = jnp.sum(jnp.where(lo, 0.0, zz), axis=-1, keepdims=True)
    inv = jnp.where(lo, lax.rsqrt(s_lo / B_HALF + EPS), lax.rsqrt(s_hi / B_HALF + EPS))
    return z * inv * g


def _rope(z, c, s_signed, even):
    partner = jnp.where(even, pltpu.roll(z, LANES - 1, 1), pltpu.roll(z, 1, 1))
    return z * c + partner * s_signed


def _proj_kernel(xp_ref, xs_ref, *rest, nt0):
    _split_apply(xp_ref, xs_ref, nt0, _proj_body, *rest)


def _proj_body(x_ref, mod_ref, gn_ref, w_ref, gqa_ref, gka_ref, gqb_ref, gkb_ref, cos_ref, sin_ref,
               dft_ref, qa_ref, ka_ref, va_ref, qb_ref, kb_ref, vb_ref, p_ref, q_ref):
    x = x_ref[...]
    y = x * lax.rsqrt(jnp.mean(x * x, axis=-1, keepdims=True) + EPS) * gn_ref[...]
    hn = (y * (1.0 + mod_ref[0, 1:2, :]) + mod_ref[0, 0:1, :]).astype(BF16)

    def seg(a, b):
        return jnp.dot(hn, w_ref[:, a:b], preferred_element_type=F32)

    tm = x.shape[0]
    lane = lax.broadcasted_iota(jnp.int32, (tm, LANES), 1)
    even = (lane % 2) == 0
    lo = lane < B_HALF
    cos = cos_ref[...]
    sin = sin_ref[...]

    scale_a = HEAD_DIM ** -0.5 * LOG2E
    z = seg(0, A_Q)
    for h in range(A_HEADS):
        sl = slice(h * HEAD_DIM, (h + 1) * HEAD_DIM)
        qa_ref[:, sl] = (_rope(_head_norm(z[:, sl], gqa_ref[...]), cos, sin, even) * scale_a).astype(BF16)
    off = A_Q
    z = seg(off, off + A_KV)
    for h in range(A_KV_HEADS):
        sl = slice(h * HEAD_DIM, (h + 1) * HEAD_DIM)
        ka_ref[:, sl] = _rope(_head_norm(z[:, sl], gka_ref[...]), cos, sin, even).astype(BF16)
    off += A_KV
    va_ref[...] = seg(off, off + A_KV).astype(BF16)
    off += A_KV

    scale_b = B_HALF ** -0.5 * LOG2E
    z = seg(off, off + B_WIDTH)
    for h in range(B_HEADS):
        sl = slice(h * HEAD_DIM, (h + 1) * HEAD_DIM)
        qb_ref[:, sl] = (_half_norm(z[:, sl], gqb_ref[...], lo) * scale_b).astype(BF16)
    off += B_WIDTH
    z = seg(off, off + B_WIDTH)
    for h in range(B_HEADS):
        sl = slice(h * HEAD_DIM, (h + 1) * HEAD_DIM)
        kb_ref[:, sl] = _half_norm(z[:, sl], gkb_ref[...], lo).astype(BF16)
    off += B_WIDTH
    vb_ref[...] = seg(off, off + B_WIDTH).astype(BF16)
    off += B_WIDTH

    z = seg(off, off + C_WIDTH).astype(BF16)
    for g in range(C_GROUPS):
        sl = slice(g * C_GROUP_W, (g + 1) * C_GROUP_W)
        pq = jnp.dot(z[:, sl], dft_ref[...], preferred_element_type=F32)
        p_ref[:, sl] = pq[:, :C_GROUP_W].astype(BF16)
        q_ref[:, sl] = pq[:, C_GROUP_W:].astype(BF16)


def _proj_call(lay, xp, xs, mod3, gn, w_in, gqa, gka, gqb, gkb, cos_e, sin_e, dft_c, tm):
    t, d = lay.t, xp.shape[1]
    row = lambda i: (i, 0)
    const = lambda i: (0, 0)
    widths = (A_Q, A_KV, A_KV, B_WIDTH, B_WIDTH, B_WIDTH, C_WIDTH, C_WIDTH)
    return pl.pallas_call(
        functools.partial(_proj_kernel, nt0=lay.t0 // tm),
        grid=(t // tm,),
        in_specs=lay.split_specs(tm, d) + [
            pl.BlockSpec((1, 6, d), lambda i: (lay.batch_of_tile(i, tm), 0, 0)),
            pl.BlockSpec((1, d), const),
            pl.BlockSpec((d, IN_WIDTH), const),
            pl.BlockSpec((1, LANES), const),
            pl.BlockSpec((1, LANES), const),
            pl.BlockSpec((1, LANES), const),
            pl.BlockSpec((1, LANES), const),
            pl.BlockSpec((tm, LANES), lambda i: (lay.pos_block_of_tile(i, tm), 0)),
            pl.BlockSpec((tm, LANES), lambda i: (lay.pos_block_of_tile(i, tm), 0)),
            pl.BlockSpec((C_GROUP_W, 2 * C_GROUP_W), const),
        ],
        out_specs=[pl.BlockSpec((tm, w), row) for w in widths],
        out_shape=[jax.ShapeDtypeStruct((t, w), BF16) for w in widths],
        compiler_params=_params(("arbitrary",)),
        name="proj",
    )(xp, xs, mod3, gn, w_in, gqa, gka, gqb, gkb, cos_e, sin_e, dft_c)


def _softmax_step(s, vc, m_ref, l_ref, acc_ref):
    m_prev = m_ref[...]
    m_new = jnp.maximum(m_prev, jnp.max(s, axis=1, keepdims=True))
    alpha = jnp.exp2(m_prev - m_new)
    p = jnp.exp2(s - jnp.tile(m_new, (1, s.shape[1] // LANES)))
    l_ref[...] = alpha * l_ref[...] + jnp.sum(p, axis=1, keepdims=True)
    acc_ref[...] = alpha * acc_ref[...] + jnp.dot(p.astype(BF16), vc, preferred_element_type=F32)
    m_ref[...] = m_new


_NT = (((1,), (1,)), ((), ()))


def _attn_a_kernel(q_ref, k_ref, v_ref, _merged_ref, o_ref, qs_ref, m_ref, l_ref, acc_ref, *, tq, tk, n):
    for g in range(A_GROUP):
        qs_ref[g * tq:(g + 1) * tq, :] = q_ref[:, g * HEAD_DIM:(g + 1) * HEAD_DIM]
    m_ref[...] = jnp.full(m_ref.shape, NEG_BIG, F32)
    l_ref[...] = jnp.zeros(l_ref.shape, F32)
    acc_ref[...] = jnp.zeros(acc_ref.shape, F32)

    def body(j, carry):
        rows = pl.ds(pl.multiple_of(j * tk, tk), tk)
        s = lax.dot_general(qs_ref[...], k_ref[rows, :], _NT, preferred_element_type=F32)
        _softmax_step(s, v_ref[rows, :], m_ref, l_ref, acc_ref)
        return carry

    lax.fori_loop(0, n // tk, body, 0, unroll=True)
    o = acc_ref[...] / l_ref[...]
    for g in range(A_GROUP):
        o_ref[:, g * HEAD_DIM:(g + 1) * HEAD_DIM] = o[g * tq:(g + 1) * tq].astype(o_ref.dtype)


def _attn_a_call(qa, ka, va, merged, tok0, b, n, tq, tk):
    qblk0 = tok0 // tq
    kblk0 = tok0 // n
    nq = n // tq
    rows = A_GROUP * tq
    return pl.pallas_call(
        functools.partial(_attn_a_kernel, tq=tq, tk=tk, n=n),
        grid=(b, A_KV_HEADS, nq),
        in_specs=[
            pl.BlockSpec((tq, A_GROUP * HEAD_DIM), lambda bi, kv, qi: (qblk0 + bi * nq + qi, kv)),
            pl.BlockSpec((n, HEAD_DIM), lambda bi, kv, qi: (kblk0 + bi, kv)),
            pl.BlockSpec((n, HEAD_DIM), lambda bi, kv, qi: (kblk0 + bi, kv)),
            pl.BlockSpec(memory_space=pl.ANY),
        ],
        out_specs=pl.BlockSpec((tq, A_GROUP * HEAD_DIM), lambda bi, kv, qi: (qblk0 + bi * nq + qi, kv)),
        out_shape=jax.ShapeDtypeStruct(merged.shape, merged.dtype),
        input_output_aliases={3: 0},
        scratch_shapes=[
            pltpu.VMEM((rows, HEAD_DIM), BF16),
            pltpu.VMEM((rows, LANES), F32),
            pltpu.VMEM((rows, LANES), F32),
            pltpu.VMEM((rows, HEAD_DIM), F32),
        ],
        compiler_params=_params(("arbitrary", "arbitrary", "arbitrary")),
        name="attn_a",
    )(qa, ka, va, merged)


def _attn_b_kernel(q_ref, k_ref, v_ref, band_ref, lam_ref, gs_ref, _merged_ref, o_ref,
                   qs_ref, m_ref, l_ref, acc_ref, *, t, ck, nchunks, lam_init):
    i = pl.program_id(2)
    c = ck // t
    q = q_ref[...]
    lo = lax.broadcasted_iota(jnp.int32, q.shape, 1) < B_HALF
    zero = jnp.zeros_like(q)
    qs_ref[0:t, :] = jnp.where(lo, q, zero)
    qs_ref[t:2 * t, :] = jnp.where(lo, zero, q)
    m_ref[...] = jnp.full(m_ref.shape, NEG_BIG, F32)
    l_ref[...] = jnp.zeros(l_ref.shape, F32)
    acc_ref[...] = jnp.zeros(acc_ref.shape, F32)

    def body(j, carry):
        rows = pl.ds(pl.multiple_of(j * ck, ck), ck)
        s = lax.dot_general(qs_ref[...], k_ref[rows, :], _NT, preferred_element_type=F32)
        lo_off, hi_off = _band_offsets(c)
        bias = band_ref[0, jnp.clip(c * j - i, lo_off, hi_off) - lo_off]
        s = (s.reshape(2, t, ck) + bias[None]).reshape(2 * t, ck)
        _softmax_step(s, v_ref[rows, :], m_ref, l_ref, acc_ref)
        return carry

    lax.fori_loop(0, nchunks, body, 0, unroll=True)
    o = acc_ref[...] / l_ref[...]
    lq = lam_ref[...]
    lam = (jnp.exp(jnp.sum(lq[0:1] * lq[1:2], axis=-1, keepdims=True))
           - jnp.exp(jnp.sum(lq[2:3] * lq[3:4], axis=-1, keepdims=True)) + lam_init)
    ob = o[0:t] - lam * o[t:2 * t]
    o_ref[...] = (_head_norm(ob, gs_ref[...]) * (1.0 - lam_init)).astype(o_ref.dtype)


def _attn_b_call(qb, kb, vb, band, lam_qk, g_subln, merged, tok0, b, n, t, lam_init):
    qblk0 = tok0 // t
    kblk0 = tok0 // n
    nq = n // t
    ck = band.shape[3]
    return pl.pallas_call(
        functools.partial(_attn_b_kernel, t=t, ck=ck, nchunks=n // ck, lam_init=lam_init),
        grid=(B_HEADS, b, nq),
        in_specs=[
            pl.BlockSpec((t, HEAD_DIM), lambda h, bi, qi: (qblk0 + bi * nq + qi, h)),
            pl.BlockSpec((n, HEAD_DIM), lambda h, bi, qi: (kblk0 + bi, h)),
            pl.BlockSpec((n, HEAD_DIM), lambda h, bi, qi: (kblk0 + bi, h)),
            pl.BlockSpec((1,) + band.shape[1:], lambda h, bi, qi: (h, 0, 0, 0), pipeline_mode=pl.Buffered(1)),
            pl.BlockSpec((4, B_HALF), lambda h, bi, qi: (0, 0)),
            pl.BlockSpec((1, LANES), lambda h, bi, qi: (0, 0)),
            pl.BlockSpec(memory_space=pl.ANY),
        ],
        out_specs=pl.BlockSpec((t, HEAD_DIM), lambda h, bi, qi: (qblk0 + bi * nq + qi, A_Q // HEAD_DIM + h)),
        out_shape=jax.ShapeDtypeStruct(merged.shape, merged.dtype),
        input_output_aliases={6: 0},
        scratch_shapes=[
            pltpu.VMEM((2 * t, HEAD_DIM), BF16),
            pltpu.VMEM((2 * t, LANES), F32),
            pltpu.VMEM((2 * t, LANES), F32),
            pltpu.VMEM((2 * t, HEAD_DIM), F32),
        ],
        compiler_params=_params(("arbitrary", "arbitrary", "arbitrary")),
        name="attn_b",
    )(qb, kb, vb, band, lam_qk, g_subln, merged)


def _t5_bucket(rel):
    nb = NUM_BUCKETS // 2
    ret = (rel > 0).astype(jnp.int32) * nb
    n = jnp.abs(rel)
    max_exact = nb // 2
    nf = jnp.maximum(n, 1).astype(F32)
    large = max_exact + (jnp.log(nf / max_exact) / math.log(MAX_DISTANCE / max_exact)
                         * (nb - max_exact)).astype(jnp.int32)
    large = jnp.minimum(large, nb - 1)
    return ret + jnp.where(n < max_exact, n, large)


_T5_SATURATION = math.ceil((NUM_BUCKETS // 4) * (MAX_DISTANCE / (NUM_BUCKETS // 4)) ** (
    (NUM_BUCKETS // 2 - 1 - NUM_BUCKETS // 4) / (NUM_BUCKETS // 2 - NUM_BUCKETS // 4)))


def _band_offsets(c):
    return -(c + 1), 2


def _band_kernel(vec_ref, o_ref, *, t, ck):
    width = vec_ref.shape[2]
    x = jnp.broadcast_to(vec_ref[0], (t, width))
    o_ref[0, 0] = pltpu.roll(x, width - t + 1, 1, stride=1, stride_axis=0)[:, :ck]


def _bias_tables(rel_bias, t, c):
    assert t + 1 >= _T5_SATURATION
    heads = rel_bias.shape[1]
    lo_off, hi_off = _band_offsets(c)
    nd = hi_off - lo_off + 1
    ck = c * t
    width = (c + 2) * t
    rel = ((jnp.arange(nd, dtype=jnp.int32)[:, None] + lo_off) * t
           + jnp.arange(-(t - 1), width - (t - 1), dtype=jnp.int32)[None, :])
    vec = jnp.take(rel_bias, _t5_bucket(rel), axis=0).astype(F32)
    vec = (vec.transpose(2, 0, 1) * LOG2E).reshape(heads * nd, 1, width)
    return pl.pallas_call(
        functools.partial(_band_kernel, t=t, ck=ck),
        grid=(heads, nd),
        in_specs=[pl.BlockSpec((1, 1, width), lambda h, d: (h * nd + d, 0, 0))],
        out_specs=pl.BlockSpec((1, 1, t, ck), lambda h, d: (h, d, 0, 0)),
        out_shape=jax.ShapeDtypeStruct((heads, nd, t, ck), F32),
        compiler_params=_params(("arbitrary", "arbitrary")),
        name="band",
    )(vec)


def _dft_gen_kernel(ck_ref, sk_ref, ca_ref, sa_ref, c_ref, s_ref):
    ck = ck_ref[0]
    sk = sk_ref[0]
    ca = ca_ref[...]
    sa = sa_ref[...]
    c_ref[...] = (ck * ca - sk * sa).astype(c_ref.dtype)
    s_ref[...] = (sk * ca + ck * sa).astype(s_ref.dtype)


def _dft_tables(n, tk):
    t = jnp.arange(n, dtype=jnp.int32)[None, :]
    k0 = (jnp.arange(n // tk, dtype=jnp.int32) * tk)[:, None]
    a = jnp.arange(tk, dtype=jnp.int32)[:, None]
    ang_k = ((k0 * t) % n).astype(F32) * (2.0 * math.pi / n)
    ang_a = ((a * t) % n).astype(F32) * (2.0 * math.pi / n)
    scale = n ** -0.5
    ck = (jnp.cos(ang_k) * scale).reshape(n // tk, 1, n)
    sk = (jnp.sin(ang_k) * scale).reshape(n // tk, 1, n)
    return pl.pallas_call(
        _dft_gen_kernel,
        grid=(n // tk,),
        in_specs=[
            pl.BlockSpec((1, 1, n), lambda i: (i, 0, 0)),
            pl.BlockSpec((1, 1, n), lambda i: (i, 0, 0)),
            pl.BlockSpec((tk, n), lambda i: (0, 0)),
            pl.BlockSpec((tk, n), lambda i: (0, 0)),
        ],
        out_specs=[pl.BlockSpec((tk, n), lambda i: (i, 0)), pl.BlockSpec((tk, n), lambda i: (i, 0))],
        out_shape=[jax.ShapeDtypeStruct((n, n), BF16), jax.ShapeDtypeStruct((n, n), BF16)],
        compiler_params=_params(("arbitrary",)),
        name="dft_gen",
    )(ck, sk, jnp.cos(ang_a), jnp.sin(ang_a))


def _fourier_kernel(c_ref, s_ref, p_ref, q_ref, w_ref, _merged_ref, o_ref):
    f = (jnp.dot(c_ref[...], p_ref[...], preferred_element_type=F32)
         - jnp.dot(s_ref[...], q_ref[...], preferred_element_type=F32)).astype(BF16)
    for g in range(C_GROUPS):
        sl = slice(g * C_GROUP_W, (g + 1) * C_GROUP_W)
        o_ref[:, sl] = jnp.dot(f[:, sl], w_ref[g], preferred_element_type=F32).astype(o_ref.dtype)


def _fourier_call(cmat, smat, p, q, wf, merged, tok0, b, n, tf):
    kblk0 = tok0 // n
    oblk0 = tok0 // tf
    nk = n // tf
    return pl.pallas_call(
        _fourier_kernel,
        grid=(b, nk),
        in_specs=[
            pl.BlockSpec((tf, n), lambda bi, kt: (kt, 0)),
            pl.BlockSpec((tf, n), lambda bi, kt: (kt, 0)),
            pl.BlockSpec((n, C_WIDTH), lambda bi, kt: (kblk0 + bi, 0)),
            pl.BlockSpec((n, C_WIDTH), lambda bi, kt: (kblk0 + bi, 0)),
            pl.BlockSpec((C_GROUPS, C_GROUP_W, C_GROUP_W), lambda bi, kt: (0, 0, 0)),
            pl.BlockSpec(memory_space=pl.ANY),
        ],
        out_specs=pl.BlockSpec((tf, C_WIDTH), lambda bi, kt: (oblk0 + bi * nk + kt, (A_Q + B_WIDTH) // C_WIDTH)),
        out_shape=jax.ShapeDtypeStruct(merged.shape, merged.dtype),
        input_output_aliases={5: 0},
        compiler_params=_params(("arbitrary", "arbitrary")),
        name="fourier",
    )(cmat, smat, p, q, wf, merged)


def _out_kernel(xp_ref, xs_ref, *rest, nt0):
    _split_apply(xp_ref, xs_ref, nt0, _out_body, *rest)


def _out_body(x_ref, mix_ref, mod_ref, w_ref, gn_ref, wr_ref, x1_ref, hn_ref, ids_ref):
    mix = jnp.dot(mix_ref[...], w_ref[...], preferred_element_type=F32)
    x1 = x_ref[...] + mod_ref[0, 2:3, :] * mix
    x1_ref[...] = x1
    y = x1 * lax.rsqrt(jnp.mean(x1 * x1, axis=-1, keepdims=True) + EPS) * gn_ref[...]
    hn = y * (1.0 + mod_ref[0, 4:5, :]) + mod_ref[0, 3:4, :]
    d = hn.shape[1]
    hn_ref[:, :d] = hn

    logits = jnp.dot(hn.astype(BF16), wr_ref[...], preferred_element_type=F32)
    lane = lax.broadcasted_iota(jnp.int32, logits.shape, 1)
    big = jnp.int32(LANES)
    is_g = lane < N_GROUPS
    gmax = jnp.max(jnp.where(is_g, logits, -jnp.inf), axis=-1, keepdims=True)
    g_sel = jnp.min(jnp.where(jnp.logical_and(is_g, logits == gmax), lane, big), axis=-1, keepdims=True)
    g_w = 1.0 / jnp.sum(jnp.where(is_g, jnp.exp(logits - gmax), 0.0), axis=-1, keepdims=True)
    lo_lane = N_GROUPS + g_sel * EXPERTS_PER_GROUP
    in_g = jnp.logical_and(lane >= lo_lane, lane < lo_lane + EXPERTS_PER_GROUP)
    v0 = jnp.max(jnp.where(in_g, logits, -jnp.inf), axis=-1, keepdims=True)
    i0 = jnp.min(jnp.where(jnp.logical_and(in_g, logits == v0), lane, big), axis=-1, keepdims=True)
    rest = jnp.logical_and(in_g, lane != i0)
    v1 = jnp.max(jnp.where(rest, logits, -jnp.inf), axis=-1, keepdims=True)
    i1 = jnp.min(jnp.where(jnp.logical_and(rest, logits == v1), lane, big), axis=-1, keepdims=True)
    e1 = jnp.exp(v1 - v0)
    w0 = g_w / (1.0 + e1)
    w1 = g_w * e1 / (1.0 + e1)
    swap = i1 < i0
    ea = jnp.where(swap, i1, i0) - N_GROUPS
    eb = jnp.where(swap, i0, i1) - N_GROUPS
    wa = jnp.where(swap, w1, w0)
    wb = jnp.where(swap, w0, w1)
    ml = lax.broadcasted_iota(jnp.int32, ids_ref.shape, 1)
    ids_ref[...] = jnp.where(ml == 0, ea, jnp.where(ml == 1, eb, 0))
    hn_ref[:, d:] = jnp.where(lane == 0, wa, jnp.where(lane == 1, wb, 0.0))


def _out_call(lay, mix, xp, xs, mod3, w_out, gn, w_router, tm):
    t, d = lay.t, xp.shape[1]
    row = lambda i: (i, 0)
    const = lambda i: (0, 0)
    meta = 8
    return pl.pallas_call(
        functools.partial(_out_kernel, nt0=lay.t0 // tm),
        grid=(t // tm,),
        in_specs=lay.split_specs(tm, d) + [
            pl.BlockSpec((tm, mix.shape[1]), row),
            pl.BlockSpec((1, 6, d), lambda i: (lay.batch_of_tile(i, tm), 0, 0)),
            pl.BlockSpec((d, d), const),
            pl.BlockSpec((1, d), const),
            pl.BlockSpec((d, LANES), const),
        ],
        out_specs=[pl.BlockSpec((tm, d), row), pl.BlockSpec((tm, d + LANES), row),
                   pl.BlockSpec((tm, meta), row)],
        out_shape=[jax.ShapeDtypeStruct((t, d), F32), jax.ShapeDtypeStruct((t, d + LANES), F32),
                   jax.ShapeDtypeStruct((t, meta), jnp.int32)],
        compiler_params=_params(("arbitrary",)),
        name="out_proj",
    )(xp, xs, mix, mod3, w_out, gn, w_router)


def _bucket_onehot(ids_ref):
    ids = ids_ref[...]
    ea = ids[:, 0:1]
    eb = ids[:, 1:2]
    la = ea % EXPERTS_PER_GROUP
    lb = eb % EXPERTS_PER_GROUP
    pair = la * (2 * EXPERTS_PER_GROUP - 1 - la) // 2 + (lb - la - 1)
    bucket = (ea // EXPERTS_PER_GROUP) * len(_PAIRS) + pair
    lane = lax.broadcasted_iota(jnp.int32, (ids.shape[0], LANES), 1)
    return lane == bucket


def _positions_kernel(ids_ref, pos_ref, counts_ref, run_ref, base_ref, start_ref, earlier_ref, *, tm):
    p = pl.program_id(0)
    i = pl.program_id(1)
    tt = ids_ref.shape[0]
    onehot = _bucket_onehot(ids_ref)

    @pl.when(jnp.logical_and(p == 0, i == 0))
    def _():
        run_ref[...] = jnp.zeros(run_ref.shape, F32)

    @pl.when(p == 0)
    def _():
        base_ref[pl.ds(i, 1), :] = run_ref[...]
        run_ref[...] += jnp.sum(jnp.where(onehot, 1.0, 0.0), axis=0, keepdims=True)

    @pl.when(jnp.logical_and(p == 1, i == 0))
    def _():
        counts = run_ref[...]
        counts_ref[...] = counts
        tiles = jnp.floor((counts + (tm - 1)) * (1.0 / tm)).astype(BF16)
        a = lax.broadcasted_iota(jnp.int32, (LANES, LANES), 0)
        b = lax.broadcasted_iota(jnp.int32, (LANES, LANES), 1)
        before = jnp.where(a < b, 1.0, 0.0).astype(BF16)
        start_ref[...] = jnp.dot(tiles, before, preferred_element_type=F32) * tm
        r = lax.broadcasted_iota(jnp.int32, (tt, tt), 0)
        c = lax.broadcasted_iota(jnp.int32, (tt, tt), 1)
        earlier_ref[...] = jnp.where(c < r, 1.0, 0.0).astype(BF16)

    @pl.when(p == 1)
    def _():
        oh = jnp.where(onehot, 1.0, 0.0)
        rank = jnp.dot(earlier_ref[...], oh.astype(BF16), preferred_element_type=F32)
        val = start_ref[...] + base_ref[pl.ds(i, 1), :] + rank
        pos = jnp.sum(oh * val, axis=1, keepdims=True)
        pos_ref[...] = jnp.broadcast_to(pos, pos_ref.shape).astype(jnp.int32)


def _positions_call(ids, tm, tt):
    t, meta = ids.shape
    nt = t // tt
    assert t // tm + 1 <= 256 and tm & (tm - 1) == 0
    pos, counts = pl.pallas_call(
        functools.partial(_positions_kernel, tm=tm),
        grid=(2, nt),
        in_specs=[pl.BlockSpec((tt, meta), lambda p, i: (i, 0))],
        out_specs=[pl.BlockSpec((tt, meta), lambda p, i: (i * p, 0)),
                   pl.BlockSpec((1, LANES), lambda p, i: (0, 0))],
        out_shape=[jax.ShapeDtypeStruct((t, meta), jnp.int32), jax.ShapeDtypeStruct((1, LANES), F32)],
        scratch_shapes=[pltpu.VMEM((1, LANES), F32), pltpu.VMEM((nt, LANES), F32), pltpu.VMEM((1, LANES), F32),
                        pltpu.VMEM((tt, tt), BF16)],
        compiler_params=_params(("arbitrary", "arbitrary")),
        name="positions",
    )(ids)
    return pos[:, 0], counts[0, :N_BUCKETS].astype(jnp.int32)


def _route_plan(ids, tm, tt):
    t = ids.shape[0]
    pos, counts = _positions_call(ids, tm, tt)
    tiles = (counts + tm - 1) // tm
    tile_end = jnp.cumsum(tiles)
    tile_start = tile_end - tiles
    assert t % tm == 0
    n_tiles = t // tm + N_BUCKETS
    src = jnp.zeros((n_tiles * tm,), jnp.int32).at[pos].set(jnp.arange(t, dtype=jnp.int32))
    tile_ids = jnp.arange(n_tiles, dtype=jnp.int32)
    used = tile_end[-1]
    last = jnp.minimum(tile_ids, used - 1)
    tile_bucket = jnp.sum((tile_end[None, :] <= last[:, None]).astype(jnp.int32), axis=1)
    pairs = np.array(_PAIRS, np.int32)
    tgrp = tile_bucket // len(_PAIRS)
    tpair = tile_bucket % len(_PAIRS)
    tile_ea = tgrp * EXPERTS_PER_GROUP + jnp.take(jnp.asarray(pairs[:, 0]), tpair)
    tile_eb = tgrp * EXPERTS_PER_GROUP + jnp.take(jnp.asarray(pairs[:, 1]), tpair)
    rows_left = jnp.take(counts, tile_bucket) - (tile_ids - jnp.take(tile_start, tile_bucket)) * tm
    tile_rows = jnp.where(tile_ids < used, jnp.clip(rows_left, 0, tm), 0).astype(jnp.int32)
    return pos.astype(jnp.int32), src, tile_ea.astype(jnp.int32), tile_eb.astype(jnp.int32), tile_rows, used


def _row_copy(src_hbm, row, buf, slot, r, sem):
    return pltpu.make_async_copy(src_hbm.at[pl.ds(row, 1), :], buf.at[slot, pl.ds(r, 1), :], sem.at[slot])


def _gather_start(idx_ref, base, rows, src_hbm, buf, slot, sem):
    for r in range(rows):
        _row_copy(src_hbm, idx_ref[base + r], buf, slot, r, sem).start()


def _gather_wait(rows, src_hbm, buf, slot, sem):
    pltpu.make_async_copy(src_hbm.at[pl.ds(0, rows), :], buf.at[slot], sem.at[slot]).wait()


def _moe_kernel(src_ref, ea_ref, eb_ref, rows_ref, h_hbm, w1a_ref, w3a_ref, w2a_ref, w1b_ref, w3b_ref, w2b_ref,
                y_ref, buf, sem, *, tm):
    i = pl.program_id(0)
    slot = i % 2
    used = rows_ref[i] > 0
    d = y_ref.shape[1]

    @pl.when(i == 0)
    def _():
        _gather_start(src_ref, 0, tm, h_hbm, buf, 0, sem)

    @pl.when(jnp.logical_or(i == 0, rows_ref[jnp.maximum(i - 1, 0)] > 0))
    def _():
        _gather_wait(tm, h_hbm, buf, slot, sem)

    @pl.when(jnp.logical_not(used))
    def _():
        y_ref[...] = jnp.zeros(y_ref.shape, y_ref.dtype)

    @pl.when(used)
    def _():
        _gather_start(src_ref, (i + 1) * tm, tm, h_hbm, buf, 1 - slot, sem)
        h = buf[slot, :, :d].astype(BF16)

        def expert(w1_ref, w3_ref, w2_ref):
            a = (jax.nn.silu(jnp.dot(h, w1_ref[0], preferred_element_type=F32))
                 * jnp.dot(h, w3_ref[0], preferred_element_type=F32))
            return jnp.dot(a.astype(BF16), w2_ref[0], preferred_element_type=F32)

        w = buf[slot, :, d:]
        y_ref[...] = (expert(w1a_ref, w3a_ref, w2a_ref) * w[:, 0:1]
                      + expert(w1b_ref, w3b_ref, w2b_ref) * w[:, 1:2])


def _moe_call(src, tile_ea, tile_eb, tile_rows, hn, w1, w3, w2, tm):
    n_tiles = tile_ea.shape[0]
    d = w1.shape[1]
    f = w1.shape[2]
    wa = lambda i, src, ea, eb, rows: (ea[i], 0, 0)
    wb = lambda i, src, ea, eb, rows: (eb[i], 0, 0)
    grid_spec = pltpu.PrefetchScalarGridSpec(
        num_scalar_prefetch=4,
        grid=(n_tiles,),
        in_specs=[
            pl.BlockSpec(memory_space=pl.ANY),
            pl.BlockSpec((1, d, f), wa), pl.BlockSpec((1, d, f), wa), pl.BlockSpec((1, f, d), wa),
            pl.BlockSpec((1, d, f), wb), pl.BlockSpec((1, d, f), wb), pl.BlockSpec((1, f, d), wb),
        ],
        out_specs=pl.BlockSpec((tm, d), lambda i, *_: (i, 0)),
        scratch_shapes=[pltpu.VMEM((2, tm, hn.shape[1]), F32), pltpu.SemaphoreType.DMA((2,))],
    )
    return pl.pallas_call(
        functools.partial(_moe_kernel, tm=tm),
        grid_spec=grid_spec,
        out_shape=jax.ShapeDtypeStruct((n_tiles * tm, d), F32),
        compiler_params=_params(("arbitrary",)),
        name="moe",
    )(src, tile_ea, tile_eb, tile_rows, hn, w1, w3, w2, w1, w3, w2)


def _combine_kernel(pos_ref, y_hbm, x1_ref, mod_ref, op_ref, os_ref, buf, sem, *, tm, nt0):
    i = pl.program_id(0)
    nt = pl.num_programs(0)
    slot = i % 2

    @pl.when(i == 0)
    def _():
        _gather_start(pos_ref, 0, tm, y_hbm, buf, 0, sem)

    _gather_wait(tm, y_hbm, buf, slot, sem)

    @pl.when(i + 1 < nt)
    def _():
        _gather_start(pos_ref, (i + 1) * tm, tm, y_hbm, buf, 1 - slot, sem)

    out = x1_ref[...] + mod_ref[0, 5:6, :] * buf[slot]

    @pl.when(i < nt0)
    def _():
        op_ref[...] = out

    @pl.when(i >= nt0)
    def _():
        os_ref[...] = out


def _combine_call(lay, pos, y_sorted, x1, mod3, tm):
    t, d = x1.shape
    grid_spec = pltpu.PrefetchScalarGridSpec(
        num_scalar_prefetch=1,
        grid=(t // tm,),
        in_specs=[
            pl.BlockSpec(memory_space=pl.ANY),
            pl.BlockSpec((tm, d), lambda i, pos: (i, 0)),
            pl.BlockSpec((1, 6, d), lambda i, pos: (lay.batch_of_tile(i, tm), 0, 0)),
        ],
        out_specs=lay.split_specs(tm, d),
        scratch_shapes=[pltpu.VMEM((2, tm, d), F32), pltpu.SemaphoreType.DMA((2,))],
    )
    return pl.pallas_call(
        functools.partial(_combine_kernel, tm=tm, nt0=lay.t0 // tm),
        grid_spec=grid_spec,
        out_shape=lay.split_shapes(d, F32),
        compiler_params=_params(("arbitrary",)),
        name="combine",
    )(pos, y_sorted, x1, mod3)


def _rope_tables(n):
    n_rows = n // GRID_W
    rows = jnp.repeat(jnp.arange(n_rows), GRID_W).astype(F32)
    cols = jnp.tile(jnp.arange(GRID_W), n_rows).astype(F32)
    half = HEAD_DIM // 2
    inv = ROPE_THETA ** (-jnp.arange(0, half, 2, dtype=F32) / half)
    ang = jnp.concatenate([rows[:, None] * inv, cols[:, None] * inv], axis=-1)
    sign = jnp.tile(jnp.array([-1.0, 1.0], F32), half)
    return jnp.repeat(jnp.cos(ang), 2, axis=-1), jnp.repeat(jnp.sin(ang), 2, axis=-1) * sign


def _lambda_init(layer_idx):
    return 0.8 - 0.6 * math.exp(-0.3 * layer_idx)


def _tiled_gain(g):
    return jnp.tile(g, LANES // g.shape[-1]).reshape(1, LANES).astype(F32)


def kernel(x_prompt, x_sample, c_prompt, c_sample, rel_bias, w_ada, b_ada, g_norm_mix, w_in, g_qa, g_ka,
           g_qb, g_kb, lam_qk, g_subln, w_fourier, w_out, g_norm_ffn, w_group, w_expert, w1, w3, w2):
    b0, n0, d = x_prompt.shape
    b1, n1, _ = x_sample.shape
    depth = w_in.shape[0]
    lay = _Layout(b0, n0, b1, n1)
    trunks = ((0, b0, n0), (lay.t0, b1, n1))
    n_max = max(n0, n1)

    tm = _pick(math.gcd(n0, n1), 512)
    tq_a = _pick(math.gcd(n0, n1), 256)
    tk_a = _pick(math.gcd(n0, n1), 2048)
    t_b = _pick(math.gcd(n0, n1), 512)
    tf = _pick(math.gcd(n0, n1), 128)
    tm_moe = 256

    xp = x_prompt.reshape(b0 * n0, d)
    xs = x_sample.reshape(b1 * n1, d)
    nb = b0 + b1
    bp = -(-nb // 8) * 8
    c_all = jnp.zeros((bp, d), F32).at[:nb].set(jnp.concatenate([c_prompt, c_sample], axis=0))
    mod = _ada_call(c_all, w_ada, b_ada).reshape(depth, bp, 6, d)

    cos_e, sin_e = _rope_tables(n_max)
    chunk_ratio = {n: min(2, n // t_b) for n in (n0, n1)}
    bands = {c: _bias_tables(rel_bias, t_b, c) for c in sorted(set(chunk_ratio.values()))}
    dft = {n: _dft_tables(n, _pick(n, 128)) for n in sorted({n0, n1})}
    cidx = jnp.arange(C_GROUP_W, dtype=jnp.int32)
    ang_c = ((cidx[:, None] * cidx[None, :]) % C_GROUP_W).astype(F32) * (2.0 * math.pi / C_GROUP_W)
    dft_c = (jnp.concatenate([jnp.cos(ang_c), jnp.sin(ang_c)], axis=1) * C_GROUP_W ** -0.5).astype(BF16)

    for l in range(depth):
        mod3 = mod[l]
        lam_init = _lambda_init(l)
        qa, ka, va, qb, kb, vb, p, q = _proj_call(
            lay, xp, xs, mod3, g_norm_mix[l].reshape(1, d), w_in[l].astype(BF16),
            _tiled_gain(g_qa[l]), _tiled_gain(g_ka[l]), _tiled_gain(g_qb[l]), _tiled_gain(g_kb[l]),
            cos_e, sin_e, dft_c, tm)

        wf = w_fourier[l].astype(BF16)
        mix = jnp.zeros((lay.t, A_Q + B_WIDTH + C_WIDTH), BF16)
        for tok0, b, n in trunks:
            mix = _attn_a_call(qa, ka, va, mix, tok0, b, n, tq_a, min(tk_a, max(n // 2, LANES)))
            mix = _attn_b_call(qb, kb, vb, bands[chunk_ratio[n]], lam_qk[l], _tiled_gain(g_subln[l]), mix,
                               tok0, b, n, t_b, lam_init)
            mix = _fourier_call(dft[n][0], dft[n][1], p, q, wf, mix, tok0, b, n, tf)

        w_router = jnp.zeros((d, LANES), F32).at[:, :N_GROUPS].set(w_group[l])
        w_router = w_router.at[:, N_GROUPS:N_GROUPS + N_EXPERTS].set(w_expert[l]).astype(BF16)
        x1, hn2, ids = _out_call(lay, mix, xp, xs, mod3, w_out[l].astype(BF16),
                                 g_norm_ffn[l].reshape(1, d), w_router, tm)

        pos, src, tile_ea, tile_eb, tile_rows, _ = _route_plan(ids, tm_moe, _pick(math.gcd(n0, n1), 2048))
        y_sorted = _moe_call(src, tile_ea, tile_eb, tile_rows, hn2, w1[l].astype(BF16), w3[l].astype(BF16),
                             w2[l].astype(BF16), tm_moe)
        xp, xs = _combine_call(lay, pos, y_sorted, x1, mod3, tm)

    return (xp.reshape(b0, n0, d), xs.reshape(b1, n1, d))
```

```python
import functools
import math

import jax
import jax.numpy as jnp
import numpy as np
from jax import lax
from jax.experimental import pallas as pl
from jax.experimental.pallas import tpu as pltpu

F32 = jnp.float32
BF16 = jnp.bfloat16

D_MODEL = 2048
HEAD_DIM = 128
A_HEADS = 8
A_KV_HEADS = 2
A_GROUP = A_HEADS // A_KV_HEADS
A_Q = A_HEADS * HEAD_DIM
A_KV = A_KV_HEADS * HEAD_DIM
B_HEADS = 4
B_HALF = HEAD_DIM // 2
B_WIDTH = B_HEADS * HEAD_DIM
C_WIDTH = 512
C_GROUPS = 4
C_GROUP_W = 128
IN_WIDTH = A_Q + 2 * A_KV + 3 * B_WIDTH + C_WIDTH
GRID_W = 64
ROPE_THETA = 10000.0
NUM_BUCKETS = 32
MAX_DISTANCE = 128
N_GROUPS = 4
EXPERTS_PER_GROUP = 4
N_EXPERTS = 16
D_FF_EXPERT = 512
EPS = 1e-6
LOG2E = 1.4426950408889634

_PAIRS = ((0, 1), (0, 2), (0, 3), (1, 2), (1, 3), (2, 3))
N_BUCKETS = N_GROUPS * len(_PAIRS)

V7X_VMEM_BYTES = 64 * 1024 * 1024
VMEM_LIMIT = V7X_VMEM_BYTES - 8 * 1024 * 1024
LANES = 128
NEG_BIG = -1e30


def _params(sem, **kw):
    return pltpu.CompilerParams(dimension_semantics=sem, vmem_limit_bytes=VMEM_LIMIT, **kw)

def _pick(total, pref):
    t = min(pref, total)
    while total % t:
        t //= 2
    return t


def _ada_kernel(c_ref, w_ref, b_ref, o_ref):
    h = jax.nn.silu(c_ref[...]).astype(BF16)
    o_ref[0] = jnp.dot(h, w_ref[0].astype(BF16), preferred_element_type=F32) + b_ref[0]


def _ada_call(c_all, w_ada, b_ada):
    depth, d, e = w_ada.shape
    bp = c_all.shape[0]
    tn = _pick(e, 1024)
    return pl.pallas_call(
        _ada_kernel,
        grid=(depth, e // tn),
        in_specs=[
            pl.BlockSpec((bp, d), lambda l, j: (0, 0)),
            pl.BlockSpec((1, d, tn), lambda l, j: (l, 0, j)),
            pl.BlockSpec((1, 1, tn), lambda l, j: (l, 0, j)),
        ],
        out_specs=pl.BlockSpec((1, bp, tn), lambda l, j: (l, 0, j)),
        out_shape=jax.ShapeDtypeStruct((depth, bp, e), F32),
        compiler_params=_params(("arbitrary", "arbitrary")),
        name="ada",
    )(c_all, w_ada, b_ada.reshape(depth, 1, e))


class _Layout:
    def __init__(self, b0, n0, b1, n1):
        self.b = (b0, b1)
        self.n = (n0, n1)
        self.t0 = b0 * n0
        self.t = b0 * n0 + b1 * n1

    def batch_of_tile(self, i, tm):
        tok = i * tm
        return jnp.where(tok < self.t0, tok // self.n[0], self.b[0] + (tok - self.t0) // self.n[1])

    def pos_block_of_tile(self, i, tm):
        tok = i * tm
        pos = jnp.where(tok < self.t0, tok % self.n[0], (tok - self.t0) % self.n[1])
        return pos // tm

    def split_specs(self, tm, d):
        nt0 = self.t0 // tm
        return [pl.BlockSpec((tm, d), lambda i, *_: (jnp.minimum(i, nt0 - 1), 0)),
                pl.BlockSpec((tm, d), lambda i, *_: (jnp.maximum(i - nt0, 0), 0))]

    def split_shapes(self, d, dtype):
        return [jax.ShapeDtypeStruct((self.t0, d), dtype), jax.ShapeDtypeStruct((self.t - self.t0, d), dtype)]


def _split_apply(xp_ref, xs_ref, nt0, body, *rest):
    i = pl.program_id(0)

    @pl.when(i < nt0)
    def _():
        body(xp_ref, *rest)

    @pl.when(i >= nt0)
    def _():
        body(xs_ref, *rest)


def _head_norm(z, g):
    return z * lax.rsqrt(jnp.mean(z * z, axis=-1, keepdims=True) + EPS) * g


def _half_norm(z, g, lo):
    zz = z * z
    s_lo = jnp.sum(jnp.where(lo, zz, 0.0), axis=-1, keepdims=True)
    s_hi = jnp.sum(jnp.where(lo, 0.0, zz), axis=-1, keepdims=True)
    inv = jnp.where(lo, lax.rsqrt(s_lo / B_HALF + EPS), lax.rsqrt(s_hi / B_HALF + EPS))
    return z * inv * g


def _rope(z, c, s_signed, even):
    partner = jnp.where(even, pltpu.roll(z, LANES - 1, 1), pltpu.roll(z, 1, 1))
    return z * c + partner * s_signed


def _proj_kernel(xp_ref, xs_ref, *rest, nt0):
    _split_apply(xp_ref, xs_ref, nt0, _proj_body, *rest)


def _proj_body(x_ref, mod_ref, gn_ref, w_ref, gqa_ref, gka_ref, gqb_ref, gkb_ref, cos_ref, sin_ref,
               dft_ref, qa_ref, ka_ref, va_ref, qb_ref, kb_ref, vb_ref, p_ref, q_ref):
    x = x_ref[...]
    y = x * lax.rsqrt(jnp.mean(x * x, axis=-1, keepdims=True) + EPS) * gn_ref[...]
    hn = (y * (1.0 + mod_ref[0, 1:2, :]) + mod_ref[0, 0:1, :]).astype(BF16)

    def seg(a, b):
        return jnp.dot(hn, w_ref[:, a:b], preferred_element_type=F32)

    tm = x.shape[0]
    lane = lax.broadcasted_iota(jnp.int32, (tm, LANES), 1)
    even = (lane % 2) == 0
    lo = lane < B_HALF
    cos = cos_ref[...]
    sin = sin_ref[...]

    scale_a = HEAD_DIM ** -0.5 * LOG2E
    z = seg(0, A_Q)
    for h in range(A_HEADS):
        sl = slice(h * HEAD_DIM, (h + 1) * HEAD_DIM)
        qa_ref[:, sl] = (_rope(_head_norm(z[:, sl], gqa_ref[...]), cos, sin, even) * scale_a).astype(BF16)
    off = A_Q
    z = seg(off, off + A_KV)
    for h in range(A_KV_HEADS):
        sl = slice(h * HEAD_DIM, (h + 1) * HEAD_DIM)
        ka_ref[:, sl] = _rope(_head_norm(z[:, sl], gka_ref[...]), cos, sin, even).astype(BF16)
    off += A_KV
    va_ref[...] = seg(off, off + A_KV).astype(BF16)
    off += A_KV

    scale_b = B_HALF ** -0.5 * LOG2E
    z = seg(off, off + B_WIDTH)
    for h in range(B_HEADS):
        sl = slice(h * HEAD_DIM, (h + 1) * HEAD_DIM)
        qb_ref[:, sl] = (_half_norm(z[:, sl], gqb_ref[...], lo) * scale_b).astype(BF16)
    off += B_WIDTH
    z = seg(off, off + B_WIDTH)
    for h in range(B_HEADS):
        sl = slice(h * HEAD_DIM, (h + 1) * HEAD_DIM)
        kb_ref[:, sl] = _half_norm(z[:, sl], gkb_ref[...], lo).astype(BF16)
    off += B_WIDTH
    vb_ref[...] = seg(off, off + B_WIDTH).astype(BF16)
    off += B_WIDTH

    z = seg(off, off + C_WIDTH).astype(BF16)
    for g in range(C_GROUPS):
        sl = slice(g * C_GROUP_W, (g + 1) * C_GROUP_W)
        pq = jnp.dot(z[:, sl], dft_ref[...], preferred_element_type=F32)
        p_ref[:, sl] = pq[:, :C_GROUP_W]
        q_ref[:, sl] = pq[:, C_GROUP_W:]


def _proj_call(lay, xp, xs, mod3, gn, w_in, gqa, gka, gqb, gkb, cos_e, sin_e, dft_c, tm):
    t, d = lay.t, xp.shape[1]
    row = lambda i: (i, 0)
    const = lambda i: (0, 0)
    widths = (A_Q, A_KV, A_KV, B_WIDTH, B_WIDTH, B_WIDTH, C_WIDTH, C_WIDTH)
    return pl.pallas_call(
        functools.partial(_proj_kernel, nt0=lay.t0 // tm),
        grid=(t // tm,),
        in_specs=lay.split_specs(tm, d) + [
            pl.BlockSpec((1, 6, d), lambda i: (lay.batch_of_tile(i, tm), 0, 0)),
            pl.BlockSpec((1, d), const),
            pl.BlockSpec((d, IN_WIDTH), const),
            pl.BlockSpec((1, LANES), const),
            pl.BlockSpec((1, LANES), const),
            pl.BlockSpec((1, LANES), const),
            pl.BlockSpec((1, LANES), const),
            pl.BlockSpec((tm, LANES), lambda i: (lay.pos_block_of_tile(i, tm), 0)),
            pl.BlockSpec((tm, LANES), lambda i: (lay.pos_block_of_tile(i, tm), 0)),
            pl.BlockSpec((C_GROUP_W, 2 * C_GROUP_W), const),
        ],
        out_specs=[pl.BlockSpec((tm, w), row) for w in widths],
        out_shape=[jax.ShapeDtypeStruct((t, w), F32 if i >= 6 else BF16) for i, w in enumerate(widths)],
        compiler_params=_params(("arbitrary",)),
        name="proj",
    )(xp, xs, mod3, gn, w_in, gqa, gka, gqb, gkb, cos_e, sin_e, dft_c)


def _softmax_step(s, vc, m_ref, l_ref, acc_ref):
    m_prev = m_ref[...]
    m_new = jnp.maximum(m_prev, jnp.max(s, axis=1, keepdims=True))
    alpha = jnp.exp2(m_prev - m_new)
    p = jnp.exp2(s - jnp.tile(m_new, (1, s.shape[1] // LANES)))
    l_ref[...] = alpha * l_ref[...] + jnp.sum(p, axis=1, keepdims=True)
    acc_ref[...] = alpha * acc_ref[...] + jnp.dot(p.astype(BF16), vc, preferred_element_type=F32)
    m_ref[...] = m_new


_NT = (((1,), (1,)), ((), ()))


def _attn_a_kernel(q_ref, k_ref, v_ref, _merged_ref, o_ref, qs_ref, m_ref, l_ref, acc_ref, *, tq, tk, n):
    for g in range(A_GROUP):
        qs_ref[g * tq:(g + 1) * tq, :] = q_ref[:, g * HEAD_DIM:(g + 1) * HEAD_DIM]
    m_ref[...] = jnp.full(m_ref.shape, NEG_BIG, F32)
    l_ref[...] = jnp.zeros(l_ref.shape, F32)
    acc_ref[...] = jnp.zeros(acc_ref.shape, F32)

    def body(j, carry):
        rows = pl.ds(pl.multiple_of(j * tk, tk), tk)
        s = lax.dot_general(qs_ref[...], k_ref[rows, :], _NT, preferred_element_type=F32)
        _softmax_step(s, v_ref[rows, :], m_ref, l_ref, acc_ref)
        return carry

    lax.fori_loop(0, n // tk, body, 0, unroll=True)
    o = acc_ref[...] / l_ref[...]
    for g in range(A_GROUP):
        o_ref[:, g * HEAD_DIM:(g + 1) * HEAD_DIM] = o[g * tq:(g + 1) * tq].astype(o_ref.dtype)


def _attn_a_call(qa, ka, va, merged, tok0, b, n, tq, tk):
    qblk0 = tok0 // tq
    kblk0 = tok0 // n
    nq = n // tq
    rows = A_GROUP * tq
    return pl.pallas_call(
        functools.partial(_attn_a_kernel, tq=tq, tk=tk, n=n),
        grid=(b, A_KV_HEADS, nq),
        in_specs=[
            pl.BlockSpec((tq, A_GROUP * HEAD_DIM), lambda bi, kv, qi: (qblk0 + bi * nq + qi, kv)),
            pl.BlockSpec((n, HEAD_DIM), lambda bi, kv, qi: (kblk0 + bi, kv)),
            pl.BlockSpec((n, HEAD_DIM), lambda bi, kv, qi: (kblk0 + bi, kv)),
            pl.BlockSpec(memory_space=pl.ANY),
        ],
        out_specs=pl.BlockSpec((tq, A_GROUP * HEAD_DIM), lambda bi, kv, qi: (qblk0 + bi * nq + qi, kv)),
        out_shape=jax.ShapeDtypeStruct(merged.shape, merged.dtype),
        input_output_aliases={3: 0},
        scratch_shapes=[
            pltpu.VMEM((rows, HEAD_DIM), BF16),
            pltpu.VMEM((rows, LANES), F32),
            pltpu.VMEM((rows, LANES), F32),
            pltpu.VMEM((rows, HEAD_DIM), F32),
        ],
        compiler_params=_params(("arbitrary", "arbitrary", "arbitrary")),
        name="attn_a",
    )(qa, ka, va, merged)


def _attn_b_kernel(q_ref, k_ref, v_ref, band_ref, lam_ref, gs_ref, _merged_ref, o_ref,
                   qs_ref, m_ref, l_ref, acc_ref, *, t, ck, nchunks, lam_init):
    i = pl.program_id(2)
    c = ck // t
    q = q_ref[...]
    lo = lax.broadcasted_iota(jnp.int32, q.shape, 1) < B_HALF
    zero = jnp.zeros_like(q)
    qs_ref[0:t, :] = jnp.where(lo, q, zero)
    qs_ref[t:2 * t, :] = jnp.where(lo, zero, q)
    m_ref[...] = jnp.full(m_ref.shape, NEG_BIG, F32)
    l_ref[...] = jnp.zeros(l_ref.shape, F32)
    acc_ref[...] = jnp.zeros(acc_ref.shape, F32)

    def body(j, carry):
        rows = pl.ds(pl.multiple_of(j * ck, ck), ck)
        s = lax.dot_general(qs_ref[...], k_ref[rows, :], _NT, preferred_element_type=F32)
        lo_off, hi_off = _band_offsets(c)
        bias = band_ref[0, jnp.clip(c * j - i, lo_off, hi_off) - lo_off]
        s = (s.reshape(2, t, ck) + bias[None]).reshape(2 * t, ck)
        _softmax_step(s, v_ref[rows, :], m_ref, l_ref, acc_ref)
        return carry

    lax.fori_loop(0, nchunks, body, 0, unroll=True)
    o = acc_ref[...] / l_ref[...]
    lq = lam_ref[...]
    lam = (jnp.exp(jnp.sum(lq[0:1] * lq[1:2], axis=-1, keepdims=True))
           - jnp.exp(jnp.sum(lq[2:3] * lq[3:4], axis=-1, keepdims=True)) + lam_init)
    ob = o[0:t] - lam * o[t:2 * t]
    o_ref[...] = (_head_norm(ob, gs_ref[...]) * (1.0 - lam_init)).astype(o_ref.dtype)


def _attn_b_call(qb, kb, vb, band, lam_qk, g_subln, merged, tok0, b, n, t, lam_init):
    qblk0 = tok0 // t
    kblk0 = tok0 // n
    nq = n // t
    ck = band.shape[3]
    return pl.pallas_call(
        functools.partial(_attn_b_kernel, t=t, ck=ck, nchunks=n // ck, lam_init=lam_init),
        grid=(B_HEADS, b, nq),
        in_specs=[
            pl.BlockSpec((t, HEAD_DIM), lambda h, bi, qi: (qblk0 + bi * nq + qi, h)),
            pl.BlockSpec((n, HEAD_DIM), lambda h, bi, qi: (kblk0 + bi, h)),
            pl.BlockSpec((n, HEAD_DIM), lambda h, bi, qi: (kblk0 + bi, h)),
            pl.BlockSpec((1,) + band.shape[1:], lambda h, bi, qi: (h, 0, 0, 0), pipeline_mode=pl.Buffered(1)),
            pl.BlockSpec((4, B_HALF), lambda h, bi, qi: (0, 0)),
            pl.BlockSpec((1, LANES), lambda h, bi, qi: (0, 0)),
            pl.BlockSpec(memory_space=pl.ANY),
        ],
        out_specs=pl.BlockSpec((t, HEAD_DIM), lambda h, bi, qi: (qblk0 + bi * nq + qi, A_Q // HEAD_DIM + h)),
        out_shape=jax.ShapeDtypeStruct(merged.shape, merged.dtype),
        input_output_aliases={6: 0},
        scratch_shapes=[
            pltpu.VMEM((2 * t, HEAD_DIM), BF16),
            pltpu.VMEM((2 * t, LANES), F32),
            pltpu.VMEM((2 * t, LANES), F32),
            pltpu.VMEM((2 * t, HEAD_DIM), F32),
        ],
        compiler_params=_params(("arbitrary", "arbitrary", "arbitrary")),
        name="attn_b",
    )(qb, kb, vb, band, lam_qk, g_subln, merged)


def _t5_bucket(rel):
    nb = NUM_BUCKETS // 2
    ret = (rel > 0).astype(jnp.int32) * nb
    n = jnp.abs(rel)
    max_exact = nb // 2
    nf = jnp.maximum(n, 1).astype(F32)
    large = max_exact + (jnp.log(nf / max_exact) / math.log(MAX_DISTANCE / max_exact)
                         * (nb - max_exact)).astype(jnp.int32)
    large = jnp.minimum(large, nb - 1)
    return ret + jnp.where(n < max_exact, n, large)


_T5_SATURATION = math.ceil((NUM_BUCKETS // 4) * (MAX_DISTANCE / (NUM_BUCKETS // 4)) ** (
    (NUM_BUCKETS // 2 - 1 - NUM_BUCKETS // 4) / (NUM_BUCKETS // 2 - NUM_BUCKETS // 4)))


def _band_offsets(c):
    return -(c + 1), 2


def _band_kernel(vec_ref, o_ref, *, t, ck):
    width = vec_ref.shape[2]
    x = jnp.broadcast_to(vec_ref[0], (t, width))
    o_ref[0, 0] = pltpu.roll(x, width - t + 1, 1, stride=1, stride_axis=0)[:, :ck]


def _bias_tables(rel_bias, t, c):
    assert t + 1 >= _T5_SATURATION
    heads = rel_bias.shape[1]
    lo_off, hi_off = _band_offsets(c)
    nd = hi_off - lo_off + 1
    ck = c * t
    width = (c + 2) * t
    rel = ((jnp.arange(nd, dtype=jnp.int32)[:, None] + lo_off) * t
           + jnp.arange(-(t - 1), width - (t - 1), dtype=jnp.int32)[None, :])
    vec = jnp.take(rel_bias, _t5_bucket(rel), axis=0).astype(F32)
    vec = (vec.transpose(2, 0, 1) * LOG2E).reshape(heads * nd, 1, width)
    return pl.pallas_call(
        functools.partial(_band_kernel, t=t, ck=ck),
        grid=(heads, nd),
        in_specs=[pl.BlockSpec((1, 1, width), lambda h, d: (h * nd + d, 0, 0))],
        out_specs=pl.BlockSpec((1, 1, t, ck), lambda h, d: (h, d, 0, 0)),
        out_shape=jax.ShapeDtypeStruct((heads, nd, t, ck), F32),
        compiler_params=_params(("arbitrary", "arbitrary")),
        name="band",
    )(vec)


_DFT_INNER = 64


def _dft_stage_tables(n):
    n2 = min(_DFT_INNER, n // 8)
    n1 = n // n2
    i1 = jnp.arange(n1, dtype=jnp.int32)
    ang1 = ((i1[:, None] * i1[None, :]) % n1).astype(F32) * (2.0 * math.pi / n1)
    c1 = jnp.cos(ang1) * n ** -0.5
    s1 = jnp.sin(ang1) * n ** -0.5
    stage1 = jnp.concatenate([jnp.concatenate([c1, -s1], axis=1), jnp.concatenate([s1, c1], axis=1)], axis=0)
    i2 = jnp.arange(n2, dtype=jnp.int32)
    k = i1[:, None, None] + n1 * i2[None, :, None]
    ang2 = ((k * i2[None, None, :]) % n).astype(F32) * (2.0 * math.pi / n)
    stage2 = jnp.concatenate([jnp.cos(ang2), -jnp.sin(ang2)], axis=2)
    return stage1.astype(BF16), stage2.astype(BF16)


def _fourier_kernel(p_ref, q_ref, s1_ref, s2_ref, w_ref, _merged_ref, o_ref, y_ref, f_ref):
    n1, n2 = s2_ref.shape[0], s2_ref.shape[1]
    lanes = p_ref.shape[1]
    cols = 4
    stage1 = s1_ref[...]
    for t2 in range(0, n2, cols):
        zr = [p_ref[pl.ds(t2 + j, n1, stride=n2), :] for j in range(cols)]
        zi = [q_ref[pl.ds(t2 + j, n1, stride=n2), :] for j in range(cols)]
        z = jnp.concatenate([jnp.concatenate(zr, axis=1), jnp.concatenate(zi, axis=1)], axis=0)
        y = jnp.dot(stage1, z.astype(BF16), preferred_element_type=F32)
        for j in range(cols):
            y_ref[0, pl.ds(t2 + j, n1, stride=n2), :] = y[:n1, j * lanes:(j + 1) * lanes]
            y_ref[1, pl.ds(t2 + j, n1, stride=n2), :] = y[n1:, j * lanes:(j + 1) * lanes]
    for k1 in range(n1):
        blk = jnp.concatenate([y_ref[0, k1 * n2:(k1 + 1) * n2, :], y_ref[1, k1 * n2:(k1 + 1) * n2, :]], axis=0)
        f_ref[pl.ds(k1, n2, stride=n1), :] = jnp.dot(s2_ref[k1], blk.astype(BF16), preferred_element_type=F32)
    o_ref[...] = jnp.dot(f_ref[...].astype(BF16), w_ref[0], preferred_element_type=F32).astype(o_ref.dtype)


def _fourier_call(stage1, stage2, p, q, wf, merged, tok0, b, n):
    kblk0 = tok0 // n
    col0 = (A_Q + B_WIDTH) // C_GROUP_W
    return pl.pallas_call(
        _fourier_kernel,
        grid=(b, C_GROUPS),
        in_specs=[
            pl.BlockSpec((n, C_GROUP_W), lambda bi, g: (kblk0 + bi, g)),
            pl.BlockSpec((n, C_GROUP_W), lambda bi, g: (kblk0 + bi, g)),
            pl.BlockSpec(stage1.shape, lambda bi, g: (0, 0)),
            pl.BlockSpec(stage2.shape, lambda bi, g: (0, 0, 0)),
            pl.BlockSpec((1, C_GROUP_W, C_GROUP_W), lambda bi, g: (g, 0, 0)),
            pl.BlockSpec(memory_space=pl.ANY),
        ],
        out_specs=pl.BlockSpec((n, C_GROUP_W), lambda bi, g: (kblk0 + bi, col0 + g)),
        out_shape=jax.ShapeDtypeStruct(merged.shape, merged.dtype),
        input_output_aliases={5: 0},
        scratch_shapes=[pltpu.VMEM((2, n, C_GROUP_W), F32), pltpu.VMEM((n, C_GROUP_W), F32)],
        compiler_params=_params(("arbitrary", "arbitrary")),
        name="fourier",
    )(p, q, stage1, stage2, wf, merged)


def _out_kernel(xp_ref, xs_ref, *rest, nt0):
    _split_apply(xp_ref, xs_ref, nt0, _out_body, *rest)


def _out_body(x_ref, mix_ref, mod_ref, w_ref, gn_ref, wr_ref, x1_ref, hn_ref, ids_ref):
    mix = jnp.dot(mix_ref[...], w_ref[...], preferred_element_type=F32)
    x1 = x_ref[...] + mod_ref[0, 2:3, :] * mix
    x1_ref[...] = x1
    y = x1 * lax.rsqrt(jnp.mean(x1 * x1, axis=-1, keepdims=True) + EPS) * gn_ref[...]
    hn = y * (1.0 + mod_ref[0, 4:5, :]) + mod_ref[0, 3:4, :]
    d = hn.shape[1]
    hn_ref[:, :d] = hn

    logits = jnp.dot(hn.astype(BF16), wr_ref[...], preferred_element_type=F32)
    lane = lax.broadcasted_iota(jnp.int32, logits.shape, 1)
    big = jnp.int32(LANES)
    is_g = lane < N_GROUPS
    gmax = jnp.max(jnp.where(is_g, logits, -jnp.inf), axis=-1, keepdims=True)
    g_sel = jnp.min(jnp.where(jnp.logical_and(is_g, logits == gmax), lane, big), axis=-1, keepdims=True)
    g_w = 1.0 / jnp.sum(jnp.where(is_g, jnp.exp(logits - gmax), 0.0), axis=-1, keepdims=True)
    lo_lane = N_GROUPS + g_sel * EXPERTS_PER_GROUP
    in_g = jnp.logical_and(lane >= lo_lane, lane < lo_lane + EXPERTS_PER_GROUP)
    v0 = jnp.max(jnp.where(in_g, logits, -jnp.inf), axis=-1, keepdims=True)
    i0 = jnp.min(jnp.where(jnp.logical_and(in_g, logits == v0), lane, big), axis=-1, keepdims=True)
    rest = jnp.logical_and(in_g, lane != i0)
    v1 = jnp.max(jnp.where(rest, logits, -jnp.inf), axis=-1, keepdims=True)
    i1 = jnp.min(jnp.where(jnp.logical_and(rest, logits == v1), lane, big), axis=-1, keepdims=True)
    e1 = jnp.exp(v1 - v0)
    w0 = g_w / (1.0 + e1)
    w1 = g_w * e1 / (1.0 + e1)
    swap = i1 < i0
    ea = jnp.where(swap, i1, i0) - N_GROUPS
    eb = jnp.where(swap, i0, i1) - N_GROUPS
    wa = jnp.where(swap, w1, w0)
    wb = jnp.where(swap, w0, w1)
    ml = lax.broadcasted_iota(jnp.int32, ids_ref.shape, 1)
    ids_ref[...] = jnp.where(ml == 0, ea, jnp.where(ml == 1, eb, 0))
    hn_ref[:, d:] = jnp.where(lane == 0, wa, jnp.where(lane == 1, wb, 0.0))


def _out_call(lay, mix, xp, xs, mod3, w_out, gn, w_router, tm):
    t, d = lay.t, xp.shape[1]
    row = lambda i: (i, 0)
    const = lambda i: (0, 0)
    meta = LANES
    return pl.pallas_call(
        functools.partial(_out_kernel, nt0=lay.t0 // tm),
        grid=(t // tm,),
        in_specs=lay.split_specs(tm, d) + [
            pl.BlockSpec((tm, mix.shape[1]), row),
            pl.BlockSpec((1, 6, d), lambda i: (lay.batch_of_tile(i, tm), 0, 0)),
            pl.BlockSpec((d, d), const),
            pl.BlockSpec((1, d), const),
            pl.BlockSpec((d, LANES), const),
        ],
        out_specs=[pl.BlockSpec((tm, d), row), pl.BlockSpec((tm, d + LANES), row),
                   pl.BlockSpec((tm, meta), row)],
        out_shape=[jax.ShapeDtypeStruct((t, d), F32), jax.ShapeDtypeStruct((t, d + LANES), F32),
                   jax.ShapeDtypeStruct((t, meta), jnp.int32)],
        compiler_params=_params(("arbitrary",)),
        name="out_proj",
    )(xp, xs, mix, mod3, w_out, gn, w_router)


def _bucket_onehot(ids_ref):
    ids = ids_ref[...]
    ea = ids[:, 0:1]
    eb = ids[:, 1:2]
    la = ea % EXPERTS_PER_GROUP
    lb = eb % EXPERTS_PER_GROUP
    pair = la * (2 * EXPERTS_PER_GROUP - 1 - la) // 2 + (lb - la - 1)
    bucket = (ea // EXPERTS_PER_GROUP) * len(_PAIRS) + pair
    lane = lax.broadcasted_iota(jnp.int32, (ids.shape[0], LANES), 1)
    return lane == bucket


def _positions_kernel(ids_ref, pos_ref, counts_ref, run_ref, base_ref, start_ref, earlier_ref, *, tm):
    p = pl.program_id(0)
    i = pl.program_id(1)
    tt = ids_ref.shape[0]
    onehot = _bucket_onehot(ids_ref)

    @pl.when(jnp.logical_and(p == 0, i == 0))
    def _():
        run_ref[...] = jnp.zeros(run_ref.shape, F32)

    @pl.when(p == 0)
    def _():
        base_ref[pl.ds(i, 1), :] = run_ref[...]
        run_ref[...] += jnp.sum(jnp.where(onehot, 1.0, 0.0), axis=0, keepdims=True)

    @pl.when(jnp.logical_and(p == 1, i == 0))
    def _():
        counts = run_ref[...]
        counts_ref[...] = counts
        tiles = jnp.floor((counts + (tm - 1)) * (1.0 / tm)).astype(BF16)
        a = lax.broadcasted_iota(jnp.int32, (LANES, LANES), 0)
        b = lax.broadcasted_iota(jnp.int32, (LANES, LANES), 1)
        before = jnp.where(a < b, 1.0, 0.0).astype(BF16)
        start_ref[...] = jnp.dot(tiles, before, preferred_element_type=F32) * tm
        r = lax.broadcasted_iota(jnp.int32, (tt, tt), 0)
        c = lax.broadcasted_iota(jnp.int32, (tt, tt), 1)
        earlier_ref[...] = jnp.where(c < r, 1.0, 0.0).astype(BF16)

    @pl.when(p == 1)
    def _():
        oh = jnp.where(onehot, 1.0, 0.0)
        rank = jnp.dot(earlier_ref[...], oh.astype(BF16), preferred_element_type=F32)
        val = start_ref[...] + base_ref[pl.ds(i, 1), :] + rank
        pos = jnp.sum(oh * val, axis=1, keepdims=True)
        pos_ref[...] = jnp.broadcast_to(pos, pos_ref.shape).astype(jnp.int32)


def _positions_call(ids, tm, tt):
    t, meta = ids.shape
    nt = t // tt
    assert t // tm + 1 <= 256 and tm & (tm - 1) == 0
    pos, counts = pl.pallas_call(
        functools.partial(_positions_kernel, tm=tm),
        grid=(2, nt),
        in_specs=[pl.BlockSpec((tt, meta), lambda p, i: (i, 0))],
        out_specs=[pl.BlockSpec((tt, meta), lambda p, i: (i * p, 0)),
                   pl.BlockSpec((1, LANES), lambda p, i: (0, 0))],
        out_shape=[jax.ShapeDtypeStruct((t, meta), jnp.int32), jax.ShapeDtypeStruct((1, LANES), F32)],
        scratch_shapes=[pltpu.VMEM((1, LANES), F32), pltpu.VMEM((nt, LANES), F32), pltpu.VMEM((1, LANES), F32),
                        pltpu.VMEM((tt, tt), BF16)],
        compiler_params=_params(("arbitrary", "arbitrary")),
        name="positions",
    )(ids)
    return pos[:, 0], counts[0, :N_BUCKETS].astype(jnp.int32)


def _route_plan(ids, tm, tt):
    t = ids.shape[0]
    pos, counts = _positions_call(ids, tm, tt)
    tiles = (counts + tm - 1) // tm
    tile_end = jnp.cumsum(tiles)
    tile_start = tile_end - tiles
    assert t % tm == 0
    n_tiles = t // tm + N_BUCKETS
    src = jnp.zeros((n_tiles * tm,), jnp.int32).at[pos].set(jnp.arange(t, dtype=jnp.int32))
    tile_ids = jnp.arange(n_tiles, dtype=jnp.int32)
    used = tile_end[-1]
    last = jnp.minimum(tile_ids, used - 1)
    tile_bucket = jnp.sum((tile_end[None, :] <= last[:, None]).astype(jnp.int32), axis=1)
    pairs = np.array(_PAIRS, np.int32)
    tgrp = tile_bucket // len(_PAIRS)
    tpair = tile_bucket % len(_PAIRS)
    tile_ea = tgrp * EXPERTS_PER_GROUP + jnp.take(jnp.asarray(pairs[:, 0]), tpair)
    tile_eb = tgrp * EXPERTS_PER_GROUP + jnp.take(jnp.asarray(pairs[:, 1]), tpair)
    rows_left = jnp.take(counts, tile_bucket) - (tile_ids - jnp.take(tile_start, tile_bucket)) * tm
    tile_rows = jnp.where(tile_ids < used, jnp.clip(rows_left, 0, tm), 0).astype(jnp.int32)
    return pos.astype(jnp.int32), src, tile_ea.astype(jnp.int32), tile_eb.astype(jnp.int32), tile_rows, used


def _row_copy(src_hbm, row, buf, slot, r, sem):
    return pltpu.make_async_copy(src_hbm.at[pl.ds(row, 1), :], buf.at[slot, pl.ds(r, 1), :], sem.at[slot])


def _gather_start(idx_ref, base, rows, src_hbm, buf, slot, sem):
    for r in range(rows):
        _row_copy(src_hbm, idx_ref[base + r], buf, slot, r, sem).start()


def _gather_wait(rows, src_hbm, buf, slot, sem):
    pltpu.make_async_copy(src_hbm.at[pl.ds(0, rows), :], buf.at[slot], sem.at[slot]).wait()


def _moe_kernel(src_ref, ea_ref, eb_ref, rows_ref, h_hbm, w1a_ref, w3a_ref, w2a_ref, w1b_ref, w3b_ref, w2b_ref,
                y_ref, buf, sem, *, tm):
    i = pl.program_id(0)
    slot = i % 2
    used = rows_ref[i] > 0
    d = y_ref.shape[1]

    @pl.when(i == 0)
    def _():
        _gather_start(src_ref, 0, tm, h_hbm, buf, 0, sem)

    @pl.when(jnp.logical_or(i == 0, rows_ref[jnp.maximum(i - 1, 0)] > 0))
    def _():
        _gather_wait(tm, h_hbm, buf, slot, sem)

    @pl.when(jnp.logical_not(used))
    def _():
        y_ref[...] = jnp.zeros(y_ref.shape, y_ref.dtype)

    @pl.when(used)
    def _():
        _gather_start(src_ref, (i + 1) * tm, tm, h_hbm, buf, 1 - slot, sem)
        h = buf[slot, :, :d].astype(BF16)

        def expert(w1_ref, w3_ref, w2_ref):
            a = (jax.nn.silu(jnp.dot(h, w1_ref[0], preferred_element_type=F32))
                 * jnp.dot(h, w3_ref[0], preferred_element_type=F32))
            return jnp.dot(a.astype(BF16), w2_ref[0], preferred_element_type=F32)

        w = buf[slot, :, d:]
        y_ref[...] = (expert(w1a_ref, w3a_ref, w2a_ref) * w[:, 0:1]
                      + expert(w1b_ref, w3b_ref, w2b_ref) * w[:, 1:2])


def _moe_call(src, tile_ea, tile_eb, tile_rows, hn, w1, w3, w2, tm):
    n_tiles = tile_ea.shape[0]
    d = w1.shape[1]
    f = w1.shape[2]
    wa = lambda i, src, ea, eb, rows: (ea[i], 0, 0)
    wb = lambda i, src, ea, eb, rows: (eb[i], 0, 0)
    grid_spec = pltpu.PrefetchScalarGridSpec(
        num_scalar_prefetch=4,
        grid=(n_tiles,),
        in_specs=[
            pl.BlockSpec(memory_space=pl.ANY),
            pl.BlockSpec((1, d, f), wa), pl.BlockSpec((1, d, f), wa), pl.BlockSpec((1, f, d), wa),
            pl.BlockSpec((1, d, f), wb), pl.BlockSpec((1, d, f), wb), pl.BlockSpec((1, f, d), wb),
        ],
        out_specs=pl.BlockSpec((tm, d), lambda i, *_: (i, 0)),
        scratch_shapes=[pltpu.VMEM((2, tm, hn.shape[1]), F32), pltpu.SemaphoreType.DMA((2,))],
    )
    return pl.pallas_call(
        functools.partial(_moe_kernel, tm=tm),
        grid_spec=grid_spec,
        out_shape=jax.ShapeDtypeStruct((n_tiles * tm, d), F32),
        compiler_params=_params(("arbitrary",)),
        name="moe",
    )(src, tile_ea, tile_eb, tile_rows, hn, w1, w3, w2, w1, w3, w2)


def _combine_kernel(pos_ref, y_hbm, x1_ref, mod_ref, op_ref, os_ref, buf, sem, *, tm, nt0):
    i = pl.program_id(0)
    nt = pl.num_programs(0)
    slot = i % 2

    @pl.when(i == 0)
    def _():
        _gather_start(pos_ref, 0, tm, y_hbm, buf, 0, sem)

    _gather_wait(tm, y_hbm, buf, slot, sem)

    @pl.when(i + 1 < nt)
    def _():
        _gather_start(pos_ref, (i + 1) * tm, tm, y_hbm, buf, 1 - slot, sem)

    out = x1_ref[...] + mod_ref[0, 5:6, :] * buf[slot]

    @pl.when(i < nt0)
    def _():
        op_ref[...] = out

    @pl.when(i >= nt0)
    def _():
        os_ref[...] = out


def _combine_call(lay, pos, y_sorted, x1, mod3, tm):
    t, d = x1.shape
    grid_spec = pltpu.PrefetchScalarGridSpec(
        num_scalar_prefetch=1,
        grid=(t // tm,),
        in_specs=[
            pl.BlockSpec(memory_space=pl.ANY),
            pl.BlockSpec((tm, d), lambda i, pos: (i, 0)),
            pl.BlockSpec((1, 6, d), lambda i, pos: (lay.batch_of_tile(i, tm), 0, 0)),
        ],
        out_specs=lay.split_specs(tm, d),
        scratch_shapes=[pltpu.VMEM((2, tm, d), F32), pltpu.SemaphoreType.DMA((2,))],
    )
    return pl.pallas_call(
        functools.partial(_combine_kernel, tm=tm, nt0=lay.t0 // tm),
        grid_spec=grid_spec,
        out_shape=lay.split_shapes(d, F32),
        compiler_params=_params(("arbitrary",)),
        name="combine",
    )(pos, y_sorted, x1, mod3)


def _rope_tables(n):
    n_rows = n // GRID_W
    rows = jnp.repeat(jnp.arange(n_rows), GRID_W).astype(F32)
    cols = jnp.tile(jnp.arange(GRID_W), n_rows).astype(F32)
    half = HEAD_DIM // 2
    inv = ROPE_THETA ** (-jnp.arange(0, half, 2, dtype=F32) / half)
    ang = jnp.concatenate([rows[:, None] * inv, cols[:, None] * inv], axis=-1)
    sign = jnp.tile(jnp.array([-1.0, 1.0], F32), half)
    return jnp.repeat(jnp.cos(ang), 2, axis=-1), jnp.repeat(jnp.sin(ang), 2, axis=-1) * sign


def _lambda_init(layer_idx):
    return 0.8 - 0.6 * math.exp(-0.3 * layer_idx)


def _tiled_gain(g):
    return jnp.tile(g, LANES // g.shape[-1]).reshape(1, LANES).astype(F32)


def kernel(x_prompt, x_sample, c_prompt, c_sample, rel_bias, w_ada, b_ada, g_norm_mix, w_in, g_qa, g_ka,
           g_qb, g_kb, lam_qk, g_subln, w_fourier, w_out, g_norm_ffn, w_group, w_expert, w1, w3, w2):
    b0, n0, d = x_prompt.shape
    b1, n1, _ = x_sample.shape
    depth = w_in.shape[0]
    lay = _Layout(b0, n0, b1, n1)
    trunks = ((0, b0, n0), (lay.t0, b1, n1))
    n_max = max(n0, n1)

    tm = _pick(math.gcd(n0, n1), 512)
    tq_a = _pick(math.gcd(n0, n1), 256)
    tk_a = _pick(math.gcd(n0, n1), 2048)
    t_b = _pick(math.gcd(n0, n1), 512)
    tm_moe = 256

    xp = x_prompt.reshape(b0 * n0, d)
    xs = x_sample.reshape(b1 * n1, d)
    nb = b0 + b1
    bp = -(-nb // 8) * 8
    c_all = jnp.zeros((bp, d), F32).at[:nb].set(jnp.concatenate([c_prompt, c_sample], axis=0))
    mod = _ada_call(c_all, w_ada, b_ada).reshape(depth, bp, 6, d)

    cos_e, sin_e = _rope_tables(n_max)
    chunk_ratio = {n: min(2, n // t_b) for n in (n0, n1)}
    bands = {c: _bias_tables(rel_bias, t_b, c) for c in sorted(set(chunk_ratio.values()))}
    dft = {n: _dft_stage_tables(n) for n in sorted({n0, n1})}
    cidx = jnp.arange(C_GROUP_W, dtype=jnp.int32)
    ang_c = ((cidx[:, None] * cidx[None, :]) % C_GROUP_W).astype(F32) * (2.0 * math.pi / C_GROUP_W)
    dft_c = (jnp.concatenate([jnp.cos(ang_c), jnp.sin(ang_c)], axis=1) * C_GROUP_W ** -0.5).astype(BF16)

    for l in range(depth):
        mod3 = mod[l]
        lam_init = _lambda_init(l)
        qa, ka, va, qb, kb, vb, p, q = _proj_call(
            lay, xp, xs, mod3, g_norm_mix[l].reshape(1, d), w_in[l].astype(BF16),
            _tiled_gain(g_qa[l]), _tiled_gain(g_ka[l]), _tiled_gain(g_qb[l]), _tiled_gain(g_kb[l]),
            cos_e, sin_e, dft_c, tm)

        wf = w_fourier[l].astype(BF16)
        mix = jnp.zeros((lay.t, A_Q + B_WIDTH + C_WIDTH), BF16)
        for tok0, b, n in trunks:
            mix = _attn_a_call(qa, ka, va, mix, tok0, b, n, tq_a, min(tk_a, max(n // 2, LANES)))
            mix = _attn_b_call(qb, kb, vb, bands[chunk_ratio[n]], lam_qk[l], _tiled_gain(g_subln[l]), mix,
                               tok0, b, n, t_b, lam_init)
            mix = _fourier_call(dft[n][0], dft[n][1], p, q, wf, mix, tok0, b, n)

        w_router = jnp.zeros((d, LANES), F32).at[:, :N_GROUPS].set(w_group[l])
        w_router = w_router.at[:, N_GROUPS:N_GROUPS + N_EXPERTS].set(w_expert[l]).astype(BF16)
        x1, hn2, ids = _out_call(lay, mix, xp, xs, mod3, w_out[l].astype(BF16),
                                 g_norm_ffn[l].reshape(1, d), w_router, tm)

        pos, src, tile_ea, tile_eb, tile_rows, _ = _route_plan(ids, tm_moe, _pick(math.gcd(n0, n1), 2048))
        y_sorted = _moe_call(src, tile_ea, tile_eb, tile_rows, hn2, w1[l].astype(BF16), w3[l].astype(BF16),
                             w2[l].astype(BF16), tm_moe)
        xp, xs = _combine_call(lay, pos, y_sorted, x1, mod3, tm)

    return (xp.reshape(b0, n0, d), xs.reshape(b1, n1, d))
```

```python
import functools
import math

import jax
import jax.numpy as jnp
import numpy as np
from jax import lax
from jax.experimental import pallas as pl
from jax.experimental.pallas import tpu as pltpu

F32 = jnp.float32
BF16 = jnp.bfloat16

D_MODEL = 2048
HEAD_DIM = 128
A_HEADS = 8
A_KV_HEADS = 2
A_GROUP = A_HEADS // A_KV_HEADS
A_Q = A_HEADS * HEAD_DIM
A_KV = A_KV_HEADS * HEAD_DIM
B_HEADS = 4
B_HALF = HEAD_DIM // 2
B_WIDTH = B_HEADS * HEAD_DIM
C_WIDTH = 512
C_GROUPS = 4
C_GROUP_W = 128
IN_WIDTH = A_Q + 2 * A_KV + 3 * B_WIDTH + C_WIDTH
GRID_W = 64
ROPE_THETA = 10000.0
NUM_BUCKETS = 32
MAX_DISTANCE = 128
N_GROUPS = 4
EXPERTS_PER_GROUP = 4
N_EXPERTS = 16
D_FF_EXPERT = 512
EPS = 1e-6
LOG2E = 1.4426950408889634

_PAIRS = ((0, 1), (0, 2), (0, 3), (1, 2), (1, 3), (2, 3))
N_BUCKETS = N_GROUPS * len(_PAIRS)

V7X_VMEM_BYTES = 64 * 1024 * 1024
VMEM_LIMIT = V7X_VMEM_BYTES - 8 * 1024 * 1024
LANES = 128
NEG_BIG = -1e30


def _params(sem, **kw):
    return pltpu.CompilerParams(dimension_semantics=sem, vmem_limit_bytes=VMEM_LIMIT, **kw)

def _pick(total, pref):
    t = min(pref, total)
    while total % t:
        t //= 2
    return t


def _ada_kernel(c_ref, w_ref, b_ref, o_ref):
    h = jax.nn.silu(c_ref[...]).astype(BF16)
    o_ref[0] = jnp.dot(h, w_ref[0].astype(BF16), preferred_element_type=F32) + b_ref[0]


def _ada_call(c_all, w_ada, b_ada):
    depth, d, e = w_ada.shape
    bp = c_all.shape[0]
    tn = _pick(e, 1024)
    return pl.pallas_call(
        _ada_kernel,
        grid=(depth, e // tn),
        in_specs=[
            pl.BlockSpec((bp, d), lambda l, j: (0, 0)),
            pl.BlockSpec((1, d, tn), lambda l, j: (l, 0, j)),
            pl.BlockSpec((1, 1, tn), lambda l, j: (l, 0, j)),
        ],
        out_specs=pl.BlockSpec((1, bp, tn), lambda l, j: (l, 0, j)),
        out_shape=jax.ShapeDtypeStruct((depth, bp, e), F32),
        compiler_params=_params(("arbitrary", "arbitrary")),
        name="ada",
    )(c_all, w_ada, b_ada.reshape(depth, 1, e))


class _Layout:
    def __init__(self, b0, n0, b1, n1):
        self.b = (b0, b1)
        self.n = (n0, n1)
        self.t0 = b0 * n0
        self.t = b0 * n0 + b1 * n1

    def batch_of_tile(self, i, tm):
        tok = i * tm
        return jnp.where(tok < self.t0, tok // self.n[0], self.b[0] + (tok - self.t0) // self.n[1])

    def pos_block_of_tile(self, i, tm):
        tok = i * tm
        pos = jnp.where(tok < self.t0, tok % self.n[0], (tok - self.t0) % self.n[1])
        return pos // tm

    def split_specs(self, tm, d):
        nt0 = self.t0 // tm
        return [pl.BlockSpec((tm, d), lambda i, *_: (jnp.minimum(i, nt0 - 1), 0)),
                pl.BlockSpec((tm, d), lambda i, *_: (jnp.maximum(i - nt0, 0), 0))]

    def split_shapes(self, d, dtype):
        return [jax.ShapeDtypeStruct((self.t0, d), dtype), jax.ShapeDtypeStruct((self.t - self.t0, d), dtype)]


def _split_apply(xp_ref, xs_ref, nt0, body, *rest):
    i = pl.program_id(0)

    @pl.when(i < nt0)
    def _():
        body(xp_ref, *rest)

    @pl.when(i >= nt0)
    def _():
        body(xs_ref, *rest)


def _head_norm(z, g):
    return z * lax.rsqrt(jnp.mean(z * z, axis=-1, keepdims=True) + EPS) * g


def _half_norm(z, g, lo):
    zz = z * z
    s_lo = jnp.sum(jnp.where(lo, zz, 0.0), axis=-1, keepdims=True)
    s_hi = jnp.sum(jnp.where(lo, 0.0, zz), axis=-1, keepdims=True)
    inv = jnp.where(lo, lax.rsqrt(s_lo / B_HALF + EPS), lax.rsqrt(s_hi / B_HALF + EPS))
    return z * inv * g


def _rope(z, c, s_signed, even):
    partner = jnp.where(even, pltpu.roll(z, LANES - 1, 1), pltpu.roll(z, 1, 1))
    return z * c + partner * s_signed


def _proj_kernel(xp_ref, xs_ref, *rest, nt0):
    _split_apply(xp_ref, xs_ref, nt0, _proj_body, *rest)


def _proj_body(x_ref, mod_ref, gn_ref, w_ref, gqa_ref, gka_ref, gqb_ref, gkb_ref, cos_ref, sin_ref,
               dft_ref, qa_ref, ka_ref, va_ref, qb_ref, kb_ref, vb_ref, p_ref, q_ref):
    x = x_ref[...]
    y = x * lax.rsqrt(jnp.mean(x * x, axis=-1, keepdims=True) + EPS) * gn_ref[...]
    hn = (y * (1.0 + mod_ref[0, 1:2, :]) + mod_ref[0, 0:1, :]).astype(BF16)

    def seg(a, b):
        return jnp.dot(hn, w_ref[:, a:b], preferred_element_type=F32)

    tm = x.shape[0]
    lane = lax.broadcasted_iota(jnp.int32, (tm, LANES), 1)
    even = (lane % 2) == 0
    lo = lane < B_HALF
    cos = cos_ref[...]
    sin = sin_ref[...]

    scale_a = HEAD_DIM ** -0.5 * LOG2E
    z = seg(0, A_Q)
    for h in range(A_HEADS):
        sl = slice(h * HEAD_DIM, (h + 1) * HEAD_DIM)
        qa_ref[:, sl] = (_rope(_head_norm(z[:, sl], gqa_ref[...]), cos, sin, even) * scale_a).astype(BF16)
    off = A_Q
    z = seg(off, off + A_KV)
    for h in range(A_KV_HEADS):
        sl = slice(h * HEAD_DIM, (h + 1) * HEAD_DIM)
        ka_ref[:, sl] = _rope(_head_norm(z[:, sl], gka_ref[...]), cos, sin, even).astype(BF16)
    off += A_KV
    va_ref[...] = seg(off, off + A_KV).astype(BF16)
    off += A_KV

    scale_b = B_HALF ** -0.5 * LOG2E
    z = seg(off, off + B_WIDTH)
    for h in range(B_HEADS):
        sl = slice(h * HEAD_DIM, (h + 1) * HEAD_DIM)
        qb_ref[:, sl] = (_half_norm(z[:, sl], gqb_ref[...], lo) * scale_b).astype(BF16)
    off += B_WIDTH
    z = seg(off, off + B_WIDTH)
    for h in range(B_HEADS):
        sl = slice(h * HEAD_DIM, (h + 1) * HEAD_DIM)
        kb_ref[:, sl] = _half_norm(z[:, sl], gkb_ref[...], lo).astype(BF16)
    off += B_WIDTH
    vb_ref[...] = seg(off, off + B_WIDTH).astype(BF16)
    off += B_WIDTH

    z = seg(off, off + C_WIDTH).astype(BF16)
    for g in range(C_GROUPS):
        sl = slice(g * C_GROUP_W, (g + 1) * C_GROUP_W)
        pq = jnp.dot(z[:, sl], dft_ref[...], preferred_element_type=F32)
        p_ref[:, sl] = pq[:, :C_GROUP_W]
        q_ref[:, sl] = pq[:, C_GROUP_W:]


def _proj_call(lay, xp, xs, mod3, gn, w_in, gqa, gka, gqb, gkb, cos_e, sin_e, dft_c, tm):
    t, d = lay.t, xp.shape[1]
    row = lambda i: (i, 0)
    const = lambda i: (0, 0)
    widths = (A_Q, A_KV, A_KV, B_WIDTH, B_WIDTH, B_WIDTH, C_WIDTH, C_WIDTH)
    return pl.pallas_call(
        functools.partial(_proj_kernel, nt0=lay.t0 // tm),
        grid=(t // tm,),
        in_specs=lay.split_specs(tm, d) + [
            pl.BlockSpec((1, 6, d), lambda i: (lay.batch_of_tile(i, tm), 0, 0)),
            pl.BlockSpec((1, d), const),
            pl.BlockSpec((d, IN_WIDTH), const),
            pl.BlockSpec((1, LANES), const),
            pl.BlockSpec((1, LANES), const),
            pl.BlockSpec((1, LANES), const),
            pl.BlockSpec((1, LANES), const),
            pl.BlockSpec((tm, LANES), lambda i: (lay.pos_block_of_tile(i, tm), 0)),
            pl.BlockSpec((tm, LANES), lambda i: (lay.pos_block_of_tile(i, tm), 0)),
            pl.BlockSpec((C_GROUP_W, 2 * C_GROUP_W), const),
        ],
        out_specs=[pl.BlockSpec((tm, w), row) for w in widths],
        out_shape=[jax.ShapeDtypeStruct((t, w), F32 if i >= 6 else BF16) for i, w in enumerate(widths)],
        compiler_params=_params(("arbitrary",)),
        name="proj",
    )(xp, xs, mod3, gn, w_in, gqa, gka, gqb, gkb, cos_e, sin_e, dft_c)


def _softmax_step(s, vc, m_ref, l_ref, acc_ref, rows=slice(None)):
    m_prev = m_ref[rows, :]
    m_new = jnp.maximum(m_prev, jnp.max(s, axis=1, keepdims=True))
    alpha = jnp.exp2(m_prev - m_new)
    p = jnp.exp2(s - jnp.tile(m_new, (1, s.shape[1] // LANES)))
    l_ref[rows, :] = alpha * l_ref[rows, :] + jnp.sum(p, axis=1, keepdims=True)
    acc_ref[rows, :] = alpha * acc_ref[rows, :] + jnp.dot(p.astype(BF16), vc, preferred_element_type=F32)
    m_ref[rows, :] = m_new


_NT = (((1,), (1,)), ((), ()))


def _attn_a_kernel(q_ref, k_ref, v_ref, _merged_ref, o_ref, qs_ref, m_ref, l_ref, acc_ref, *, tq, tk, n):
    for g in range(A_GROUP):
        qs_ref[g * tq:(g + 1) * tq, :] = q_ref[:, g * HEAD_DIM:(g + 1) * HEAD_DIM]
    m_ref[...] = jnp.full(m_ref.shape, NEG_BIG, F32)
    l_ref[...] = jnp.zeros(l_ref.shape, F32)
    acc_ref[...] = jnp.zeros(acc_ref.shape, F32)

    def body(j, carry):
        rows = pl.ds(pl.multiple_of(j * tk, tk), tk)
        s = lax.dot_general(qs_ref[...], k_ref[rows, :], _NT, preferred_element_type=F32)
        _softmax_step(s, v_ref[rows, :], m_ref, l_ref, acc_ref)
        return carry

    lax.fori_loop(0, n // tk, body, 0, unroll=True)
    o = acc_ref[...] / l_ref[...]
    for g in range(A_GROUP):
        o_ref[:, g * HEAD_DIM:(g + 1) * HEAD_DIM] = o[g * tq:(g + 1) * tq].astype(o_ref.dtype)


def _attn_a_call(qa, ka, va, merged, tok0, b, n, tq, tk):
    qblk0 = tok0 // tq
    kblk0 = tok0 // n
    nq = n // tq
    rows = A_GROUP * tq
    return pl.pallas_call(
        functools.partial(_attn_a_kernel, tq=tq, tk=tk, n=n),
        grid=(b, A_KV_HEADS, nq),
        in_specs=[
            pl.BlockSpec((tq, A_GROUP * HEAD_DIM), lambda bi, kv, qi: (qblk0 + bi * nq + qi, kv)),
            pl.BlockSpec((n, HEAD_DIM), lambda bi, kv, qi: (kblk0 + bi, kv)),
            pl.BlockSpec((n, HEAD_DIM), lambda bi, kv, qi: (kblk0 + bi, kv)),
            pl.BlockSpec(memory_space=pl.ANY),
        ],
        out_specs=pl.BlockSpec((tq, A_GROUP * HEAD_DIM), lambda bi, kv, qi: (qblk0 + bi * nq + qi, kv)),
        out_shape=jax.ShapeDtypeStruct(merged.shape, merged.dtype),
        input_output_aliases={3: 0},
        scratch_shapes=[
            pltpu.VMEM((rows, HEAD_DIM), BF16),
            pltpu.VMEM((rows, LANES), F32),
            pltpu.VMEM((rows, LANES), F32),
            pltpu.VMEM((rows, HEAD_DIM), F32),
        ],
        compiler_params=_params(("arbitrary", "arbitrary", "arbitrary")),
        name="attn_a",
    )(qa, ka, va, merged)


def _attn_b_kernel(q_ref, k_ref, v_ref, band_ref, lam_ref, gs_ref, _merged_ref, o_ref,
                   qs_ref, m_ref, l_ref, acc_ref, *, t, ck, nchunks, lam_init):
    i = pl.program_id(2)
    c = ck // t
    q = q_ref[...]
    lo = lax.broadcasted_iota(jnp.int32, q.shape, 1) < B_HALF
    zero = jnp.zeros_like(q)
    qs_ref[0:t, :] = jnp.where(lo, q, zero)
    qs_ref[t:2 * t, :] = jnp.where(lo, zero, q)
    m_ref[...] = jnp.full(m_ref.shape, NEG_BIG, F32)
    l_ref[...] = jnp.zeros(l_ref.shape, F32)
    acc_ref[...] = jnp.zeros(acc_ref.shape, F32)

    def body(j, carry):
        rows = pl.ds(pl.multiple_of(j * ck, ck), ck)
        s = lax.dot_general(qs_ref[...], k_ref[rows, :], _NT, preferred_element_type=F32)
        lo_off, hi_off = _band_offsets(c)
        bias = band_ref[0, jnp.clip(c * j - i, lo_off, hi_off) - lo_off]
        for comp in range(2):
            part = slice(comp * t, (comp + 1) * t)
            _softmax_step(s[part] + bias, v_ref[rows, :], m_ref, l_ref, acc_ref, part)
        return carry

    lax.fori_loop(0, nchunks, body, 0, unroll=True)
    o = acc_ref[...] / l_ref[...]
    lq = lam_ref[...]
    lam = (jnp.exp(jnp.sum(lq[0:1] * lq[1:2], axis=-1, keepdims=True))
           - jnp.exp(jnp.sum(lq[2:3] * lq[3:4], axis=-1, keepdims=True)) + lam_init)
    ob = o[0:t] - lam * o[t:2 * t]
    o_ref[...] = (_head_norm(ob, gs_ref[...]) * (1.0 - lam_init)).astype(o_ref.dtype)


def _attn_b_call(qb, kb, vb, band, lam_qk, g_subln, merged, tok0, b, n, t, lam_init):
    qblk0 = tok0 // t
    kblk0 = tok0 // n
    nq = n // t
    ck = band.shape[3]
    return pl.pallas_call(
        functools.partial(_attn_b_kernel, t=t, ck=ck, nchunks=n // ck, lam_init=lam_init),
        grid=(B_HEADS, b, nq),
        in_specs=[
            pl.BlockSpec((t, HEAD_DIM), lambda h, bi, qi: (qblk0 + bi * nq + qi, h)),
            pl.BlockSpec((n, HEAD_DIM), lambda h, bi, qi: (kblk0 + bi, h)),
            pl.BlockSpec((n, HEAD_DIM), lambda h, bi, qi: (kblk0 + bi, h)),
            pl.BlockSpec((1,) + band.shape[1:], lambda h, bi, qi: (h, 0, 0, 0), pipeline_mode=pl.Buffered(1)),
            pl.BlockSpec((4, B_HALF), lambda h, bi, qi: (0, 0)),
            pl.BlockSpec((1, LANES), lambda h, bi, qi: (0, 0)),
            pl.BlockSpec(memory_space=pl.ANY),
        ],
        out_specs=pl.BlockSpec((t, HEAD_DIM), lambda h, bi, qi: (qblk0 + bi * nq + qi, A_Q // HEAD_DIM + h)),
        out_shape=jax.ShapeDtypeStruct(merged.shape, merged.dtype),
        input_output_aliases={6: 0},
        scratch_shapes=[
            pltpu.VMEM((2 * t, HEAD_DIM), BF16),
            pltpu.VMEM((2 * t, LANES), F32),
            pltpu.VMEM((2 * t, LANES), F32),
            pltpu.VMEM((2 * t, HEAD_DIM), F32),
        ],
        compiler_params=_params(("arbitrary", "arbitrary", "arbitrary")),
        name="attn_b",
    )(qb, kb, vb, band, lam_qk, g_subln, merged)


def _t5_bucket(rel):
    nb = NUM_BUCKETS // 2
    ret = (rel > 0).astype(jnp.int32) * nb
    n = jnp.abs(rel)
    max_exact = nb // 2
    nf = jnp.maximum(n, 1).astype(F32)
    large = max_exact + (jnp.log(nf / max_exact) / math.log(MAX_DISTANCE / max_exact)
                         * (nb - max_exact)).astype(jnp.int32)
    large = jnp.minimum(large, nb - 1)
    return ret + jnp.where(n < max_exact, n, large)


_T5_SATURATION = math.ceil((NUM_BUCKETS // 4) * (MAX_DISTANCE / (NUM_BUCKETS // 4)) ** (
    (NUM_BUCKETS // 2 - 1 - NUM_BUCKETS // 4) / (NUM_BUCKETS // 2 - NUM_BUCKETS // 4)))


def _band_offsets(c):
    return -(c + 1), 2


def _band_kernel(vec_ref, o_ref, *, t, ck):
    width = vec_ref.shape[2]
    x = jnp.broadcast_to(vec_ref[0], (t, width))
    o_ref[0, 0] = pltpu.roll(x, width - t + 1, 1, stride=1, stride_axis=0)[:, :ck]


def _bias_tables(rel_bias, t, c):
    assert t + 1 >= _T5_SATURATION
    heads = rel_bias.shape[1]
    lo_off, hi_off = _band_offsets(c)
    nd = hi_off - lo_off + 1
    ck = c * t
    width = (c + 2) * t
    rel = ((jnp.arange(nd, dtype=jnp.int32)[:, None] + lo_off) * t
           + jnp.arange(-(t - 1), width - (t - 1), dtype=jnp.int32)[None, :])
    vec = jnp.take(rel_bias, _t5_bucket(rel), axis=0).astype(F32)
    vec = (vec.transpose(2, 0, 1) * LOG2E).reshape(heads * nd, 1, width)
    return pl.pallas_call(
        functools.partial(_band_kernel, t=t, ck=ck),
        grid=(heads, nd),
        in_specs=[pl.BlockSpec((1, 1, width), lambda h, d: (h * nd + d, 0, 0))],
        out_specs=pl.BlockSpec((1, 1, t, ck), lambda h, d: (h, d, 0, 0)),
        out_shape=jax.ShapeDtypeStruct((heads, nd, t, ck), F32),
        compiler_params=_params(("arbitrary", "arbitrary")),
        name="band",
    )(vec)


_DFT_INNER = 64


def _dft_stage_tables(n):
    n2 = min(_DFT_INNER, n // 8)
    n1 = n // n2
    i1 = jnp.arange(n1, dtype=jnp.int32)
    ang1 = ((i1[:, None] * i1[None, :]) % n1).astype(F32) * (2.0 * math.pi / n1)
    c1 = jnp.cos(ang1) * n ** -0.5
    s1 = jnp.sin(ang1) * n ** -0.5
    stage1 = jnp.concatenate([jnp.concatenate([c1, -s1], axis=1), jnp.concatenate([s1, c1], axis=1)], axis=0)
    i2 = jnp.arange(n2, dtype=jnp.int32)
    k = i1[:, None, None] + n1 * i2[None, :, None]
    ang2 = ((k * i2[None, None, :]) % n).astype(F32) * (2.0 * math.pi / n)
    stage2 = jnp.concatenate([jnp.cos(ang2), -jnp.sin(ang2)], axis=2)
    return stage1.astype(BF16), stage2.astype(BF16)


def _fourier_kernel(p_ref, q_ref, s1_ref, s2_ref, w_ref, _merged_ref, o_ref, y_ref, f_ref):
    n1, n2 = s2_ref.shape[0], s2_ref.shape[1]
    lanes = p_ref.shape[1]
    cols = 4
    stage1 = s1_ref[...]
    for t2 in range(0, n2, cols):
        zr = [p_ref[pl.ds(t2 + j, n1, stride=n2), :] for j in range(cols)]
        zi = [q_ref[pl.ds(t2 + j, n1, stride=n2), :] for j in range(cols)]
        z = jnp.concatenate([jnp.concatenate(zr, axis=1), jnp.concatenate(zi, axis=1)], axis=0)
        y = jnp.dot(stage1, z.astype(BF16), preferred_element_type=F32)
        for j in range(cols):
            y_ref[0, pl.ds(t2 + j, n1, stride=n2), :] = y[:n1, j * lanes:(j + 1) * lanes]
            y_ref[1, pl.ds(t2 + j, n1, stride=n2), :] = y[n1:, j * lanes:(j + 1) * lanes]
    for k1 in range(n1):
        blk = jnp.concatenate([y_ref[0, k1 * n2:(k1 + 1) * n2, :], y_ref[1, k1 * n2:(k1 + 1) * n2, :]], axis=0)
        f_ref[pl.ds(k1, n2, stride=n1), :] = jnp.dot(s2_ref[k1], blk.astype(BF16), preferred_element_type=F32)
    o_ref[...] = jnp.dot(f_ref[...].astype(BF16), w_ref[0], preferred_element_type=F32).astype(o_ref.dtype)


def _fourier_call(stage1, stage2, p, q, wf, merged, tok0, b, n):
    kblk0 = tok0 // n
    col0 = (A_Q + B_WIDTH) // C_GROUP_W
    return pl.pallas_call(
        _fourier_kernel,
        grid=(b, C_GROUPS),
        in_specs=[
            pl.BlockSpec((n, C_GROUP_W), lambda bi, g: (kblk0 + bi, g)),
            pl.BlockSpec((n, C_GROUP_W), lambda bi, g: (kblk0 + bi, g)),
            pl.BlockSpec(stage1.shape, lambda bi, g: (0, 0)),
            pl.BlockSpec(stage2.shape, lambda bi, g: (0, 0, 0)),
            pl.BlockSpec((1, C_GROUP_W, C_GROUP_W), lambda bi, g: (g, 0, 0)),
            pl.BlockSpec(memory_space=pl.ANY),
        ],
        out_specs=pl.BlockSpec((n, C_GROUP_W), lambda bi, g: (kblk0 + bi, col0 + g)),
        out_shape=jax.ShapeDtypeStruct(merged.shape, merged.dtype),
        input_output_aliases={5: 0},
        scratch_shapes=[pltpu.VMEM((2, n, C_GROUP_W), F32), pltpu.VMEM((n, C_GROUP_W), F32)],
        compiler_params=_params(("arbitrary", "arbitrary")),
        name="fourier",
    )(p, q, stage1, stage2, wf, merged)


def _out_kernel(xp_ref, xs_ref, *rest, nt0):
    _split_apply(xp_ref, xs_ref, nt0, _out_body, *rest)


def _out_body(x_ref, mix_ref, mod_ref, w_ref, gn_ref, wr_ref, x1_ref, hn_ref, ids_ref):
    mix = jnp.dot(mix_ref[...], w_ref[...], preferred_element_type=F32)
    x1 = x_ref[...] + mod_ref[0, 2:3, :] * mix
    x1_ref[...] = x1
    y = x1 * lax.rsqrt(jnp.mean(x1 * x1, axis=-1, keepdims=True) + EPS) * gn_ref[...]
    hn = y * (1.0 + mod_ref[0, 4:5, :]) + mod_ref[0, 3:4, :]
    d = hn.shape[1]
    hn_ref[:, :d] = hn

    logits = jnp.dot(hn.astype(BF16), wr_ref[...], preferred_element_type=F32)
    lane = lax.broadcasted_iota(jnp.int32, logits.shape, 1)
    big = jnp.int32(LANES)
    is_g = lane < N_GROUPS
    gmax = jnp.max(jnp.where(is_g, logits, -jnp.inf), axis=-1, keepdims=True)
    g_sel = jnp.min(jnp.where(jnp.logical_and(is_g, logits == gmax), lane, big), axis=-1, keepdims=True)
    g_w = 1.0 / jnp.sum(jnp.where(is_g, jnp.exp(logits - gmax), 0.0), axis=-1, keepdims=True)
    lo_lane = N_GROUPS + g_sel * EXPERTS_PER_GROUP
    in_g = jnp.logical_and(lane >= lo_lane, lane < lo_lane + EXPERTS_PER_GROUP)
    v0 = jnp.max(jnp.where(in_g, logits, -jnp.inf), axis=-1, keepdims=True)
    i0 = jnp.min(jnp.where(jnp.logical_and(in_g, logits == v0), lane, big), axis=-1, keepdims=True)
    rest = jnp.logical_and(in_g, lane != i0)
    v1 = jnp.max(jnp.where(rest, logits, -jnp.inf), axis=-1, keepdims=True)
    i1 = jnp.min(jnp.where(jnp.logical_and(rest, logits == v1), lane, big), axis=-1, keepdims=True)
    e1 = jnp.exp(v1 - v0)
    w0 = g_w / (1.0 + e1)
    w1 = g_w * e1 / (1.0 + e1)
    swap = i1 < i0
    ea = jnp.where(swap, i1, i0) - N_GROUPS
    eb = jnp.where(swap, i0, i1) - N_GROUPS
    wa = jnp.where(swap, w1, w0)
    wb = jnp.where(swap, w0, w1)
    ml = lax.broadcasted_iota(jnp.int32, ids_ref.shape, 1)
    ids_ref[...] = jnp.where(ml == 0, ea, jnp.where(ml == 1, eb, 0))
    hn_ref[:, d:] = jnp.where(lane == 0, wa, jnp.where(lane == 1, wb, 0.0))


def _out_call(lay, mix, xp, xs, mod3, w_out, gn, w_router, tm):
    t, d = lay.t, xp.shape[1]
    row = lambda i: (i, 0)
    const = lambda i: (0, 0)
    meta = LANES
    return pl.pallas_call(
        functools.partial(_out_kernel, nt0=lay.t0 // tm),
        grid=(t // tm,),
        in_specs=lay.split_specs(tm, d) + [
            pl.BlockSpec((tm, mix.shape[1]), row),
            pl.BlockSpec((1, 6, d), lambda i: (lay.batch_of_tile(i, tm), 0, 0)),
            pl.BlockSpec((d, d), const),
            pl.BlockSpec((1, d), const),
            pl.BlockSpec((d, LANES), const),
        ],
        out_specs=[pl.BlockSpec((tm, d), row), pl.BlockSpec((tm, d + LANES), row),
                   pl.BlockSpec((tm, meta), row)],
        out_shape=[jax.ShapeDtypeStruct((t, d), F32), jax.ShapeDtypeStruct((t, d + LANES), F32),
                   jax.ShapeDtypeStruct((t, meta), jnp.int32)],
        compiler_params=_params(("arbitrary",)),
        name="out_proj",
    )(xp, xs, mix, mod3, w_out, gn, w_router)


def _bucket_onehot(ids_ref):
    ids = ids_ref[...]
    ea = ids[:, 0:1]
    eb = ids[:, 1:2]
    la = ea & (EXPERTS_PER_GROUP - 1)
    lb = eb & (EXPERTS_PER_GROUP - 1)
    pair = ((la * (2 * EXPERTS_PER_GROUP - 1 - la)) >> 1) + (lb - la - 1)
    bucket = (ea >> (EXPERTS_PER_GROUP.bit_length() - 1)) * len(_PAIRS) + pair
    lane = lax.broadcasted_iota(jnp.int32, (ids.shape[0], LANES), 1)
    return lane == bucket


def _positions_kernel(ids_ref, pos_ref, counts_ref, run_ref, base_ref, start_ref, earlier_ref, *, tm):
    p = pl.program_id(0)
    i = pl.program_id(1)
    tt = ids_ref.shape[0]
    onehot = _bucket_onehot(ids_ref)

    @pl.when(jnp.logical_and(p == 0, i == 0))
    def _():
        run_ref[...] = jnp.zeros(run_ref.shape, F32)

    @pl.when(p == 0)
    def _():
        base_ref[pl.ds(i, 1), :] = run_ref[...]
        run_ref[...] += jnp.sum(jnp.where(onehot, 1.0, 0.0), axis=0, keepdims=True)

    @pl.when(jnp.logical_and(p == 1, i == 0))
    def _():
        counts = run_ref[...]
        counts_ref[...] = counts
        tiles = jnp.floor((counts + (tm - 1)) * (1.0 / tm)).astype(BF16)
        a = lax.broadcasted_iota(jnp.int32, (LANES, LANES), 0)
        b = lax.broadcasted_iota(jnp.int32, (LANES, LANES), 1)
        before = jnp.where(a < b, 1.0, 0.0).astype(BF16)
        start_ref[...] = jnp.dot(tiles, before, preferred_element_type=F32) * tm
        r = lax.broadcasted_iota(jnp.int32, (tt, tt), 0)
        c = lax.broadcasted_iota(jnp.int32, (tt, tt), 1)
        earlier_ref[...] = jnp.where(c < r, 1.0, 0.0).astype(BF16)

    @pl.when(p == 1)
    def _():
        oh = jnp.where(onehot, 1.0, 0.0)
        rank = jnp.dot(earlier_ref[...], oh.astype(BF16), preferred_element_type=F32)
        val = start_ref[...] + base_ref[pl.ds(i, 1), :] + rank
        pos = jnp.sum(oh * val, axis=1, keepdims=True)
        pos_ref[...] = jnp.broadcast_to(pos, pos_ref.shape).astype(jnp.int32)


def _positions_call(ids, tm, tt):
    t, meta = ids.shape
    nt = t // tt
    assert t // tm + 1 <= 256 and tm & (tm - 1) == 0
    pos, counts = pl.pallas_call(
        functools.partial(_positions_kernel, tm=tm),
        grid=(2, nt),
        in_specs=[pl.BlockSpec((tt, meta), lambda p, i: (i, 0))],
        out_specs=[pl.BlockSpec((tt, meta), lambda p, i: (i * p, 0)),
                   pl.BlockSpec((1, LANES), lambda p, i: (0, 0))],
        out_shape=[jax.ShapeDtypeStruct((t, meta), jnp.int32), jax.ShapeDtypeStruct((1, LANES), F32)],
        scratch_shapes=[pltpu.VMEM((1, LANES), F32), pltpu.VMEM((nt, LANES), F32), pltpu.VMEM((1, LANES), F32),
                        pltpu.VMEM((tt, tt), BF16)],
        compiler_params=_params(("arbitrary", "arbitrary")),
        name="positions",
    )(ids)
    return pos[:, 0], counts[0, :N_BUCKETS].astype(jnp.int32)


def _route_plan(ids, tm, tt):
    t = ids.shape[0]
    pos, counts = _positions_call(ids, tm, tt)
    tiles = (counts + tm - 1) // tm
    tile_end = jnp.cumsum(tiles)
    tile_start = tile_end - tiles
    assert t % tm == 0
    n_tiles = t // tm + N_BUCKETS
    src = jnp.zeros((n_tiles * tm,), jnp.int32).at[pos].set(jnp.arange(t, dtype=jnp.int32))
    tile_ids = jnp.arange(n_tiles, dtype=jnp.int32)
    used = tile_end[-1]
    last = jnp.minimum(tile_ids, used - 1)
    tile_bucket = jnp.sum((tile_end[None, :] <= last[:, None]).astype(jnp.int32), axis=1)
    pairs = np.array(_PAIRS, np.int32)
    tgrp = tile_bucket // len(_PAIRS)
    tpair = tile_bucket % len(_PAIRS)
    tile_ea = tgrp * EXPERTS_PER_GROUP + jnp.take(jnp.asarray(pairs[:, 0]), tpair)
    tile_eb = tgrp * EXPERTS_PER_GROUP + jnp.take(jnp.asarray(pairs[:, 1]), tpair)
    rows_left = jnp.take(counts, tile_bucket) - (tile_ids - jnp.take(tile_start, tile_bucket)) * tm
    tile_rows = jnp.where(tile_ids < used, jnp.clip(rows_left, 0, tm), 0).astype(jnp.int32)
    return pos.astype(jnp.int32), src, tile_ea.astype(jnp.int32), tile_eb.astype(jnp.int32), tile_rows, used


def _row_copy(src_hbm, row, buf, slot, r, sem):
    return pltpu.make_async_copy(src_hbm.at[pl.ds(row, 1), :], buf.at[slot, pl.ds(r, 1), :], sem.at[slot])


def _gather_start(idx_ref, base, rows, src_hbm, buf, slot, sem, first=0):
    for r in range(first, first + rows):
        _row_copy(src_hbm, idx_ref[base + r], buf, slot, r, sem).start()


def _gather_wait(rows, src_hbm, buf, slot, sem):
    pltpu.make_async_copy(src_hbm.at[pl.ds(0, rows), :], buf.at[slot], sem.at[slot]).wait()


def _moe_kernel(src_ref, ea_ref, eb_ref, rows_ref, h_hbm, w1a_ref, w3a_ref, w2a_ref, w1b_ref, w3b_ref, w2b_ref,
                y_ref, buf, sem, *, tm):
    i = pl.program_id(0)
    slot = i % 2
    used = rows_ref[i] > 0
    d = y_ref.shape[1]

    @pl.when(i == 0)
    def _():
        _gather_start(src_ref, 0, tm, h_hbm, buf, 0, sem)

    @pl.when(jnp.logical_or(i == 0, rows_ref[jnp.maximum(i - 1, 0)] > 0))
    def _():
        _gather_wait(tm, h_hbm, buf, slot, sem)

    @pl.when(jnp.logical_not(used))
    def _():
        y_ref[...] = jnp.zeros(y_ref.shape, y_ref.dtype)

    @pl.when(used)
    def _():
        h = buf[slot, :, :d].astype(BF16)
        w = buf[slot, :, d:]
        part = tm // 4

        def prefetch(k):
            _gather_start(src_ref, (i + 1) * tm, part, h_hbm, buf, 1 - slot, sem, first=k * part)

        def expert(w1_ref, w3_ref, w2_ref, k):
            g = jnp.dot(h, w1_ref[0], preferred_element_type=F32)
            prefetch(k)
            a = jax.nn.silu(g) * jnp.dot(h, w3_ref[0], preferred_element_type=F32)
            prefetch(k + 1)
            return jnp.dot(a.astype(BF16), w2_ref[0], preferred_element_type=F32)

        y_ref[...] = (expert(w1a_ref, w3a_ref, w2a_ref, 0) * w[:, 0:1]
                      + expert(w1b_ref, w3b_ref, w2b_ref, 2) * w[:, 1:2])


def _moe_call(src, tile_ea, tile_eb, tile_rows, hn, w1, w3, w2, tm):
    n_tiles = tile_ea.shape[0]
    d = w1.shape[1]
    f = w1.shape[2]
    wa = lambda i, src, ea, eb, rows: (ea[i], 0, 0)
    wb = lambda i, src, ea, eb, rows: (eb[i], 0, 0)
    grid_spec = pltpu.PrefetchScalarGridSpec(
        num_scalar_prefetch=4,
        grid=(n_tiles,),
        in_specs=[
            pl.BlockSpec(memory_space=pl.ANY),
            pl.BlockSpec((1, d, f), wa), pl.BlockSpec((1, d, f), wa), pl.BlockSpec((1, f, d), wa),
            pl.BlockSpec((1, d, f), wb), pl.BlockSpec((1, d, f), wb), pl.BlockSpec((1, f, d), wb),
        ],
        out_specs=pl.BlockSpec((tm, d), lambda i, *_: (i, 0)),
        scratch_shapes=[pltpu.VMEM((2, tm, hn.shape[1]), F32), pltpu.SemaphoreType.DMA((2,))],
    )
    return pl.pallas_call(
        functools.partial(_moe_kernel, tm=tm),
        grid_spec=grid_spec,
        out_shape=jax.ShapeDtypeStruct((n_tiles * tm, d), F32),
        compiler_params=_params(("arbitrary",)),
        name="moe",
    )(src, tile_ea, tile_eb, tile_rows, hn, w1, w3, w2, w1, w3, w2)


def _combine_kernel(pos_ref, y_hbm, x1_ref, mod_ref, op_ref, os_ref, buf, sem, *, tm, nt0):
    i = pl.program_id(0)
    nt = pl.num_programs(0)
    slot = i % 2

    @pl.when(i == 0)
    def _():
        _gather_start(pos_ref, 0, tm, y_hbm, buf, 0, sem)

    _gather_wait(tm, y_hbm, buf, slot, sem)

    @pl.when(i + 1 < nt)
    def _():
        _gather_start(pos_ref, (i + 1) * tm, tm, y_hbm, buf, 1 - slot, sem)

    out = x1_ref[...] + mod_ref[0, 5:6, :] * buf[slot]

    @pl.when(i < nt0)
    def _():
        op_ref[...] = out

    @pl.when(i >= nt0)
    def _():
        os_ref[...] = out


def _combine_call(lay, pos, y_sorted, x1, mod3, tm):
    t, d = x1.shape
    grid_spec = pltpu.PrefetchScalarGridSpec(
        num_scalar_prefetch=1,
        grid=(t // tm,),
        in_specs=[
            pl.BlockSpec(memory_space=pl.ANY),
            pl.BlockSpec((tm, d), lambda i, pos: (i, 0)),
            pl.BlockSpec((1, 6, d), lambda i, pos: (lay.batch_of_tile(i, tm), 0, 0)),
        ],
        out_specs=lay.split_specs(tm, d),
        scratch_shapes=[pltpu.VMEM((2, tm, d), F32), pltpu.SemaphoreType.DMA((2,))],
    )
    return pl.pallas_call(
        functools.partial(_combine_kernel, tm=tm, nt0=lay.t0 // tm),
        grid_spec=grid_spec,
        out_shape=lay.split_shapes(d, F32),
        compiler_params=_params(("arbitrary",)),
        name="combine",
    )(pos, y_sorted, x1, mod3)


def _rope_tables(n):
    n_rows = n // GRID_W
    rows = jnp.repeat(jnp.arange(n_rows), GRID_W).astype(F32)
    cols = jnp.tile(jnp.arange(GRID_W), n_rows).astype(F32)
    half = HEAD_DIM // 2
    inv = ROPE_THETA ** (-jnp.arange(0, half, 2, dtype=F32) / half)
    ang = jnp.concatenate([rows[:, None] * inv, cols[:, None] * inv], axis=-1)
    sign = jnp.tile(jnp.array([-1.0, 1.0], F32), half)
    return jnp.repeat(jnp.cos(ang), 2, axis=-1), jnp.repeat(jnp.sin(ang), 2, axis=-1) * sign


def _lambda_init(layer_idx):
    return 0.8 - 0.6 * math.exp(-0.3 * layer_idx)


def _tiled_gain(g):
    return jnp.tile(g, LANES // g.shape[-1]).reshape(1, LANES).astype(F32)


def kernel(x_prompt, x_sample, c_prompt, c_sample, rel_bias, w_ada, b_ada, g_norm_mix, w_in, g_qa, g_ka,
           g_qb, g_kb, lam_qk, g_subln, w_fourier, w_out, g_norm_ffn, w_group, w_expert, w1, w3, w2):
    b0, n0, d = x_prompt.shape
    b1, n1, _ = x_sample.shape
    depth = w_in.shape[0]
    lay = _Layout(b0, n0, b1, n1)
    trunks = ((0, b0, n0), (lay.t0, b1, n1))
    n_max = max(n0, n1)

    tm = _pick(math.gcd(n0, n1), 512)
    tq_a = _pick(math.gcd(n0, n1), 256)
    tk_a = _pick(math.gcd(n0, n1), 2048)
    t_b = _pick(math.gcd(n0, n1), 512)
    tm_moe = 256

    xp = x_prompt.reshape(b0 * n0, d)
    xs = x_sample.reshape(b1 * n1, d)
    nb = b0 + b1
    bp = -(-nb // 8) * 8
    c_all = jnp.zeros((bp, d), F32).at[:nb].set(jnp.concatenate([c_prompt, c_sample], axis=0))
    mod = _ada_call(c_all, w_ada, b_ada).reshape(depth, bp, 6, d)

    cos_e, sin_e = _rope_tables(n_max)
    chunk_ratio = {n: min(2, n // t_b) for n in (n0, n1)}
    bands = {c: _bias_tables(rel_bias, t_b, c) for c in sorted(set(chunk_ratio.values()))}
    dft = {n: _dft_stage_tables(n) for n in sorted({n0, n1})}
    cidx = jnp.arange(C_GROUP_W, dtype=jnp.int32)
    ang_c = ((cidx[:, None] * cidx[None, :]) % C_GROUP_W).astype(F32) * (2.0 * math.pi / C_GROUP_W)
    dft_c = (jnp.concatenate([jnp.cos(ang_c), jnp.sin(ang_c)], axis=1) * C_GROUP_W ** -0.5).astype(BF16)

    for l in range(depth):
        mod3 = mod[l]
        lam_init = _lambda_init(l)
        qa, ka, va, qb, kb, vb, p, q = _proj_call(
            lay, xp, xs, mod3, g_norm_mix[l].reshape(1, d), w_in[l].astype(BF16),
            _tiled_gain(g_qa[l]), _tiled_gain(g_ka[l]), _tiled_gain(g_qb[l]), _tiled_gain(g_kb[l]),
            cos_e, sin_e, dft_c, tm)

        wf = w_fourier[l].astype(BF16)
        mix = jnp.zeros((lay.t, A_Q + B_WIDTH + C_WIDTH), BF16)
        for tok0, b, n in trunks:
            mix = _attn_a_call(qa, ka, va, mix, tok0, b, n, tq_a, min(tk_a, max(n // 2, LANES)))
            mix = _attn_b_call(qb, kb, vb, bands[chunk_ratio[n]], lam_qk[l], _tiled_gain(g_subln[l]), mix,
                               tok0, b, n, t_b, lam_init)
            mix = _fourier_call(dft[n][0], dft[n][1], p, q, wf, mix, tok0, b, n)

        w_router = jnp.zeros((d, LANES), F32).at[:, :N_GROUPS].set(w_group[l])
        w_router = w_router.at[:, N_GROUPS:N_GROUPS + N_EXPERTS].set(w_expert[l]).astype(BF16)
        x1, hn2, ids = _out_call(lay, mix, xp, xs, mod3, w_out[l].astype(BF16),
                                 g_norm_ffn[l].reshape(1, d), w_router, tm)

        pos, src, tile_ea, tile_eb, tile_rows, _ = _route_plan(ids, tm_moe, _pick(math.gcd(n0, n1), 2048))
        y_sorted = _moe_call(src, tile_ea, tile_eb, tile_rows, hn2, w1[l].astype(BF16), w3[l].astype(BF16),
                             w2[l].astype(BF16), tm_moe)
        xp, xs = _combine_call(lay, pos, y_sorted, x1, mod3, tm)

    return (xp.reshape(b0, n0, d), xs.reshape(b1, n1, d))
```

```python
import functools
import math

import jax
import jax.numpy as jnp
import numpy as np
from jax import lax
from jax.experimental import pallas as pl
from jax.experimental.pallas import tpu as pltpu

F32 = jnp.float32
BF16 = jnp.bfloat16

D_MODEL = 2048
HEAD_DIM = 128
A_HEADS = 8
A_KV_HEADS = 2
A_GROUP = A_HEADS // A_KV_HEADS
A_Q = A_HEADS * HEAD_DIM
A_KV = A_KV_HEADS * HEAD_DIM
B_HEADS = 4
B_HALF = HEAD_DIM // 2
B_WIDTH = B_HEADS * HEAD_DIM
C_WIDTH = 512
C_GROUPS = 4
C_GROUP_W = 128
IN_WIDTH = A_Q + 2 * A_KV + 3 * B_WIDTH + C_WIDTH
GRID_W = 64
ROPE_THETA = 10000.0
NUM_BUCKETS = 32
MAX_DISTANCE = 128
N_GROUPS = 4
EXPERTS_PER_GROUP = 4
N_EXPERTS = 16
D_FF_EXPERT = 512
EPS = 1e-6
LOG2E = 1.4426950408889634

_PAIRS = ((0, 1), (0, 2), (0, 3), (1, 2), (1, 3), (2, 3))
N_BUCKETS = N_GROUPS * len(_PAIRS)

V7X_VMEM_BYTES = 64 * 1024 * 1024
VMEM_LIMIT = V7X_VMEM_BYTES - 8 * 1024 * 1024
LANES = 128
NEG_BIG = -1e30


def _params(sem, **kw):
    return pltpu.CompilerParams(dimension_semantics=sem, vmem_limit_bytes=VMEM_LIMIT, **kw)

def _pick(total, pref):
    t = min(pref, total)
    while total % t:
        t //= 2
    return t


def _ada_kernel(c_ref, w_ref, b_ref, o_ref):
    h = jax.nn.silu(c_ref[...]).astype(BF16)
    o_ref[0] = jnp.dot(h, w_ref[0].astype(BF16), preferred_element_type=F32) + b_ref[0]


def _ada_call(c_all, w_ada, b_ada):
    depth, d, e = w_ada.shape
    bp = c_all.shape[0]
    tn = _pick(e, 1024)
    return pl.pallas_call(
        _ada_kernel,
        grid=(depth, e // tn),
        in_specs=[
            pl.BlockSpec((bp, d), lambda l, j: (0, 0)),
            pl.BlockSpec((1, d, tn), lambda l, j: (l, 0, j)),
            pl.BlockSpec((1, 1, tn), lambda l, j: (l, 0, j)),
        ],
        out_specs=pl.BlockSpec((1, bp, tn), lambda l, j: (l, 0, j)),
        out_shape=jax.ShapeDtypeStruct((depth, bp, e), F32),
        compiler_params=_params(("arbitrary", "arbitrary")),
        name="ada",
    )(c_all, w_ada, b_ada.reshape(depth, 1, e))


class _Layout:
    def __init__(self, b0, n0, b1, n1):
        self.b = (b0, b1)
        self.n = (n0, n1)
        self.t0 = b0 * n0
        self.t = b0 * n0 + b1 * n1

    def batch_of_tile(self, i, tm):
        tok = i * tm
        return jnp.where(tok < self.t0, tok // self.n[0], self.b[0] + (tok - self.t0) // self.n[1])

    def pos_block_of_tile(self, i, tm):
        tok = i * tm
        pos = jnp.where(tok < self.t0, tok % self.n[0], (tok - self.t0) % self.n[1])
        return pos // tm

    def split_specs(self, tm, d):
        nt0 = self.t0 // tm
        return [pl.BlockSpec((tm, d), lambda i, *_: (jnp.minimum(i, nt0 - 1), 0)),
                pl.BlockSpec((tm, d), lambda i, *_: (jnp.maximum(i - nt0, 0), 0))]

    def split_shapes(self, d, dtype):
        return [jax.ShapeDtypeStruct((self.t0, d), dtype), jax.ShapeDtypeStruct((self.t - self.t0, d), dtype)]


def _split_apply(xp_ref, xs_ref, nt0, body, *rest):
    i = pl.program_id(0)

    @pl.when(i < nt0)
    def _():
        body(xp_ref, *rest)

    @pl.when(i >= nt0)
    def _():
        body(xs_ref, *rest)


def _head_norm(z, g):
    return z * lax.rsqrt(jnp.mean(z * z, axis=-1, keepdims=True) + EPS) * g


def _half_norm(z, g, lo):
    zz = z * z
    s_lo = jnp.sum(jnp.where(lo, zz, 0.0), axis=-1, keepdims=True)
    s_hi = jnp.sum(jnp.where(lo, 0.0, zz), axis=-1, keepdims=True)
    inv = jnp.where(lo, lax.rsqrt(s_lo / B_HALF + EPS), lax.rsqrt(s_hi / B_HALF + EPS))
    return z * inv * g


def _rope(z, c, s_signed, even):
    partner = jnp.where(even, pltpu.roll(z, LANES - 1, 1), pltpu.roll(z, 1, 1))
    return z * c + partner * s_signed


def _proj_kernel(xp_ref, xs_ref, *rest, nt0):
    _split_apply(xp_ref, xs_ref, nt0, _proj_body, *rest)


def _proj_body(x_ref, mod_ref, gn_ref, w_ref, gqa_ref, gka_ref, gqb_ref, gkb_ref, cos_ref, sin_ref,
               dft_ref, qa_ref, ka_ref, va_ref, qb_ref, kb_ref, vb_ref, p_ref, q_ref):
    x = x_ref[...]
    y = x * lax.rsqrt(jnp.mean(x * x, axis=-1, keepdims=True) + EPS) * gn_ref[...]
    hn = (y * (1.0 + mod_ref[0, 1:2, :]) + mod_ref[0, 0:1, :]).astype(BF16)

    def seg(a, b):
        return jnp.dot(hn, w_ref[:, a:b], preferred_element_type=F32)

    tm = x.shape[0]
    lane = lax.broadcasted_iota(jnp.int32, (tm, LANES), 1)
    even = (lane % 2) == 0
    lo = lane < B_HALF
    cos = cos_ref[...]
    sin = sin_ref[...]

    scale_a = HEAD_DIM ** -0.5 * LOG2E
    z = seg(0, A_Q)
    for h in range(A_HEADS):
        sl = slice(h * HEAD_DIM, (h + 1) * HEAD_DIM)
        qa_ref[:, sl] = (_rope(_head_norm(z[:, sl], gqa_ref[...]), cos, sin, even) * scale_a).astype(BF16)
    off = A_Q
    z = seg(off, off + A_KV)
    for h in range(A_KV_HEADS):
        sl = slice(h * HEAD_DIM, (h + 1) * HEAD_DIM)
        ka_ref[:, sl] = _rope(_head_norm(z[:, sl], gka_ref[...]), cos, sin, even).astype(BF16)
    off += A_KV
    va_ref[...] = seg(off, off + A_KV).astype(BF16)
    off += A_KV

    scale_b = B_HALF ** -0.5 * LOG2E
    z = seg(off, off + B_WIDTH)
    for h in range(B_HEADS):
        sl = slice(h * HEAD_DIM, (h + 1) * HEAD_DIM)
        qb_ref[:, sl] = (_half_norm(z[:, sl], gqb_ref[...], lo) * scale_b).astype(BF16)
    off += B_WIDTH
    z = seg(off, off + B_WIDTH)
    for h in range(B_HEADS):
        sl = slice(h * HEAD_DIM, (h + 1) * HEAD_DIM)
        kb_ref[:, sl] = _half_norm(z[:, sl], gkb_ref[...], lo).astype(BF16)
    off += B_WIDTH
    vb_ref[...] = seg(off, off + B_WIDTH).astype(BF16)
    off += B_WIDTH

    z = seg(off, off + C_WIDTH).astype(BF16)
    for g in range(C_GROUPS):
        sl = slice(g * C_GROUP_W, (g + 1) * C_GROUP_W)
        pq = jnp.dot(z[:, sl], dft_ref[...], preferred_element_type=F32)
        p_ref[:, sl] = pq[:, :C_GROUP_W]
        q_ref[:, sl] = pq[:, C_GROUP_W:]


def _proj_call(lay, xp, xs, mod3, gn, w_in, gqa, gka, gqb, gkb, cos_e, sin_e, dft_c, tm):
    t, d = lay.t, xp.shape[1]
    row = lambda i: (i, 0)
    const = lambda i: (0, 0)
    widths = (A_Q, A_KV, A_KV, B_WIDTH, B_WIDTH, B_WIDTH, C_WIDTH, C_WIDTH)
    return pl.pallas_call(
        functools.partial(_proj_kernel, nt0=lay.t0 // tm),
        grid=(t // tm,),
        in_specs=lay.split_specs(tm, d) + [
            pl.BlockSpec((1, 6, d), lambda i: (lay.batch_of_tile(i, tm), 0, 0)),
            pl.BlockSpec((1, d), const),
            pl.BlockSpec((d, IN_WIDTH), const),
            pl.BlockSpec((1, LANES), const),
            pl.BlockSpec((1, LANES), const),
            pl.BlockSpec((1, LANES), const),
            pl.BlockSpec((1, LANES), const),
            pl.BlockSpec((tm, LANES), lambda i: (lay.pos_block_of_tile(i, tm), 0)),
            pl.BlockSpec((tm, LANES), lambda i: (lay.pos_block_of_tile(i, tm), 0)),
            pl.BlockSpec((C_GROUP_W, 2 * C_GROUP_W), const),
        ],
        out_specs=[pl.BlockSpec((tm, w), row) for w in widths],
        out_shape=[jax.ShapeDtypeStruct((t, w), F32 if i >= 6 else BF16) for i, w in enumerate(widths)],
        compiler_params=_params(("arbitrary",)),
        name="proj",
    )(xp, xs, mod3, gn, w_in, gqa, gka, gqb, gkb, cos_e, sin_e, dft_c)


def _softmax_step(s, vc, m_ref, l_ref, acc_ref, rows=slice(None)):
    m_prev = m_ref[rows, :]
    m_new = jnp.maximum(m_prev, jnp.max(s, axis=1, keepdims=True))
    alpha = jnp.exp2(m_prev - m_new)
    p = jnp.exp2(s - jnp.tile(m_new, (1, s.shape[1] // LANES)))
    l_ref[rows, :] = alpha * l_ref[rows, :] + jnp.sum(p, axis=1, keepdims=True)
    acc_ref[rows, :] = alpha * acc_ref[rows, :] + jnp.dot(p.astype(BF16), vc, preferred_element_type=F32)
    m_ref[rows, :] = m_new


_NT = (((1,), (1,)), ((), ()))


def _attn_a_kernel(q_ref, k_ref, v_ref, _merged_ref, o_ref, qs_ref, m_ref, l_ref, acc_ref, *, tq, tk, n):
    for g in range(A_GROUP):
        qs_ref[g * tq:(g + 1) * tq, :] = q_ref[:, g * HEAD_DIM:(g + 1) * HEAD_DIM]
    m_ref[...] = jnp.full(m_ref.shape, NEG_BIG, F32)
    l_ref[...] = jnp.zeros(l_ref.shape, F32)
    acc_ref[...] = jnp.zeros(acc_ref.shape, F32)

    def body(j, carry):
        rows = pl.ds(pl.multiple_of(j * tk, tk), tk)
        s = lax.dot_general(qs_ref[...], k_ref[rows, :], _NT, preferred_element_type=F32)
        _softmax_step(s, v_ref[rows, :], m_ref, l_ref, acc_ref)
        return carry

    lax.fori_loop(0, n // tk, body, 0, unroll=True)
    o = acc_ref[...] / l_ref[...]
    for g in range(A_GROUP):
        o_ref[:, g * HEAD_DIM:(g + 1) * HEAD_DIM] = o[g * tq:(g + 1) * tq].astype(o_ref.dtype)


def _attn_a_call(qa, ka, va, merged, tok0, b, n, tq, tk):
    qblk0 = tok0 // tq
    kblk0 = tok0 // n
    nq = n // tq
    rows = A_GROUP * tq
    return pl.pallas_call(
        functools.partial(_attn_a_kernel, tq=tq, tk=tk, n=n),
        grid=(b, A_KV_HEADS, nq),
        in_specs=[
            pl.BlockSpec((tq, A_GROUP * HEAD_DIM), lambda bi, kv, qi: (qblk0 + bi * nq + qi, kv)),
            pl.BlockSpec((n, HEAD_DIM), lambda bi, kv, qi: (kblk0 + bi, kv)),
            pl.BlockSpec((n, HEAD_DIM), lambda bi, kv, qi: (kblk0 + bi, kv)),
            pl.BlockSpec(memory_space=pl.ANY),
        ],
        out_specs=pl.BlockSpec((tq, A_GROUP * HEAD_DIM), lambda bi, kv, qi: (qblk0 + bi * nq + qi, kv)),
        out_shape=jax.ShapeDtypeStruct(merged.shape, merged.dtype),
        input_output_aliases={3: 0},
        scratch_shapes=[
            pltpu.VMEM((rows, HEAD_DIM), BF16),
            pltpu.VMEM((rows, LANES), F32),
            pltpu.VMEM((rows, LANES), F32),
            pltpu.VMEM((rows, HEAD_DIM), F32),
        ],
        compiler_params=_params(("arbitrary", "arbitrary", "arbitrary")),
        name="attn_a",
    )(qa, ka, va, merged)


def _attn_b_kernel(q_ref, k_ref, v_ref, band_ref, lam_ref, gs_ref, _merged_ref, o_ref,
                   qs_ref, m_ref, l_ref, acc_ref, *, t, ck, nchunks, lam_init):
    i = pl.program_id(2)
    c = ck // t
    q = q_ref[...]
    lo = lax.broadcasted_iota(jnp.int32, q.shape, 1) < B_HALF
    zero = jnp.zeros_like(q)
    qs_ref[0:t, :] = jnp.where(lo, q, zero)
    qs_ref[t:2 * t, :] = jnp.where(lo, zero, q)
    m_ref[...] = jnp.full(m_ref.shape, NEG_BIG, F32)
    l_ref[...] = jnp.zeros(l_ref.shape, F32)
    acc_ref[...] = jnp.zeros(acc_ref.shape, F32)

    def body(j, carry):
        rows = pl.ds(pl.multiple_of(j * ck, ck), ck)
        s = lax.dot_general(qs_ref[...], k_ref[rows, :], _NT, preferred_element_type=F32)
        lo_off, hi_off = _band_offsets(c)
        bias = band_ref[0, jnp.clip(c * j - i, lo_off, hi_off) - lo_off]
        for comp in range(2):
            part = slice(comp * t, (comp + 1) * t)
            _softmax_step(s[part] + bias, v_ref[rows, :], m_ref, l_ref, acc_ref, part)
        return carry

    lax.fori_loop(0, nchunks, body, 0, unroll=True)
    o = acc_ref[...] / l_ref[...]
    lq = lam_ref[...]
    lam = (jnp.exp(jnp.sum(lq[0:1] * lq[1:2], axis=-1, keepdims=True))
           - jnp.exp(jnp.sum(lq[2:3] * lq[3:4], axis=-1, keepdims=True)) + lam_init)
    ob = o[0:t] - lam * o[t:2 * t]
    o_ref[...] = (_head_norm(ob, gs_ref[...]) * (1.0 - lam_init)).astype(o_ref.dtype)


def _attn_b_call(qb, kb, vb, band, lam_qk, g_subln, merged, tok0, b, n, t, lam_init):
    qblk0 = tok0 // t
    kblk0 = tok0 // n
    nq = n // t
    ck = band.shape[3]
    return pl.pallas_call(
        functools.partial(_attn_b_kernel, t=t, ck=ck, nchunks=n // ck, lam_init=lam_init),
        grid=(B_HEADS, b, nq),
        in_specs=[
            pl.BlockSpec((t, HEAD_DIM), lambda h, bi, qi: (qblk0 + bi * nq + qi, h)),
            pl.BlockSpec((n, HEAD_DIM), lambda h, bi, qi: (kblk0 + bi, h)),
            pl.BlockSpec((n, HEAD_DIM), lambda h, bi, qi: (kblk0 + bi, h)),
            pl.BlockSpec((1,) + band.shape[1:], lambda h, bi, qi: (h, 0, 0, 0), pipeline_mode=pl.Buffered(1)),
            pl.BlockSpec((4, B_HALF), lambda h, bi, qi: (0, 0)),
            pl.BlockSpec((1, LANES), lambda h, bi, qi: (0, 0)),
            pl.BlockSpec(memory_space=pl.ANY),
        ],
        out_specs=pl.BlockSpec((t, HEAD_DIM), lambda h, bi, qi: (qblk0 + bi * nq + qi, A_Q // HEAD_DIM + h)),
        out_shape=jax.ShapeDtypeStruct(merged.shape, merged.dtype),
        input_output_aliases={6: 0},
        scratch_shapes=[
            pltpu.VMEM((2 * t, HEAD_DIM), BF16),
            pltpu.VMEM((2 * t, LANES), F32),
            pltpu.VMEM((2 * t, LANES), F32),
            pltpu.VMEM((2 * t, HEAD_DIM), F32),
        ],
        compiler_params=_params(("arbitrary", "arbitrary", "arbitrary")),
        name="attn_b",
    )(qb, kb, vb, band, lam_qk, g_subln, merged)


def _t5_bucket(rel):
    nb = NUM_BUCKETS // 2
    ret = (rel > 0).astype(jnp.int32) * nb
    n = jnp.abs(rel)
    max_exact = nb // 2
    nf = jnp.maximum(n, 1).astype(F32)
    large = max_exact + (jnp.log(nf / max_exact) / math.log(MAX_DISTANCE / max_exact)
                         * (nb - max_exact)).astype(jnp.int32)
    large = jnp.minimum(large, nb - 1)
    return ret + jnp.where(n < max_exact, n, large)


_T5_SATURATION = math.ceil((NUM_BUCKETS // 4) * (MAX_DISTANCE / (NUM_BUCKETS // 4)) ** (
    (NUM_BUCKETS // 2 - 1 - NUM_BUCKETS // 4) / (NUM_BUCKETS // 2 - NUM_BUCKETS // 4)))


def _band_offsets(c):
    return -(c + 1), 2


def _band_kernel(vec_ref, o_ref, *, t, ck):
    width = vec_ref.shape[2]
    x = jnp.broadcast_to(vec_ref[0], (t, width))
    o_ref[0, 0] = pltpu.roll(x, width - t + 1, 1, stride=1, stride_axis=0)[:, :ck]


def _bias_tables(rel_bias, t, c):
    assert t + 1 >= _T5_SATURATION
    heads = rel_bias.shape[1]
    lo_off, hi_off = _band_offsets(c)
    nd = hi_off - lo_off + 1
    ck = c * t
    width = (c + 2) * t
    rel = ((jnp.arange(nd, dtype=jnp.int32)[:, None] + lo_off) * t
           + jnp.arange(-(t - 1), width - (t - 1), dtype=jnp.int32)[None, :])
    vec = jnp.take(rel_bias, _t5_bucket(rel), axis=0).astype(F32)
    vec = (vec.transpose(2, 0, 1) * LOG2E).reshape(heads * nd, 1, width)
    return pl.pallas_call(
        functools.partial(_band_kernel, t=t, ck=ck),
        grid=(heads, nd),
        in_specs=[pl.BlockSpec((1, 1, width), lambda h, d: (h * nd + d, 0, 0))],
        out_specs=pl.BlockSpec((1, 1, t, ck), lambda h, d: (h, d, 0, 0)),
        out_shape=jax.ShapeDtypeStruct((heads, nd, t, ck), F32),
        compiler_params=_params(("arbitrary", "arbitrary")),
        name="band",
    )(vec)


_DFT_INNER = 64


def _dft_stage_tables(n):
    n2 = min(_DFT_INNER, n // 8)
    n1 = n // n2
    i1 = jnp.arange(n1, dtype=jnp.int32)
    ang1 = ((i1[:, None] * i1[None, :]) % n1).astype(F32) * (2.0 * math.pi / n1)
    c1 = jnp.cos(ang1) * n ** -0.5
    s1 = jnp.sin(ang1) * n ** -0.5
    stage1 = jnp.concatenate([jnp.concatenate([c1, -s1], axis=1), jnp.concatenate([s1, c1], axis=1)], axis=0)
    i2 = jnp.arange(n2, dtype=jnp.int32)
    k = i1[:, None, None] + n1 * i2[None, :, None]
    ang2 = ((k * i2[None, None, :]) % n).astype(F32) * (2.0 * math.pi / n)
    stage2 = jnp.concatenate([jnp.cos(ang2), -jnp.sin(ang2)], axis=2)
    return stage1.astype(BF16), stage2.astype(BF16)


def _fourier_kernel(p_ref, q_ref, s1_ref, s2_ref, w_ref, _merged_ref, o_ref, y_ref, f_ref):
    n1, n2 = s2_ref.shape[0], s2_ref.shape[1]
    lanes = p_ref.shape[1]
    cols = 4
    stage1 = s1_ref[...]
    for t2 in range(0, n2, cols):
        zr = [p_ref[pl.ds(t2 + j, n1, stride=n2), :] for j in range(cols)]
        zi = [q_ref[pl.ds(t2 + j, n1, stride=n2), :] for j in range(cols)]
        z = jnp.concatenate([jnp.concatenate(zr, axis=1), jnp.concatenate(zi, axis=1)], axis=0)
        y = jnp.dot(stage1, z.astype(BF16), preferred_element_type=F32)
        for j in range(cols):
            y_ref[0, pl.ds(t2 + j, n1, stride=n2), :] = y[:n1, j * lanes:(j + 1) * lanes]
            y_ref[1, pl.ds(t2 + j, n1, stride=n2), :] = y[n1:, j * lanes:(j + 1) * lanes]
    for k1 in range(n1):
        blk = jnp.concatenate([y_ref[0, k1 * n2:(k1 + 1) * n2, :], y_ref[1, k1 * n2:(k1 + 1) * n2, :]], axis=0)
        f_ref[pl.ds(k1, n2, stride=n1), :] = jnp.dot(s2_ref[k1], blk.astype(BF16), preferred_element_type=F32)
    o_ref[...] = jnp.dot(f_ref[...].astype(BF16), w_ref[0], preferred_element_type=F32).astype(o_ref.dtype)


def _fourier_call(stage1, stage2, p, q, wf, merged, tok0, b, n):
    kblk0 = tok0 // n
    col0 = (A_Q + B_WIDTH) // C_GROUP_W
    return pl.pallas_call(
        _fourier_kernel,
        grid=(b, C_GROUPS),
        in_specs=[
            pl.BlockSpec((n, C_GROUP_W), lambda bi, g: (kblk0 + bi, g)),
            pl.BlockSpec((n, C_GROUP_W), lambda bi, g: (kblk0 + bi, g)),
            pl.BlockSpec(stage1.shape, lambda bi, g: (0, 0)),
            pl.BlockSpec(stage2.shape, lambda bi, g: (0, 0, 0)),
            pl.BlockSpec((1, C_GROUP_W, C_GROUP_W), lambda bi, g: (g, 0, 0)),
            pl.BlockSpec(memory_space=pl.ANY),
        ],
        out_specs=pl.BlockSpec((n, C_GROUP_W), lambda bi, g: (kblk0 + bi, col0 + g)),
        out_shape=jax.ShapeDtypeStruct(merged.shape, merged.dtype),
        input_output_aliases={5: 0},
        scratch_shapes=[pltpu.VMEM((2, n, C_GROUP_W), F32), pltpu.VMEM((n, C_GROUP_W), F32)],
        compiler_params=_params(("arbitrary", "arbitrary")),
        name="fourier",
    )(p, q, stage1, stage2, wf, merged)


def _out_kernel(xp_ref, xs_ref, *rest, nt0):
    _split_apply(xp_ref, xs_ref, nt0, _out_body, *rest)


def _out_body(x_ref, mix_ref, mod_ref, w_ref, gn_ref, wr_ref, x1_ref, hn_ref, ids_ref):
    mix = jnp.dot(mix_ref[...], w_ref[...], preferred_element_type=F32)
    x1 = x_ref[...] + mod_ref[0, 2:3, :] * mix
    x1_ref[...] = x1
    y = x1 * lax.rsqrt(jnp.mean(x1 * x1, axis=-1, keepdims=True) + EPS) * gn_ref[...]
    hn = y * (1.0 + mod_ref[0, 4:5, :]) + mod_ref[0, 3:4, :]
    d = hn.shape[1]
    hn_ref[:, :d] = hn

    logits = jnp.dot(hn.astype(BF16), wr_ref[...], preferred_element_type=F32)
    lane = lax.broadcasted_iota(jnp.int32, logits.shape, 1)
    big = jnp.int32(LANES)
    is_g = lane < N_GROUPS
    gmax = jnp.max(jnp.where(is_g, logits, -jnp.inf), axis=-1, keepdims=True)
    g_sel = jnp.min(jnp.where(jnp.logical_and(is_g, logits == gmax), lane, big), axis=-1, keepdims=True)
    g_w = 1.0 / jnp.sum(jnp.where(is_g, jnp.exp(logits - gmax), 0.0), axis=-1, keepdims=True)
    lo_lane = N_GROUPS + g_sel * EXPERTS_PER_GROUP
    in_g = jnp.logical_and(lane >= lo_lane, lane < lo_lane + EXPERTS_PER_GROUP)
    v0 = jnp.max(jnp.where(in_g, logits, -jnp.inf), axis=-1, keepdims=True)
    i0 = jnp.min(jnp.where(jnp.logical_and(in_g, logits == v0), lane, big), axis=-1, keepdims=True)
    rest = jnp.logical_and(in_g, lane != i0)
    v1 = jnp.max(jnp.where(rest, logits, -jnp.inf), axis=-1, keepdims=True)
    i1 = jnp.min(jnp.where(jnp.logical_and(rest, logits == v1), lane, big), axis=-1, keepdims=True)
    e1 = jnp.exp(v1 - v0)
    w0 = g_w / (1.0 + e1)
    w1 = g_w * e1 / (1.0 + e1)
    swap = i1 < i0
    ea = jnp.where(swap, i1, i0) - N_GROUPS
    eb = jnp.where(swap, i0, i1) - N_GROUPS
    wa = jnp.where(swap, w1, w0)
    wb = jnp.where(swap, w0, w1)
    ml = lax.broadcasted_iota(jnp.int32, ids_ref.shape, 1)
    ids_ref[...] = jnp.where(ml == 0, ea, jnp.where(ml == 1, eb, 0))
    hn_ref[:, d:] = jnp.where(lane == 0, wa, jnp.where(lane == 1, wb, 0.0))


def _out_call(lay, mix, xp, xs, mod3, w_out, gn, w_router, tm):
    t, d = lay.t, xp.shape[1]
    row = lambda i: (i, 0)
    const = lambda i: (0, 0)
    meta = LANES
    return pl.pallas_call(
        functools.partial(_out_kernel, nt0=lay.t0 // tm),
        grid=(t // tm,),
        in_specs=lay.split_specs(tm, d) + [
            pl.BlockSpec((tm, mix.shape[1]), row),
            pl.BlockSpec((1, 6, d), lambda i: (lay.batch_of_tile(i, tm), 0, 0)),
            pl.BlockSpec((d, d), const),
            pl.BlockSpec((1, d), const),
            pl.BlockSpec((d, LANES), const),
        ],
        out_specs=[pl.BlockSpec((tm, d), row), pl.BlockSpec((tm, d + LANES), row),
                   pl.BlockSpec((tm, meta), row)],
        out_shape=[jax.ShapeDtypeStruct((t, d), F32), jax.ShapeDtypeStruct((t, d + LANES), F32),
                   jax.ShapeDtypeStruct((t, meta), jnp.int32)],
        compiler_params=_params(("arbitrary",)),
        name="out_proj",
    )(xp, xs, mix, mod3, w_out, gn, w_router)


def _bucket_onehot(ids_ref):
    ids = ids_ref[...]
    ea = ids[:, 0:1]
    eb = ids[:, 1:2]
    la = ea & (EXPERTS_PER_GROUP - 1)
    lb = eb & (EXPERTS_PER_GROUP - 1)
    pair = ((la * (2 * EXPERTS_PER_GROUP - 1 - la)) >> 1) + (lb - la - 1)
    bucket = (ea >> (EXPERTS_PER_GROUP.bit_length() - 1)) * len(_PAIRS) + pair
    lane = lax.broadcasted_iota(jnp.int32, (ids.shape[0], LANES), 1)
    return lane == bucket


def _positions_kernel(ids_ref, pos_ref, counts_ref, run_ref, base_ref, start_ref, earlier_ref, *, tm):
    p = pl.program_id(0)
    i = pl.program_id(1)
    tt = ids_ref.shape[0]
    onehot = _bucket_onehot(ids_ref)

    @pl.when(jnp.logical_and(p == 0, i == 0))
    def _():
        run_ref[...] = jnp.zeros(run_ref.shape, F32)

    @pl.when(p == 0)
    def _():
        base_ref[pl.ds(i, 1), :] = run_ref[...]
        run_ref[...] += jnp.sum(jnp.where(onehot, 1.0, 0.0), axis=0, keepdims=True)

    @pl.when(jnp.logical_and(p == 1, i == 0))
    def _():
        counts = run_ref[...]
        counts_ref[...] = counts
        tiles = jnp.floor((counts + (tm - 1)) * (1.0 / tm)).astype(BF16)
        a = lax.broadcasted_iota(jnp.int32, (LANES, LANES), 0)
        b = lax.broadcasted_iota(jnp.int32, (LANES, LANES), 1)
        before = jnp.where(a < b, 1.0, 0.0).astype(BF16)
        start_ref[...] = jnp.dot(tiles, before, preferred_element_type=F32) * tm
        r = lax.broadcasted_iota(jnp.int32, (tt, tt), 0)
        c = lax.broadcasted_iota(jnp.int32, (tt, tt), 1)
        earlier_ref[...] = jnp.where(c < r, 1.0, 0.0).astype(BF16)

    @pl.when(p == 1)
    def _():
        oh = jnp.where(onehot, 1.0, 0.0)
        rank = jnp.dot(earlier_ref[...], oh.astype(BF16), preferred_element_type=F32)
        val = start_ref[...] + base_ref[pl.ds(i, 1), :] + rank
        pos = jnp.sum(oh * val, axis=1, keepdims=True)
        pos_ref[...] = jnp.broadcast_to(pos, pos_ref.shape).astype(jnp.int32)


def _positions_call(ids, tm, tt):
    t, meta = ids.shape
    nt = t // tt
    assert t // tm + 1 <= 256 and tm & (tm - 1) == 0
    pos, counts = pl.pallas_call(
        functools.partial(_positions_kernel, tm=tm),
        grid=(2, nt),
        in_specs=[pl.BlockSpec((tt, meta), lambda p, i: (i, 0))],
        out_specs=[pl.BlockSpec((tt, meta), lambda p, i: (i * p, 0)),
                   pl.BlockSpec((1, LANES), lambda p, i: (0, 0))],
        out_shape=[jax.ShapeDtypeStruct((t, meta), jnp.int32), jax.ShapeDtypeStruct((1, LANES), F32)],
        scratch_shapes=[pltpu.VMEM((1, LANES), F32), pltpu.VMEM((nt, LANES), F32), pltpu.VMEM((1, LANES), F32),
                        pltpu.VMEM((tt, tt), BF16)],
        compiler_params=_params(("arbitrary", "arbitrary")),
        name="positions",
    )(ids)
    return pos[:, 0], counts[0, :N_BUCKETS].astype(jnp.int32)


def _route_plan(ids, tm, tt):
    t = ids.shape[0]
    pos, counts = _positions_call(ids, tm, tt)
    tiles = (counts + tm - 1) // tm
    tile_end = jnp.cumsum(tiles)
    tile_start = tile_end - tiles
    assert t % tm == 0
    n_tiles = t // tm + N_BUCKETS
    src = jnp.zeros((n_tiles * tm,), jnp.int32).at[pos].set(jnp.arange(t, dtype=jnp.int32))
    tile_ids = jnp.arange(n_tiles, dtype=jnp.int32)
    used = tile_end[-1]
    last = jnp.minimum(tile_ids, used - 1)
    tile_bucket = jnp.sum((tile_end[None, :] <= last[:, None]).astype(jnp.int32), axis=1)
    pairs = np.array(_PAIRS, np.int32)
    tgrp = tile_bucket // len(_PAIRS)
    tpair = tile_bucket % len(_PAIRS)
    tile_ea = tgrp * EXPERTS_PER_GROUP + jnp.take(jnp.asarray(pairs[:, 0]), tpair)
    tile_eb = tgrp * EXPERTS_PER_GROUP + jnp.take(jnp.asarray(pairs[:, 1]), tpair)
    rows_left = jnp.take(counts, tile_bucket) - (tile_ids - jnp.take(tile_start, tile_bucket)) * tm
    tile_rows = jnp.where(tile_ids < used, jnp.clip(rows_left, 0, tm), 0).astype(jnp.int32)
    return pos.astype(jnp.int32), src, tile_ea.astype(jnp.int32), tile_eb.astype(jnp.int32), tile_rows, used


def _row_copy(src_hbm, row, buf, slot, r, sem):
    return pltpu.make_async_copy(src_hbm.at[pl.ds(row, 1), :], buf.at[slot, pl.ds(r, 1), :], sem.at[slot])


def _gather_start(idx_ref, base, rows, src_hbm, buf, slot, sem):
    for r in range(rows):
        _row_copy(src_hbm, idx_ref[base + r], buf, slot, r, sem).start()


def _gather_wait(rows, src_hbm, buf, slot, sem):
    pltpu.make_async_copy(src_hbm.at[pl.ds(0, rows), :], buf.at[slot], sem.at[slot]).wait()


def _moe_kernel(src_ref, ea_ref, eb_ref, rows_ref, h_hbm, w1a_ref, w3a_ref, w2a_ref, w1b_ref, w3b_ref, w2b_ref,
                y_ref, buf, sem, *, tm):
    i = pl.program_id(0)
    nt = pl.num_programs(0)
    depth = buf.shape[0]
    ahead = depth - 1
    slot = i % depth
    used = rows_ref[i] > 0
    d = y_ref.shape[1]

    @pl.when(i == 0)
    def _():
        for j in range(ahead):
            _gather_start(src_ref, j * tm, tm, h_hbm, buf, j, sem)

    @pl.when(jnp.logical_and(used, i + ahead < nt))
    def _():
        _gather_start(src_ref, (i + ahead) * tm, tm, h_hbm, buf, (i + ahead) % depth, sem)

    @pl.when(jnp.logical_or(i < ahead, rows_ref[jnp.maximum(i - ahead, 0)] > 0))
    def _():
        _gather_wait(tm, h_hbm, buf, slot, sem)

    @pl.when(jnp.logical_not(used))
    def _():
        y_ref[...] = jnp.zeros(y_ref.shape, y_ref.dtype)

    @pl.when(used)
    def _():
        h = buf[slot, :, :d].astype(BF16)

        def expert(w1_ref, w3_ref, w2_ref):
            a = (jax.nn.silu(jnp.dot(h, w1_ref[0], preferred_element_type=F32))
                 * jnp.dot(h, w3_ref[0], preferred_element_type=F32))
            return jnp.dot(a.astype(BF16), w2_ref[0], preferred_element_type=F32)

        w = buf[slot, :, d:]
        y_ref[...] = (expert(w1a_ref, w3a_ref, w2a_ref) * w[:, 0:1]
                      + expert(w1b_ref, w3b_ref, w2b_ref) * w[:, 1:2])


def _moe_call(src, tile_ea, tile_eb, tile_rows, hn, w1, w3, w2, tm):
    n_tiles = tile_ea.shape[0]
    d = w1.shape[1]
    f = w1.shape[2]
    wa = lambda i, src, ea, eb, rows: (ea[i], 0, 0)
    wb = lambda i, src, ea, eb, rows: (eb[i], 0, 0)
    grid_spec = pltpu.PrefetchScalarGridSpec(
        num_scalar_prefetch=4,
        grid=(n_tiles,),
        in_specs=[
            pl.BlockSpec(memory_space=pl.ANY),
            pl.BlockSpec((1, d, f), wa), pl.BlockSpec((1, d, f), wa), pl.BlockSpec((1, f, d), wa),
            pl.BlockSpec((1, d, f), wb), pl.BlockSpec((1, d, f), wb), pl.BlockSpec((1, f, d), wb),
        ],
        out_specs=pl.BlockSpec((tm, d), lambda i, *_: (i, 0)),
        scratch_shapes=[pltpu.VMEM((3, tm, hn.shape[1]), F32), pltpu.SemaphoreType.DMA((3,))],
    )
    assert n_tiles >= 3
    return pl.pallas_call(
        functools.partial(_moe_kernel, tm=tm),
        grid_spec=grid_spec,
        out_shape=jax.ShapeDtypeStruct((n_tiles * tm, d), F32),
        compiler_params=_params(("arbitrary",)),
        name="moe",
    )(src, tile_ea, tile_eb, tile_rows, hn, w1, w3, w2, w1, w3, w2)


def _combine_kernel(pos_ref, y_hbm, x1_ref, mod_ref, op_ref, os_ref, buf, sem, *, tm, nt0):
    i = pl.program_id(0)
    nt = pl.num_programs(0)
    slot = i % 2

    @pl.when(i == 0)
    def _():
        _gather_start(pos_ref, 0, tm, y_hbm, buf, 0, sem)

    _gather_wait(tm, y_hbm, buf, slot, sem)

    @pl.when(i + 1 < nt)
    def _():
        _gather_start(pos_ref, (i + 1) * tm, tm, y_hbm, buf, 1 - slot, sem)

    out = x1_ref[...] + mod_ref[0, 5:6, :] * buf[slot]

    @pl.when(i < nt0)
    def _():
        op_ref[...] = out

    @pl.when(i >= nt0)
    def _():
        os_ref[...] = out


def _combine_call(lay, pos, y_sorted, x1, mod3, tm):
    t, d = x1.shape
    grid_spec = pltpu.PrefetchScalarGridSpec(
        num_scalar_prefetch=1,
        grid=(t // tm,),
        in_specs=[
            pl.BlockSpec(memory_space=pl.ANY),
            pl.BlockSpec((tm, d), lambda i, pos: (i, 0)),
            pl.BlockSpec((1, 6, d), lambda i, pos: (lay.batch_of_tile(i, tm), 0, 0)),
        ],
        out_specs=lay.split_specs(tm, d),
        scratch_shapes=[pltpu.VMEM((2, tm, d), F32), pltpu.SemaphoreType.DMA((2,))],
    )
    return pl.pallas_call(
        functools.partial(_combine_kernel, tm=tm, nt0=lay.t0 // tm),
        grid_spec=grid_spec,
        out_shape=lay.split_shapes(d, F32),
        compiler_params=_params(("arbitrary",)),
        name="combine",
    )(pos, y_sorted, x1, mod3)


def _rope_tables(n):
    n_rows = n // GRID_W
    rows = jnp.repeat(jnp.arange(n_rows), GRID_W).astype(F32)
    cols = jnp.tile(jnp.arange(GRID_W), n_rows).astype(F32)
    half = HEAD_DIM // 2
    inv = ROPE_THETA ** (-jnp.arange(0, half, 2, dtype=F32) / half)
    ang = jnp.concatenate([rows[:, None] * inv, cols[:, None] * inv], axis=-1)
    sign = jnp.tile(jnp.array([-1.0, 1.0], F32), half)
    return jnp.repeat(jnp.cos(ang), 2, axis=-1), jnp.repeat(jnp.sin(ang), 2, axis=-1) * sign


def _lambda_init(layer_idx):
    return 0.8 - 0.6 * math.exp(-0.3 * layer_idx)


def _tiled_gain(g):
    return jnp.tile(g, LANES // g.shape[-1]).reshape(1, LANES).astype(F32)


def kernel(x_prompt, x_sample, c_prompt, c_sample, rel_bias, w_ada, b_ada, g_norm_mix, w_in, g_qa, g_ka,
           g_qb, g_kb, lam_qk, g_subln, w_fourier, w_out, g_norm_ffn, w_group, w_expert, w1, w3, w2):
    b0, n0, d = x_prompt.shape
    b1, n1, _ = x_sample.shape
    depth = w_in.shape[0]
    lay = _Layout(b0, n0, b1, n1)
    trunks = ((0, b0, n0), (lay.t0, b1, n1))
    n_max = max(n0, n1)

    tm = _pick(math.gcd(n0, n1), 512)
    tq_a = _pick(math.gcd(n0, n1), 256)
    tk_a = _pick(math.gcd(n0, n1), 2048)
    t_b = _pick(math.gcd(n0, n1), 512)
    tm_moe = 256

    xp = x_prompt.reshape(b0 * n0, d)
    xs = x_sample.reshape(b1 * n1, d)
    nb = b0 + b1
    bp = -(-nb // 8) * 8
    c_all = jnp.zeros((bp, d), F32).at[:nb].set(jnp.concatenate([c_prompt, c_sample], axis=0))
    mod = _ada_call(c_all, w_ada, b_ada).reshape(depth, bp, 6, d)

    cos_e, sin_e = _rope_tables(n_max)
    chunk_ratio = {n: min(2, n // t_b) for n in (n0, n1)}
    bands = {c: _bias_tables(rel_bias, t_b, c) for c in sorted(set(chunk_ratio.values()))}
    dft = {n: _dft_stage_tables(n) for n in sorted({n0, n1})}
    cidx = jnp.arange(C_GROUP_W, dtype=jnp.int32)
    ang_c = ((cidx[:, None] * cidx[None, :]) % C_GROUP_W).astype(F32) * (2.0 * math.pi / C_GROUP_W)
    dft_c = (jnp.concatenate([jnp.cos(ang_c), jnp.sin(ang_c)], axis=1) * C_GROUP_W ** -0.5).astype(BF16)

    for l in range(depth):
        mod3 = mod[l]
        lam_init = _lambda_init(l)
        qa, ka, va, qb, kb, vb, p, q = _proj_call(
            lay, xp, xs, mod3, g_norm_mix[l].reshape(1, d), w_in[l].astype(BF16),
            _tiled_gain(g_qa[l]), _tiled_gain(g_ka[l]), _tiled_gain(g_qb[l]), _tiled_gain(g_kb[l]),
            cos_e, sin_e, dft_c, tm)

        wf = w_fourier[l].astype(BF16)
        mix = jnp.zeros((lay.t, A_Q + B_WIDTH + C_WIDTH), BF16)
        for tok0, b, n in trunks:
            mix = _attn_a_call(qa, ka, va, mix, tok0, b, n, tq_a, min(tk_a, max(n // 2, LANES)))
            mix = _attn_b_call(qb, kb, vb, bands[chunk_ratio[n]], lam_qk[l], _tiled_gain(g_subln[l]), mix,
                               tok0, b, n, t_b, lam_init)
            mix = _fourier_call(dft[n][0], dft[n][1], p, q, wf, mix, tok0, b, n)

        w_router = jnp.zeros((d, LANES), F32).at[:, :N_GROUPS].set(w_group[l])
        w_router = w_router.at[:, N_GROUPS:N_GROUPS + N_EXPERTS].set(w_expert[l]).astype(BF16)
        x1, hn2, ids = _out_call(lay, mix, xp, xs, mod3, w_out[l].astype(BF16),
                                 g_norm_ffn[l].reshape(1, d), w_router, tm)

        pos, src, tile_ea, tile_eb, tile_rows, _ = _route_plan(ids, tm_moe, _pick(math.gcd(n0, n1), 2048))
        y_sorted = _moe_call(src, tile_ea, tile_eb, tile_rows, hn2, w1[l].astype(BF16), w3[l].astype(BF16),
                             w2[l].astype(BF16), tm_moe)
        xp, xs = _combine_call(lay, pos, y_sorted, x1, mod3, tm)

    return (xp.reshape(b0, n0, d), xs.reshape(b1, n1, d))
```

```python
import functools
import math

import jax
import jax.numpy as jnp
import numpy as np
from jax import lax
from jax.experimental import pallas as pl
from jax.experimental.pallas import tpu as pltpu

F32 = jnp.float32
BF16 = jnp.bfloat16

D_MODEL = 2048
HEAD_DIM = 128
A_HEADS = 8
A_KV_HEADS = 2
A_GROUP = A_HEADS // A_KV_HEADS
A_Q = A_HEADS * HEAD_DIM
A_KV = A_KV_HEADS * HEAD_DIM
B_HEADS = 4
B_HALF = HEAD_DIM // 2
B_WIDTH = B_HEADS * HEAD_DIM
C_WIDTH = 512
C_GROUPS = 4
C_GROUP_W = 128
IN_WIDTH = A_Q + 2 * A_KV + 3 * B_WIDTH + C_WIDTH
GRID_W = 64
ROPE_THETA = 10000.0
NUM_BUCKETS = 32
MAX_DISTANCE = 128
N_GROUPS = 4
EXPERTS_PER_GROUP = 4
N_EXPERTS = 16
D_FF_EXPERT = 512
EPS = 1e-6
LOG2E = 1.4426950408889634

_PAIRS = ((0, 1), (0, 2), (0, 3), (1, 2), (1, 3), (2, 3))
N_BUCKETS = N_GROUPS * len(_PAIRS)

V7X_VMEM_BYTES = 64 * 1024 * 1024
VMEM_LIMIT = V7X_VMEM_BYTES - 8 * 1024 * 1024
LANES = 128
NEG_BIG = -1e30


def _params(sem, **kw):
    return pltpu.CompilerParams(dimension_semantics=sem, vmem_limit_bytes=VMEM_LIMIT, **kw)

def _pick(total, pref):
    t = min(pref, total)
    while total % t:
        t //= 2
    return t


def _ada_kernel(c_ref, w_ref, b_ref, o_ref):
    h = jax.nn.silu(c_ref[...]).astype(BF16)
    o_ref[0] = jnp.dot(h, w_ref[0].astype(BF16), preferred_element_type=F32) + b_ref[0]


def _ada_call(c_all, w_ada, b_ada):
    depth, d, e = w_ada.shape
    bp = c_all.shape[0]
    tn = _pick(e, 1024)
    return pl.pallas_call(
        _ada_kernel,
        grid=(depth, e // tn),
        in_specs=[
            pl.BlockSpec((bp, d), lambda l, j: (0, 0)),
            pl.BlockSpec((1, d, tn), lambda l, j: (l, 0, j)),
            pl.BlockSpec((1, 1, tn), lambda l, j: (l, 0, j)),
        ],
        out_specs=pl.BlockSpec((1, bp, tn), lambda l, j: (l, 0, j)),
        out_shape=jax.ShapeDtypeStruct((depth, bp, e), F32),
        compiler_params=_params(("arbitrary", "arbitrary")),
        name="ada",
    )(c_all, w_ada, b_ada.reshape(depth, 1, e))


class _Layout:
    def __init__(self, b0, n0, b1, n1):
        self.b = (b0, b1)
        self.n = (n0, n1)
        self.t0 = b0 * n0
        self.t = b0 * n0 + b1 * n1

    def batch_of_tile(self, i, tm):
        tok = i * tm
        return jnp.where(tok < self.t0, tok // self.n[0], self.b[0] + (tok - self.t0) // self.n[1])

    def pos_block_of_tile(self, i, tm):
        tok = i * tm
        pos = jnp.where(tok < self.t0, tok % self.n[0], (tok - self.t0) % self.n[1])
        return pos // tm

    def split_specs(self, tm, d):
        nt0 = self.t0 // tm
        return [pl.BlockSpec((tm, d), lambda i, *_: (jnp.minimum(i, nt0 - 1), 0)),
                pl.BlockSpec((tm, d), lambda i, *_: (jnp.maximum(i - nt0, 0), 0))]

    def split_shapes(self, d, dtype):
        return [jax.ShapeDtypeStruct((self.t0, d), dtype), jax.ShapeDtypeStruct((self.t - self.t0, d), dtype)]


def _split_apply(xp_ref, xs_ref, nt0, body, *rest):
    i = pl.program_id(0)

    @pl.when(i < nt0)
    def _():
        body(xp_ref, *rest)

    @pl.when(i >= nt0)
    def _():
        body(xs_ref, *rest)


def _head_norm(z, g):
    return z * lax.rsqrt(jnp.mean(z * z, axis=-1, keepdims=True) + EPS) * g


def _half_norm(z, g, lo):
    zz = z * z
    s_lo = jnp.sum(jnp.where(lo, zz, 0.0), axis=-1, keepdims=True)
    s_hi = jnp.sum(jnp.where(lo, 0.0, zz), axis=-1, keepdims=True)
    inv = jnp.where(lo, lax.rsqrt(s_lo / B_HALF + EPS), lax.rsqrt(s_hi / B_HALF + EPS))
    return z * inv * g


def _rope(z, c, s_signed, even):
    partner = jnp.where(even, pltpu.roll(z, LANES - 1, 1), pltpu.roll(z, 1, 1))
    return z * c + partner * s_signed


def _proj_kernel(xp_ref, xs_ref, *rest, nt0):
    _split_apply(xp_ref, xs_ref, nt0, _proj_body, *rest)


def _proj_body(x_ref, mod_ref, gn_ref, w_ref, gqa_ref, gka_ref, gqb_ref, gkb_ref, cos_ref, sin_ref,
               dft_ref, qa_ref, ka_ref, va_ref, qb_ref, kb_ref, vb_ref, p_ref, q_ref):
    x = x_ref[...]
    y = x * lax.rsqrt(jnp.mean(x * x, axis=-1, keepdims=True) + EPS) * gn_ref[...]
    hn = (y * (1.0 + mod_ref[0, 1:2, :]) + mod_ref[0, 0:1, :]).astype(BF16)

    def seg(a, b):
        return jnp.dot(hn, w_ref[:, a:b], preferred_element_type=F32)

    tm = x.shape[0]
    lane = lax.broadcasted_iota(jnp.int32, (tm, LANES), 1)
    even = (lane % 2) == 0
    lo = lane < B_HALF
    cos = cos_ref[...]
    sin = sin_ref[...]

    scale_a = HEAD_DIM ** -0.5 * LOG2E
    z = seg(0, A_Q)
    for h in range(A_HEADS):
        sl = slice(h * HEAD_DIM, (h + 1) * HEAD_DIM)
        qa_ref[:, sl] = (_rope(_head_norm(z[:, sl], gqa_ref[...]), cos, sin, even) * scale_a).astype(BF16)
    off = A_Q
    z = seg(off, off + A_KV)
    for h in range(A_KV_HEADS):
        sl = slice(h * HEAD_DIM, (h + 1) * HEAD_DIM)
        ka_ref[:, sl] = _rope(_head_norm(z[:, sl], gka_ref[...]), cos, sin, even).astype(BF16)
    off += A_KV
    va_ref[...] = seg(off, off + A_KV).astype(BF16)
    off += A_KV

    scale_b = B_HALF ** -0.5 * LOG2E
    z = seg(off, off + B_WIDTH)
    for h in range(B_HEADS):
        sl = slice(h * HEAD_DIM, (h + 1) * HEAD_DIM)
        qb_ref[:, sl] = (_half_norm(z[:, sl], gqb_ref[...], lo) * scale_b).astype(BF16)
    off += B_WIDTH
    z = seg(off, off + B_WIDTH)
    for h in range(B_HEADS):
        sl = slice(h * HEAD_DIM, (h + 1) * HEAD_DIM)
        kb_ref[:, sl] = _half_norm(z[:, sl], gkb_ref[...], lo).astype(BF16)
    off += B_WIDTH
    vb_ref[...] = seg(off, off + B_WIDTH).astype(BF16)
    off += B_WIDTH

    z = seg(off, off + C_WIDTH).astype(BF16)
    for g in range(C_GROUPS):
        sl = slice(g * C_GROUP_W, (g + 1) * C_GROUP_W)
        pq = jnp.dot(z[:, sl], dft_ref[...], preferred_element_type=F32)
        p_ref[:, sl] = pq[:, :C_GROUP_W]
        q_ref[:, sl] = pq[:, C_GROUP_W:]


def _proj_call(lay, xp, xs, mod3, gn, w_in, gqa, gka, gqb, gkb, cos_e, sin_e, dft_c, tm):
    t, d = lay.t, xp.shape[1]
    row = lambda i: (i, 0)
    const = lambda i: (0, 0)
    widths = (A_Q, A_KV, A_KV, B_WIDTH, B_WIDTH, B_WIDTH, C_WIDTH, C_WIDTH)
    return pl.pallas_call(
        functools.partial(_proj_kernel, nt0=lay.t0 // tm),
        grid=(t // tm,),
        in_specs=lay.split_specs(tm, d) + [
            pl.BlockSpec((1, 6, d), lambda i: (lay.batch_of_tile(i, tm), 0, 0)),
            pl.BlockSpec((1, d), const),
            pl.BlockSpec((d, IN_WIDTH), const),
            pl.BlockSpec((1, LANES), const),
            pl.BlockSpec((1, LANES), const),
            pl.BlockSpec((1, LANES), const),
            pl.BlockSpec((1, LANES), const),
            pl.BlockSpec((tm, LANES), lambda i: (lay.pos_block_of_tile(i, tm), 0)),
            pl.BlockSpec((tm, LANES), lambda i: (lay.pos_block_of_tile(i, tm), 0)),
            pl.BlockSpec((C_GROUP_W, 2 * C_GROUP_W), const),
        ],
        out_specs=[pl.BlockSpec((tm, w), row) for w in widths],
        out_shape=[jax.ShapeDtypeStruct((t, w), F32 if i >= 6 else BF16) for i, w in enumerate(widths)],
        compiler_params=_params(("arbitrary",)),
        name="proj",
    )(xp, xs, mod3, gn, w_in, gqa, gka, gqb, gkb, cos_e, sin_e, dft_c)


def _softmax_step(s, vc, m_ref, l_ref, acc_ref, rows=slice(None)):
    m_prev = m_ref[rows, :]
    m_new = jnp.maximum(m_prev, jnp.max(s, axis=1, keepdims=True))
    alpha = jnp.exp2(m_prev - m_new)
    p = jnp.exp2(s - jnp.tile(m_new, (1, s.shape[1] // LANES)))
    l_ref[rows, :] = alpha * l_ref[rows, :] + jnp.sum(p, axis=1, keepdims=True)
    acc_ref[rows, :] = alpha * acc_ref[rows, :] + jnp.dot(p.astype(BF16), vc, preferred_element_type=F32)
    m_ref[rows, :] = m_new


_NT = (((1,), (1,)), ((), ()))


def _attn_a_kernel(q_ref, k_ref, v_ref, _merged_ref, o_ref, qs_ref, m_ref, l_ref, acc_ref, *, tq, tk, n):
    for g in range(A_GROUP):
        qs_ref[g * tq:(g + 1) * tq, :] = q_ref[:, g * HEAD_DIM:(g + 1) * HEAD_DIM]
    m_ref[...] = jnp.full(m_ref.shape, NEG_BIG, F32)
    l_ref[...] = jnp.zeros(l_ref.shape, F32)
    acc_ref[...] = jnp.zeros(acc_ref.shape, F32)

    def body(j, carry):
        rows = pl.ds(pl.multiple_of(j * tk, tk), tk)
        s = lax.dot_general(qs_ref[...], k_ref[rows, :], _NT, preferred_element_type=F32)
        _softmax_step(s, v_ref[rows, :], m_ref, l_ref, acc_ref)
        return carry

    lax.fori_loop(0, n // tk, body, 0, unroll=True)
    o = acc_ref[...] / l_ref[...]
    for g in range(A_GROUP):
        o_ref[:, g * HEAD_DIM:(g + 1) * HEAD_DIM] = o[g * tq:(g + 1) * tq].astype(o_ref.dtype)


def _attn_a_call(qa, ka, va, merged, tok0, b, n, tq, tk):
    qblk0 = tok0 // tq
    kblk0 = tok0 // n
    nq = n // tq
    rows = A_GROUP * tq
    return pl.pallas_call(
        functools.partial(_attn_a_kernel, tq=tq, tk=tk, n=n),
        grid=(b, A_KV_HEADS, nq),
        in_specs=[
            pl.BlockSpec((tq, A_GROUP * HEAD_DIM), lambda bi, kv, qi: (qblk0 + bi * nq + qi, kv)),
            pl.BlockSpec((n, HEAD_DIM), lambda bi, kv, qi: (kblk0 + bi, kv)),
            pl.BlockSpec((n, HEAD_DIM), lambda bi, kv, qi: (kblk0 + bi, kv)),
            pl.BlockSpec(memory_space=pl.ANY),
        ],
        out_specs=pl.BlockSpec((tq, A_GROUP * HEAD_DIM), lambda bi, kv, qi: (qblk0 + bi * nq + qi, kv)),
        out_shape=jax.ShapeDtypeStruct(merged.shape, merged.dtype),
        input_output_aliases={3: 0},
        scratch_shapes=[
            pltpu.VMEM((rows, HEAD_DIM), BF16),
            pltpu.VMEM((rows, LANES), F32),
            pltpu.VMEM((rows, LANES), F32),
            pltpu.VMEM((rows, HEAD_DIM), F32),
        ],
        compiler_params=_params(("arbitrary", "arbitrary", "arbitrary")),
        name="attn_a",
    )(qa, ka, va, merged)


def _attn_b_kernel(q_ref, k_ref, v_ref, band_ref, lam_ref, gs_ref, _merged_ref, o_ref,
                   qs_ref, m_ref, l_ref, acc_ref, *, t, ck, nchunks, lam_init):
    i = pl.program_id(2)
    c = ck // t
    q = q_ref[...]
    lo = lax.broadcasted_iota(jnp.int32, q.shape, 1) < B_HALF
    zero = jnp.zeros_like(q)
    qs_ref[0:t, :] = jnp.where(lo, q, zero)
    qs_ref[t:2 * t, :] = jnp.where(lo, zero, q)
    m_ref[...] = jnp.full(m_ref.shape, NEG_BIG, F32)
    l_ref[...] = jnp.zeros(l_ref.shape, F32)
    acc_ref[...] = jnp.zeros(acc_ref.shape, F32)

    def body(j, carry):
        rows = pl.ds(pl.multiple_of(j * ck, ck), ck)
        s = lax.dot_general(qs_ref[...], k_ref[rows, :], _NT, preferred_element_type=F32)
        lo_off, hi_off = _band_offsets(c)
        bias = band_ref[0, jnp.clip(c * j - i, lo_off, hi_off) - lo_off]
        for comp in range(2):
            part = slice(comp * t, (comp + 1) * t)
            _softmax_step(s[part] + bias, v_ref[rows, :], m_ref, l_ref, acc_ref, part)
        return carry

    lax.fori_loop(0, nchunks, body, 0, unroll=True)
    o = acc_ref[...] / l_ref[...]
    lq = lam_ref[...]
    lam = (jnp.exp(jnp.sum(lq[0:1] * lq[1:2], axis=-1, keepdims=True))
           - jnp.exp(jnp.sum(lq[2:3] * lq[3:4], axis=-1, keepdims=True)) + lam_init)
    ob = o[0:t] - lam * o[t:2 * t]
    o_ref[...] = (_head_norm(ob, gs_ref[...]) * (1.0 - lam_init)).astype(o_ref.dtype)


def _attn_b_call(qb, kb, vb, band, lam_qk, g_subln, merged, tok0, b, n, t, lam_init):
    qblk0 = tok0 // t
    kblk0 = tok0 // n
    nq = n // t
    ck = band.shape[3]
    return pl.pallas_call(
        functools.partial(_attn_b_kernel, t=t, ck=ck, nchunks=n // ck, lam_init=lam_init),
        grid=(B_HEADS, b, nq),
        in_specs=[
            pl.BlockSpec((t, HEAD_DIM), lambda h, bi, qi: (qblk0 + bi * nq + qi, h)),
            pl.BlockSpec((n, HEAD_DIM), lambda h, bi, qi: (kblk0 + bi, h)),
            pl.BlockSpec((n, HEAD_DIM), lambda h, bi, qi: (kblk0 + bi, h)),
            pl.BlockSpec((1,) + band.shape[1:], lambda h, bi, qi: (h, 0, 0, 0), pipeline_mode=pl.Buffered(1)),
            pl.BlockSpec((4, B_HALF), lambda h, bi, qi: (0, 0)),
            pl.BlockSpec((1, LANES), lambda h, bi, qi: (0, 0)),
            pl.BlockSpec(memory_space=pl.ANY),
        ],
        out_specs=pl.BlockSpec((t, HEAD_DIM), lambda h, bi, qi: (qblk0 + bi * nq + qi, A_Q // HEAD_DIM + h)),
        out_shape=jax.ShapeDtypeStruct(merged.shape, merged.dtype),
        input_output_aliases={6: 0},
        scratch_shapes=[
            pltpu.VMEM((2 * t, HEAD_DIM), BF16),
            pltpu.VMEM((2 * t, LANES), F32),
            pltpu.VMEM((2 * t, LANES), F32),
            pltpu.VMEM((2 * t, HEAD_DIM), F32),
        ],
        compiler_params=_params(("arbitrary", "arbitrary", "arbitrary")),
        name="attn_b",
    )(qb, kb, vb, band, lam_qk, g_subln, merged)


def _t5_bucket(rel):
    nb = NUM_BUCKETS // 2
    ret = (rel > 0).astype(jnp.int32) * nb
    n = jnp.abs(rel)
    max_exact = nb // 2
    nf = jnp.maximum(n, 1).astype(F32)
    large = max_exact + (jnp.log(nf / max_exact) / math.log(MAX_DISTANCE / max_exact)
                         * (nb - max_exact)).astype(jnp.int32)
    large = jnp.minimum(large, nb - 1)
    return ret + jnp.where(n < max_exact, n, large)


_T5_SATURATION = math.ceil((NUM_BUCKETS // 4) * (MAX_DISTANCE / (NUM_BUCKETS // 4)) ** (
    (NUM_BUCKETS // 2 - 1 - NUM_BUCKETS // 4) / (NUM_BUCKETS // 2 - NUM_BUCKETS // 4)))


def _band_offsets(c):
    return -(c + 1), 2


def _band_kernel(vec_ref, o_ref, *, t, ck):
    width = vec_ref.shape[2]
    x = jnp.broadcast_to(vec_ref[0], (t, width))
    o_ref[0, 0] = pltpu.roll(x, width - t + 1, 1, stride=1, stride_axis=0)[:, :ck]


def _bias_tables(rel_bias, t, c):
    assert t + 1 >= _T5_SATURATION
    heads = rel_bias.shape[1]
    lo_off, hi_off = _band_offsets(c)
    nd = hi_off - lo_off + 1
    ck = c * t
    width = (c + 2) * t
    rel = ((jnp.arange(nd, dtype=jnp.int32)[:, None] + lo_off) * t
           + jnp.arange(-(t - 1), width - (t - 1), dtype=jnp.int32)[None, :])
    vec = jnp.take(rel_bias, _t5_bucket(rel), axis=0).astype(F32)
    vec = (vec.transpose(2, 0, 1) * LOG2E).reshape(heads * nd, 1, width)
    return pl.pallas_call(
        functools.partial(_band_kernel, t=t, ck=ck),
        grid=(heads, nd),
        in_specs=[pl.BlockSpec((1, 1, width), lambda h, d: (h * nd + d, 0, 0))],
        out_specs=pl.BlockSpec((1, 1, t, ck), lambda h, d: (h, d, 0, 0)),
        out_shape=jax.ShapeDtypeStruct((heads, nd, t, ck), F32),
        compiler_params=_params(("arbitrary", "arbitrary")),
        name="band",
    )(vec)


_DFT_INNER = 64


def _dft_stage_tables(n):
    n2 = min(_DFT_INNER, n // 8)
    n1 = n // n2
    i1 = jnp.arange(n1, dtype=jnp.int32)
    ang1 = ((i1[:, None] * i1[None, :]) % n1).astype(F32) * (2.0 * math.pi / n1)
    c1 = jnp.cos(ang1) * n ** -0.5
    s1 = jnp.sin(ang1) * n ** -0.5
    stage1 = jnp.concatenate([jnp.concatenate([c1, -s1], axis=1), jnp.concatenate([s1, c1], axis=1)], axis=0)
    i2 = jnp.arange(n2, dtype=jnp.int32)
    k = i1[:, None, None] + n1 * i2[None, :, None]
    ang2 = ((k * i2[None, None, :]) % n).astype(F32) * (2.0 * math.pi / n)
    stage2 = jnp.concatenate([jnp.cos(ang2), -jnp.sin(ang2)], axis=2)
    return stage1.astype(BF16), stage2.astype(BF16)


def _fourier_kernel(p_ref, q_ref, s1_ref, s2_ref, w_ref, _merged_ref, o_ref, y_ref, f_ref):
    n1, n2 = s2_ref.shape[0], s2_ref.shape[1]
    lanes = p_ref.shape[1]
    cols = 4
    stage1 = s1_ref[...]
    for t2 in range(0, n2, cols):
        zr = [p_ref[pl.ds(t2 + j, n1, stride=n2), :] for j in range(cols)]
        zi = [q_ref[pl.ds(t2 + j, n1, stride=n2), :] for j in range(cols)]
        z = jnp.concatenate([jnp.concatenate(zr, axis=1), jnp.concatenate(zi, axis=1)], axis=0)
        y = jnp.dot(stage1, z.astype(BF16), preferred_element_type=F32)
        for j in range(cols):
            y_ref[0, pl.ds(t2 + j, n1, stride=n2), :] = y[:n1, j * lanes:(j + 1) * lanes]
            y_ref[1, pl.ds(t2 + j, n1, stride=n2), :] = y[n1:, j * lanes:(j + 1) * lanes]
    for k1 in range(n1):
        blk = jnp.concatenate([y_ref[0, k1 * n2:(k1 + 1) * n2, :], y_ref[1, k1 * n2:(k1 + 1) * n2, :]], axis=0)
        f_ref[pl.ds(k1, n2, stride=n1), :] = jnp.dot(s2_ref[k1], blk.astype(BF16), preferred_element_type=F32)
    o_ref[...] = jnp.dot(f_ref[...].astype(BF16), w_ref[0], preferred_element_type=F32).astype(o_ref.dtype)


def _fourier_call(stage1, stage2, p, q, wf, merged, tok0, b, n):
    kblk0 = tok0 // n
    col0 = (A_Q + B_WIDTH) // C_GROUP_W
    return pl.pallas_call(
        _fourier_kernel,
        grid=(b, C_GROUPS),
        in_specs=[
            pl.BlockSpec((n, C_GROUP_W), lambda bi, g: (kblk0 + bi, g)),
            pl.BlockSpec((n, C_GROUP_W), lambda bi, g: (kblk0 + bi, g)),
            pl.BlockSpec(stage1.shape, lambda bi, g: (0, 0)),
            pl.BlockSpec(stage2.shape, lambda bi, g: (0, 0, 0)),
            pl.BlockSpec((1, C_GROUP_W, C_GROUP_W), lambda bi, g: (g, 0, 0)),
            pl.BlockSpec(memory_space=pl.ANY),
        ],
        out_specs=pl.BlockSpec((n, C_GROUP_W), lambda bi, g: (kblk0 + bi, col0 + g)),
        out_shape=jax.ShapeDtypeStruct(merged.shape, merged.dtype),
        input_output_aliases={5: 0},
        scratch_shapes=[pltpu.VMEM((2, n, C_GROUP_W), F32), pltpu.VMEM((n, C_GROUP_W), F32)],
        compiler_params=_params(("arbitrary", "arbitrary")),
        name="fourier",
    )(p, q, stage1, stage2, wf, merged)


def _out_kernel(xp_ref, xs_ref, *rest, nt0):
    _split_apply(xp_ref, xs_ref, nt0, _out_body, *rest)


def _out_body(x_ref, mix_ref, mod_ref, w_ref, gn_ref, wr_ref, x1_ref, hn_ref, ids_ref):
    mix = jnp.dot(mix_ref[...], w_ref[...], preferred_element_type=F32)
    x1 = x_ref[...] + mod_ref[0, 2:3, :] * mix
    x1_ref[...] = x1
    y = x1 * lax.rsqrt(jnp.mean(x1 * x1, axis=-1, keepdims=True) + EPS) * gn_ref[...]
    hn = y * (1.0 + mod_ref[0, 4:5, :]) + mod_ref[0, 3:4, :]
    d = hn.shape[1]
    hn_ref[:, :d] = hn

    logits = jnp.dot(hn.astype(BF16), wr_ref[...], preferred_element_type=F32)
    lane = lax.broadcasted_iota(jnp.int32, logits.shape, 1)
    big = jnp.int32(LANES)
    is_g = lane < N_GROUPS
    gmax = jnp.max(jnp.where(is_g, logits, -jnp.inf), axis=-1, keepdims=True)
    g_sel = jnp.min(jnp.where(jnp.logical_and(is_g, logits == gmax), lane, big), axis=-1, keepdims=True)
    g_w = 1.0 / jnp.sum(jnp.where(is_g, jnp.exp(logits - gmax), 0.0), axis=-1, keepdims=True)
    lo_lane = N_GROUPS + g_sel * EXPERTS_PER_GROUP
    in_g = jnp.logical_and(lane >= lo_lane, lane < lo_lane + EXPERTS_PER_GROUP)
    v0 = jnp.max(jnp.where(in_g, logits, -jnp.inf), axis=-1, keepdims=True)
    i0 = jnp.min(jnp.where(jnp.logical_and(in_g, logits == v0), lane, big), axis=-1, keepdims=True)
    rest = jnp.logical_and(in_g, lane != i0)
    v1 = jnp.max(jnp.where(rest, logits, -jnp.inf), axis=-1, keepdims=True)
    i1 = jnp.min(jnp.where(jnp.logical_and(rest, logits == v1), lane, big), axis=-1, keepdims=True)
    e1 = jnp.exp(v1 - v0)
    w0 = g_w / (1.0 + e1)
    w1 = g_w * e1 / (1.0 + e1)
    swap = i1 < i0
    ea = jnp.where(swap, i1, i0) - N_GROUPS
    eb = jnp.where(swap, i0, i1) - N_GROUPS
    wa = jnp.where(swap, w1, w0)
    wb = jnp.where(swap, w0, w1)
    ml = lax.broadcasted_iota(jnp.int32, ids_ref.shape, 1)
    ids_ref[...] = jnp.where(ml == 0, ea, jnp.where(ml == 1, eb, 0))
    hn_ref[:, d:] = jnp.where(lane == 0, wa, jnp.where(lane == 1, wb, 0.0))


def _out_call(lay, mix, xp, xs, mod3, w_out, gn, w_router, tm):
    t, d = lay.t, xp.shape[1]
    row = lambda i: (i, 0)
    const = lambda i: (0, 0)
    meta = LANES
    return pl.pallas_call(
        functools.partial(_out_kernel, nt0=lay.t0 // tm),
        grid=(t // tm,),
        in_specs=lay.split_specs(tm, d) + [
            pl.BlockSpec((tm, mix.shape[1]), row),
            pl.BlockSpec((1, 6, d), lambda i: (lay.batch_of_tile(i, tm), 0, 0)),
            pl.BlockSpec((d, d), const),
            pl.BlockSpec((1, d), const),
            pl.BlockSpec((d, LANES), const),
        ],
        out_specs=[pl.BlockSpec((tm, d), row), pl.BlockSpec((tm, d + LANES), row),
                   pl.BlockSpec((tm, meta), row)],
        out_shape=[jax.ShapeDtypeStruct((t, d), F32), jax.ShapeDtypeStruct((t, d + LANES), F32),
                   jax.ShapeDtypeStruct((t, meta), jnp.int32)],
        compiler_params=_params(("arbitrary",)),
        name="out_proj",
    )(xp, xs, mix, mod3, w_out, gn, w_router)


def _bucket_onehot(ids_ref):
    ids = ids_ref[...]
    ea = ids[:, 0:1]
    eb = ids[:, 1:2]
    la = ea & (EXPERTS_PER_GROUP - 1)
    lb = eb & (EXPERTS_PER_GROUP - 1)
    pair = ((la * (2 * EXPERTS_PER_GROUP - 1 - la)) >> 1) + (lb - la - 1)
    bucket = (ea >> (EXPERTS_PER_GROUP.bit_length() - 1)) * len(_PAIRS) + pair
    lane = lax.broadcasted_iota(jnp.int32, (ids.shape[0], LANES), 1)
    return lane == bucket


def _positions_kernel(ids_ref, pos_ref, counts_ref, run_ref, base_ref, start_ref, earlier_ref, *, tm):
    p = pl.program_id(0)
    i = pl.program_id(1)
    tt = ids_ref.shape[0]
    onehot = _bucket_onehot(ids_ref)

    @pl.when(jnp.logical_and(p == 0, i == 0))
    def _():
        run_ref[...] = jnp.zeros(run_ref.shape, F32)

    @pl.when(p == 0)
    def _():
        base_ref[pl.ds(i, 1), :] = run_ref[...]
        run_ref[...] += jnp.sum(jnp.where(onehot, 1.0, 0.0), axis=0, keepdims=True)

    @pl.when(jnp.logical_and(p == 1, i == 0))
    def _():
        counts = run_ref[...]
        counts_ref[...] = counts
        tiles = jnp.floor((counts + (tm - 1)) * (1.0 / tm)).astype(BF16)
        a = lax.broadcasted_iota(jnp.int32, (LANES, LANES), 0)
        b = lax.broadcasted_iota(jnp.int32, (LANES, LANES), 1)
        before = jnp.where(a < b, 1.0, 0.0).astype(BF16)
        start_ref[...] = jnp.dot(tiles, before, preferred_element_type=F32) * tm
        r = lax.broadcasted_iota(jnp.int32, (tt, tt), 0)
        c = lax.broadcasted_iota(jnp.int32, (tt, tt), 1)
        earlier_ref[...] = jnp.where(c < r, 1.0, 0.0).astype(BF16)

    @pl.when(p == 1)
    def _():
        oh = jnp.where(onehot, 1.0, 0.0)
        rank = jnp.dot(earlier_ref[...], oh.astype(BF16), preferred_element_type=F32)
        val = start_ref[...] + base_ref[pl.ds(i, 1), :] + rank
        pos = jnp.sum(oh * val, axis=1, keepdims=True)
        pos_ref[...] = jnp.broadcast_to(pos, pos_ref.shape).astype(jnp.int32)


def _positions_call(ids, tm, tt):
    t, meta = ids.shape
    nt = t // tt
    assert t // tm + 1 <= 256 and tm & (tm - 1) == 0
    pos, counts = pl.pallas_call(
        functools.partial(_positions_kernel, tm=tm),
        grid=(2, nt),
        in_specs=[pl.BlockSpec((tt, meta), lambda p, i: (i, 0))],
        out_specs=[pl.BlockSpec((tt, meta), lambda p, i: (i * p, 0)),
                   pl.BlockSpec((1, LANES), lambda p, i: (0, 0))],
        out_shape=[jax.ShapeDtypeStruct((t, meta), jnp.int32), jax.ShapeDtypeStruct((1, LANES), F32)],
        scratch_shapes=[pltpu.VMEM((1, LANES), F32), pltpu.VMEM((nt, LANES), F32), pltpu.VMEM((1, LANES), F32),
                        pltpu.VMEM((tt, tt), BF16)],
        compiler_params=_params(("arbitrary", "arbitrary")),
        name="positions",
    )(ids)
    return pos[:, 0], counts[0, :N_BUCKETS].astype(jnp.int32)


def _route_plan(ids, tm, tt):
    t = ids.shape[0]
    pos, counts = _positions_call(ids, tm, tt)
    tiles = (counts + tm - 1) // tm
    tile_end = jnp.cumsum(tiles)
    tile_start = tile_end - tiles
    assert t % tm == 0
    n_tiles = t // tm + N_BUCKETS
    src = jnp.zeros((n_tiles * tm,), jnp.int32).at[pos].set(jnp.arange(t, dtype=jnp.int32))
    tile_ids = jnp.arange(n_tiles, dtype=jnp.int32)
    used = tile_end[-1]
    last = jnp.minimum(tile_ids, used - 1)
    tile_bucket = jnp.sum((tile_end[None, :] <= last[:, None]).astype(jnp.int32), axis=1)
    pairs = np.array(_PAIRS, np.int32)
    tgrp = tile_bucket // len(_PAIRS)
    tpair = tile_bucket % len(_PAIRS)
    tile_ea = tgrp * EXPERTS_PER_GROUP + jnp.take(jnp.asarray(pairs[:, 0]), tpair)
    tile_eb = tgrp * EXPERTS_PER_GROUP + jnp.take(jnp.asarray(pairs[:, 1]), tpair)
    rows_left = jnp.take(counts, tile_bucket) - (tile_ids - jnp.take(tile_start, tile_bucket)) * tm
    tile_rows = jnp.where(tile_ids < used, jnp.clip(rows_left, 0, tm), 0).astype(jnp.int32)
    return pos.astype(jnp.int32), src, tile_ea.astype(jnp.int32), tile_eb.astype(jnp.int32), tile_rows, used


def _row_copy(src_hbm, row, buf, slot, r, sem):
    return pltpu.make_async_copy(src_hbm.at[pl.ds(row, 1), :], buf.at[slot, pl.ds(r, 1), :], sem.at[slot])


def _gather_start(idx_ref, base, rows, src_hbm, buf, slot, sem):
    for r in range(rows):
        _row_copy(src_hbm, idx_ref[base + r], buf, slot, r, sem).start()


def _gather_wait(rows, src_hbm, buf, slot, sem):
    pltpu.make_async_copy(src_hbm.at[pl.ds(0, rows), :], buf.at[slot], sem.at[slot]).wait()


def _moe_kernel(src_ref, ea_ref, eb_ref, rows_ref, h_hbm, w1a_ref, w3a_ref, w2a_ref, w1b_ref, w3b_ref, w2b_ref,
                y_ref, buf, sem, *, tm):
    i = pl.program_id(0)
    nt = pl.num_programs(0)
    depth = buf.shape[0]
    ahead = depth - 1
    slot = i % depth
    used = rows_ref[i] > 0
    d = y_ref.shape[1]

    @pl.when(i == 0)
    def _():
        for j in range(ahead):
            _gather_start(src_ref, j * tm, tm, h_hbm, buf, j, sem)

    @pl.when(jnp.logical_and(used, i + ahead < nt))
    def _():
        _gather_start(src_ref, (i + ahead) * tm, tm, h_hbm, buf, (i + ahead) % depth, sem)

    @pl.when(jnp.logical_or(i < ahead, rows_ref[jnp.maximum(i - ahead, 0)] > 0))
    def _():
        _gather_wait(tm, h_hbm, buf, slot, sem)

    @pl.when(jnp.logical_not(used))
    def _():
        y_ref[...] = jnp.zeros(y_ref.shape, y_ref.dtype)

    @pl.when(used)
    def _():
        h = buf[slot, :, :d].astype(BF16)

        def expert(w1_ref, w3_ref, w2_ref):
            a = (jax.nn.silu(jnp.dot(h, w1_ref[0], preferred_element_type=F32))
                 * jnp.dot(h, w3_ref[0], preferred_element_type=F32))
            return jnp.dot(a.astype(BF16), w2_ref[0], preferred_element_type=F32)

        w = buf[slot, :, d:]
        y_ref[...] = (expert(w1a_ref, w3a_ref, w2a_ref) * w[:, 0:1]
                      + expert(w1b_ref, w3b_ref, w2b_ref) * w[:, 1:2])


_MOE_GATHER_DEPTH = 4


def _moe_call(src, tile_ea, tile_eb, tile_rows, hn, w1, w3, w2, tm):
    n_tiles = tile_ea.shape[0]
    d = w1.shape[1]
    f = w1.shape[2]
    wa = lambda i, src, ea, eb, rows: (ea[i], 0, 0)
    wb = lambda i, src, ea, eb, rows: (eb[i], 0, 0)
    grid_spec = pltpu.PrefetchScalarGridSpec(
        num_scalar_prefetch=4,
        grid=(n_tiles,),
        in_specs=[
            pl.BlockSpec(memory_space=pl.ANY),
            pl.BlockSpec((1, d, f), wa), pl.BlockSpec((1, d, f), wa), pl.BlockSpec((1, f, d), wa),
            pl.BlockSpec((1, d, f), wb), pl.BlockSpec((1, d, f), wb), pl.BlockSpec((1, f, d), wb),
        ],
        out_specs=pl.BlockSpec((tm, d), lambda i, *_: (i, 0)),
        scratch_shapes=[pltpu.VMEM((_MOE_GATHER_DEPTH, tm, hn.shape[1]), F32),
                        pltpu.SemaphoreType.DMA((_MOE_GATHER_DEPTH,))],
    )
    assert n_tiles >= _MOE_GATHER_DEPTH
    return pl.pallas_call(
        functools.partial(_moe_kernel, tm=tm),
        grid_spec=grid_spec,
        out_shape=jax.ShapeDtypeStruct((n_tiles * tm, d), F32),
        compiler_params=_params(("arbitrary",)),
        name="moe",
    )(src, tile_ea, tile_eb, tile_rows, hn, w1, w3, w2, w1, w3, w2)


def _combine_kernel(pos_ref, y_hbm, x1_ref, mod_ref, op_ref, os_ref, buf, sem, *, tm, nt0):
    i = pl.program_id(0)
    nt = pl.num_programs(0)
    slot = i % 2

    @pl.when(i == 0)
    def _():
        _gather_start(pos_ref, 0, tm, y_hbm, buf, 0, sem)

    _gather_wait(tm, y_hbm, buf, slot, sem)

    @pl.when(i + 1 < nt)
    def _():
        _gather_start(pos_ref, (i + 1) * tm, tm, y_hbm, buf, 1 - slot, sem)

    out = x1_ref[...] + mod_ref[0, 5:6, :] * buf[slot]

    @pl.when(i < nt0)
    def _():
        op_ref[...] = out

    @pl.when(i >= nt0)
    def _():
        os_ref[...] = out


def _combine_call(lay, pos, y_sorted, x1, mod3, tm):
    t, d = x1.shape
    grid_spec = pltpu.PrefetchScalarGridSpec(
        num_scalar_prefetch=1,
        grid=(t // tm,),
        in_specs=[
            pl.BlockSpec(memory_space=pl.ANY),
            pl.BlockSpec((tm, d), lambda i, pos: (i, 0)),
            pl.BlockSpec((1, 6, d), lambda i, pos: (lay.batch_of_tile(i, tm), 0, 0)),
        ],
        out_specs=lay.split_specs(tm, d),
        scratch_shapes=[pltpu.VMEM((2, tm, d), F32), pltpu.SemaphoreType.DMA((2,))],
    )
    return pl.pallas_call(
        functools.partial(_combine_kernel, tm=tm, nt0=lay.t0 // tm),
        grid_spec=grid_spec,
        out_shape=lay.split_shapes(d, F32),
        compiler_params=_params(("arbitrary",)),
        name="combine",
    )(pos, y_sorted, x1, mod3)


def _rope_tables(n):
    n_rows = n // GRID_W
    rows = jnp.repeat(jnp.arange(n_rows), GRID_W).astype(F32)
    cols = jnp.tile(jnp.arange(GRID_W), n_rows).astype(F32)
    half = HEAD_DIM // 2
    inv = ROPE_THETA ** (-jnp.arange(0, half, 2, dtype=F32) / half)
    ang = jnp.concatenate([rows[:, None] * inv, cols[:, None] * inv], axis=-1)
    sign = jnp.tile(jnp.array([-1.0, 1.0], F32), half)
    return jnp.repeat(jnp.cos(ang), 2, axis=-1), jnp.repeat(jnp.sin(ang), 2, axis=-1) * sign


def _lambda_init(layer_idx):
    return 0.8 - 0.6 * math.exp(-0.3 * layer_idx)


def _tiled_gain(g):
    return jnp.tile(g, LANES // g.shape[-1]).reshape(1, LANES).astype(F32)


def kernel(x_prompt, x_sample, c_prompt, c_sample, rel_bias, w_ada, b_ada, g_norm_mix, w_in, g_qa, g_ka,
           g_qb, g_kb, lam_qk, g_subln, w_fourier, w_out, g_norm_ffn, w_group, w_expert, w1, w3, w2):
    b0, n0, d = x_prompt.shape
    b1, n1, _ = x_sample.shape
    depth = w_in.shape[0]
    lay = _Layout(b0, n0, b1, n1)
    trunks = ((0, b0, n0), (lay.t0, b1, n1))
    n_max = max(n0, n1)

    tm = _pick(math.gcd(n0, n1), 512)
    tq_a = _pick(math.gcd(n0, n1), 256)
    tk_a = _pick(math.gcd(n0, n1), 2048)
    t_b = _pick(math.gcd(n0, n1), 512)
    tm_moe = 256

    xp = x_prompt.reshape(b0 * n0, d)
    xs = x_sample.reshape(b1 * n1, d)
    nb = b0 + b1
    bp = -(-nb // 8) * 8
    c_all = jnp.zeros((bp, d), F32).at[:nb].set(jnp.concatenate([c_prompt, c_sample], axis=0))
    mod = _ada_call(c_all, w_ada, b_ada).reshape(depth, bp, 6, d)

    cos_e, sin_e = _rope_tables(n_max)
    chunk_ratio = {n: 2 if n // t_b > 4 else 1 for n in (n0, n1)}
    bands = {c: _bias_tables(rel_bias, t_b, c) for c in sorted(set(chunk_ratio.values()))}
    dft = {n: _dft_stage_tables(n) for n in sorted({n0, n1})}
    cidx = jnp.arange(C_GROUP_W, dtype=jnp.int32)
    ang_c = ((cidx[:, None] * cidx[None, :]) % C_GROUP_W).astype(F32) * (2.0 * math.pi / C_GROUP_W)
    dft_c = (jnp.concatenate([jnp.cos(ang_c), jnp.sin(ang_c)], axis=1) * C_GROUP_W ** -0.5).astype(BF16)

    for l in range(depth):
        mod3 = mod[l]
        lam_init = _lambda_init(l)
        qa, ka, va, qb, kb, vb, p, q = _proj_call(
            lay, xp, xs, mod3, g_norm_mix[l].reshape(1, d), w_in[l].astype(BF16),
            _tiled_gain(g_qa[l]), _tiled_gain(g_ka[l]), _tiled_gain(g_qb[l]), _tiled_gain(g_kb[l]),
            cos_e, sin_e, dft_c, tm)

        wf = w_fourier[l].astype(BF16)
        mix = jnp.zeros((lay.t, A_Q + B_WIDTH + C_WIDTH), BF16)
        for tok0, b, n in trunks:
            mix = _attn_a_call(qa, ka, va, mix, tok0, b, n, tq_a, min(tk_a, max(n // 2, LANES)))
            mix = _attn_b_call(qb, kb, vb, bands[chunk_ratio[n]], lam_qk[l], _tiled_gain(g_subln[l]), mix,
                               tok0, b, n, t_b, lam_init)
            mix = _fourier_call(dft[n][0], dft[n][1], p, q, wf, mix, tok0, b, n)

        w_router = jnp.zeros((d, LANES), F32).at[:, :N_GROUPS].set(w_group[l])
        w_router = w_router.at[:, N_GROUPS:N_GROUPS + N_EXPERTS].set(w_expert[l]).astype(BF16)
        x1, hn2, ids = _out_call(lay, mix, xp, xs, mod3, w_out[l].astype(BF16),
                                 g_norm_ffn[l].reshape(1, d), w_router, tm)

        pos, src, tile_ea, tile_eb, tile_rows, _ = _route_plan(ids, tm_moe, _pick(math.gcd(n0, n1), 2048))
        y_sorted = _moe_call(src, tile_ea, tile_eb, tile_rows, hn2, w1[l].astype(BF16), w3[l].astype(BF16),
                             w2[l].astype(BF16), tm_moe)
        xp, xs = _combine_call(lay, pos, y_sorted, x1, mod3, tm)

    return (xp.reshape(b0, n0, d), xs.reshape(b1, n1, d))
```

```python
import functools
import math

import jax
import jax.numpy as jnp
import numpy as np
from jax import lax
from jax.experimental import pallas as pl
from jax.experimental.pallas import tpu as pltpu

F32 = jnp.float32
BF16 = jnp.bfloat16

D_MODEL = 2048
HEAD_DIM = 128
A_HEADS = 8
A_KV_HEADS = 2
A_GROUP = A_HEADS // A_KV_HEADS
A_Q = A_HEADS * HEAD_DIM
A_KV = A_KV_HEADS * HEAD_DIM
B_HEADS = 4
B_HALF = HEAD_DIM // 2
B_WIDTH = B_HEADS * HEAD_DIM
C_WIDTH = 512
C_GROUPS = 4
C_GROUP_W = 128
IN_WIDTH = A_Q + 2 * A_KV + 3 * B_WIDTH + C_WIDTH
GRID_W = 64
ROPE_THETA = 10000.0
NUM_BUCKETS = 32
MAX_DISTANCE = 128
N_GROUPS = 4
EXPERTS_PER_GROUP = 4
N_EXPERTS = 16
D_FF_EXPERT = 512
EPS = 1e-6
LOG2E = 1.4426950408889634

_PAIRS = ((0, 1), (0, 2), (0, 3), (1, 2), (1, 3), (2, 3))
N_BUCKETS = N_GROUPS * len(_PAIRS)

V7X_VMEM_BYTES = 64 * 1024 * 1024
VMEM_LIMIT = V7X_VMEM_BYTES - 8 * 1024 * 1024
LANES = 128
NEG_BIG = -1e30


def _params(sem, **kw):
    return pltpu.CompilerParams(dimension_semantics=sem, vmem_limit_bytes=VMEM_LIMIT, **kw)

def _pick(total, pref):
    t = min(pref, total)
    while total % t:
        t //= 2
    return t


def _ada_kernel(c_ref, w_ref, b_ref, o_ref):
    h = jax.nn.silu(c_ref[...]).astype(BF16)
    o_ref[0] = jnp.dot(h, w_ref[0].astype(BF16), preferred_element_type=F32) + b_ref[0]


def _ada_call(c_all, w_ada, b_ada):
    depth, d, e = w_ada.shape
    bp = c_all.shape[0]
    tn = _pick(e, 1024)
    return pl.pallas_call(
        _ada_kernel,
        grid=(depth, e // tn),
        in_specs=[
            pl.BlockSpec((bp, d), lambda l, j: (0, 0)),
            pl.BlockSpec((1, d, tn), lambda l, j: (l, 0, j)),
            pl.BlockSpec((1, 1, tn), lambda l, j: (l, 0, j)),
        ],
        out_specs=pl.BlockSpec((1, bp, tn), lambda l, j: (l, 0, j)),
        out_shape=jax.ShapeDtypeStruct((depth, bp, e), F32),
        compiler_params=_params(("arbitrary", "arbitrary")),
        name="ada",
    )(c_all, w_ada, b_ada.reshape(depth, 1, e))


class _Layout:
    def __init__(self, b0, n0, b1, n1):
        self.b = (b0, b1)
        self.n = (n0, n1)
        self.t0 = b0 * n0
        self.t = b0 * n0 + b1 * n1

    def batch_of_tile(self, i, tm):
        tok = i * tm
        return jnp.where(tok < self.t0, tok // self.n[0], self.b[0] + (tok - self.t0) // self.n[1])

    def pos_block_of_tile(self, i, tm):
        tok = i * tm
        pos = jnp.where(tok < self.t0, tok % self.n[0], (tok - self.t0) % self.n[1])
        return pos // tm

    def split_specs(self, tm, d):
        nt0 = self.t0 // tm
        return [pl.BlockSpec((tm, d), lambda i, *_: (jnp.minimum(i, nt0 - 1), 0)),
                pl.BlockSpec((tm, d), lambda i, *_: (jnp.maximum(i - nt0, 0), 0))]

    def split_shapes(self, d, dtype):
        return [jax.ShapeDtypeStruct((self.t0, d), dtype), jax.ShapeDtypeStruct((self.t - self.t0, d), dtype)]


def _split_apply(xp_ref, xs_ref, nt0, body, *rest):
    i = pl.program_id(0)

    @pl.when(i < nt0)
    def _():
        body(xp_ref, *rest)

    @pl.when(i >= nt0)
    def _():
        body(xs_ref, *rest)


def _head_norm(z, g):
    return z * lax.rsqrt(jnp.mean(z * z, axis=-1, keepdims=True) + EPS) * g


def _half_norm(z, g, lo):
    zz = z * z
    s_lo = jnp.sum(jnp.where(lo, zz, 0.0), axis=-1, keepdims=True)
    s_hi = jnp.sum(jnp.where(lo, 0.0, zz), axis=-1, keepdims=True)
    inv = jnp.where(lo, lax.rsqrt(s_lo / B_HALF + EPS), lax.rsqrt(s_hi / B_HALF + EPS))
    return z * inv * g


def _rope(z, c, s_signed, even):
    partner = jnp.where(even, pltpu.roll(z, LANES - 1, 1), pltpu.roll(z, 1, 1))
    return z * c + partner * s_signed


def _proj_kernel(xp_ref, xs_ref, *rest, nt0):
    _split_apply(xp_ref, xs_ref, nt0, _proj_body, *rest)


def _proj_body(x_ref, mod_ref, gn_ref, w_ref, gqa_ref, gka_ref, gqb_ref, gkb_ref, cos_ref, sin_ref,
               dft_ref, qa_ref, ka_ref, va_ref, qb_ref, kb_ref, vb_ref, p_ref, q_ref):
    x = x_ref[...]
    y = x * lax.rsqrt(jnp.mean(x * x, axis=-1, keepdims=True) + EPS) * gn_ref[...]
    hn = (y * (1.0 + mod_ref[0, 1:2, :]) + mod_ref[0, 0:1, :]).astype(BF16)

    def seg(a, b):
        return jnp.dot(hn, w_ref[:, a:b], preferred_element_type=F32)

    tm = x.shape[0]
    lane = lax.broadcasted_iota(jnp.int32, (tm, LANES), 1)
    even = (lane % 2) == 0
    lo = lane < B_HALF
    cos = cos_ref[...]
    sin = sin_ref[...]

    scale_a = HEAD_DIM ** -0.5 * LOG2E
    z = seg(0, A_Q)
    for h in range(A_HEADS):
        sl = slice(h * HEAD_DIM, (h + 1) * HEAD_DIM)
        qa_ref[:, sl] = (_rope(_head_norm(z[:, sl], gqa_ref[...]), cos, sin, even) * scale_a).astype(BF16)
    off = A_Q
    z = seg(off, off + A_KV)
    for h in range(A_KV_HEADS):
        sl = slice(h * HEAD_DIM, (h + 1) * HEAD_DIM)
        ka_ref[:, sl] = _rope(_head_norm(z[:, sl], gka_ref[...]), cos, sin, even).astype(BF16)
    off += A_KV
    va_ref[...] = seg(off, off + A_KV).astype(BF16)
    off += A_KV

    scale_b = B_HALF ** -0.5 * LOG2E
    z = seg(off, off + B_WIDTH)
    for h in range(B_HEADS):
        sl = slice(h * HEAD_DIM, (h + 1) * HEAD_DIM)
        qb_ref[:, sl] = (_half_norm(z[:, sl], gqb_ref[...], lo) * scale_b).astype(BF16)
    off += B_WIDTH
    z = seg(off, off + B_WIDTH)
    for h in range(B_HEADS):
        sl = slice(h * HEAD_DIM, (h + 1) * HEAD_DIM)
        kb_ref[:, sl] = _half_norm(z[:, sl], gkb_ref[...], lo).astype(BF16)
    off += B_WIDTH
    vb_ref[...] = seg(off, off + B_WIDTH).astype(BF16)
    off += B_WIDTH

    z = seg(off, off + C_WIDTH).astype(BF16)
    for g in range(C_GROUPS):
        sl = slice(g * C_GROUP_W, (g + 1) * C_GROUP_W)
        pq = jnp.dot(z[:, sl], dft_ref[...], preferred_element_type=F32)
        p_ref[:, sl] = pq[:, :C_GROUP_W]
        q_ref[:, sl] = pq[:, C_GROUP_W:]


def _proj_call(lay, xp, xs, mod3, gn, w_in, gqa, gka, gqb, gkb, cos_e, sin_e, dft_c, tm):
    t, d = lay.t, xp.shape[1]
    row = lambda i: (i, 0)
    const = lambda i: (0, 0)
    widths = (A_Q, A_KV, A_KV, B_WIDTH, B_WIDTH, B_WIDTH, C_WIDTH, C_WIDTH)
    return pl.pallas_call(
        functools.partial(_proj_kernel, nt0=lay.t0 // tm),
        grid=(t // tm,),
        in_specs=lay.split_specs(tm, d) + [
            pl.BlockSpec((1, 6, d), lambda i: (lay.batch_of_tile(i, tm), 0, 0)),
            pl.BlockSpec((1, d), const),
            pl.BlockSpec((d, IN_WIDTH), const),
            pl.BlockSpec((1, LANES), const),
            pl.BlockSpec((1, LANES), const),
            pl.BlockSpec((1, LANES), const),
            pl.BlockSpec((1, LANES), const),
            pl.BlockSpec((tm, LANES), lambda i: (lay.pos_block_of_tile(i, tm), 0)),
            pl.BlockSpec((tm, LANES), lambda i: (lay.pos_block_of_tile(i, tm), 0)),
            pl.BlockSpec((C_GROUP_W, 2 * C_GROUP_W), const),
        ],
        out_specs=[pl.BlockSpec((tm, w), row) for w in widths],
        out_shape=[jax.ShapeDtypeStruct((t, w), F32 if i >= 6 else BF16) for i, w in enumerate(widths)],
        compiler_params=_params(("arbitrary",)),
        name="proj",
    )(xp, xs, mod3, gn, w_in, gqa, gka, gqb, gkb, cos_e, sin_e, dft_c)


def _softmax_step(s, vc, m_ref, l_ref, acc_ref, rows=slice(None)):
    m_prev = m_ref[rows, :]
    m_new = jnp.maximum(m_prev, jnp.max(s, axis=1, keepdims=True))
    alpha = jnp.exp2(m_prev - m_new)
    p = jnp.exp2(s - jnp.tile(m_new, (1, s.shape[1] // LANES)))
    l_ref[rows, :] = alpha * l_ref[rows, :] + jnp.sum(p, axis=1, keepdims=True)
    acc_ref[rows, :] = alpha * acc_ref[rows, :] + jnp.dot(p.astype(BF16), vc, preferred_element_type=F32)
    m_ref[rows, :] = m_new


_NT = (((1,), (1,)), ((), ()))


def _attn_a_kernel(q_ref, k_ref, v_ref, _merged_ref, o_ref, qs_ref, m_ref, vx_ref, acc_ref, *, tq, tk, n):
    @pl.when(pl.program_id(2) == 0)
    def _():
        vx_ref[:, :HEAD_DIM] = v_ref[...]
        vx_ref[:, HEAD_DIM:] = jnp.ones((n, LANES), vx_ref.dtype)

    for g in range(A_GROUP):
        qs_ref[g * tq:(g + 1) * tq, :] = q_ref[:, g * HEAD_DIM:(g + 1) * HEAD_DIM]
    m_ref[...] = jnp.full(m_ref.shape, NEG_BIG, F32)
    acc_ref[...] = jnp.zeros(acc_ref.shape, F32)

    def body(j, carry):
        rows = pl.ds(pl.multiple_of(j * tk, tk), tk)
        s = lax.dot_general(qs_ref[...], k_ref[rows, :], _NT, preferred_element_type=F32)
        m_prev = m_ref[...]
        m_new = jnp.maximum(m_prev, jnp.max(s, axis=1, keepdims=True))
        alpha = jnp.exp2(m_prev - m_new)
        p = jnp.exp2(s - jnp.tile(m_new, (1, tk // LANES))).astype(BF16)
        acc_ref[...] = (jnp.tile(alpha, (1, 2)) * acc_ref[...]
                        + jnp.dot(p, vx_ref[rows, :], preferred_element_type=F32))
        m_ref[...] = m_new
        return carry

    lax.fori_loop(0, n // tk, body, 0, unroll=True)
    o = acc_ref[:, :HEAD_DIM] / acc_ref[:, HEAD_DIM:]
    for g in range(A_GROUP):
        o_ref[:, g * HEAD_DIM:(g + 1) * HEAD_DIM] = o[g * tq:(g + 1) * tq].astype(o_ref.dtype)


def _attn_a_call(qa, ka, va, merged, tok0, b, n, tq, tk):
    qblk0 = tok0 // tq
    kblk0 = tok0 // n
    nq = n // tq
    rows = A_GROUP * tq
    return pl.pallas_call(
        functools.partial(_attn_a_kernel, tq=tq, tk=tk, n=n),
        grid=(b, A_KV_HEADS, nq),
        in_specs=[
            pl.BlockSpec((tq, A_GROUP * HEAD_DIM), lambda bi, kv, qi: (qblk0 + bi * nq + qi, kv)),
            pl.BlockSpec((n, HEAD_DIM), lambda bi, kv, qi: (kblk0 + bi, kv)),
            pl.BlockSpec((n, HEAD_DIM), lambda bi, kv, qi: (kblk0 + bi, kv)),
            pl.BlockSpec(memory_space=pl.ANY),
        ],
        out_specs=pl.BlockSpec((tq, A_GROUP * HEAD_DIM), lambda bi, kv, qi: (qblk0 + bi * nq + qi, kv)),
        out_shape=jax.ShapeDtypeStruct(merged.shape, merged.dtype),
        input_output_aliases={3: 0},
        scratch_shapes=[
            pltpu.VMEM((rows, HEAD_DIM), BF16),
            pltpu.VMEM((rows, LANES), F32),
            pltpu.VMEM((n, HEAD_DIM + LANES), BF16),
            pltpu.VMEM((rows, HEAD_DIM + LANES), F32),
        ],
        compiler_params=_params(("arbitrary", "arbitrary", "arbitrary")),
        name="attn_a",
    )(qa, ka, va, merged)


def _attn_b_kernel(q_ref, k_ref, v_ref, band_ref, lam_ref, gs_ref, _merged_ref, o_ref,
                   qs_ref, m_ref, l_ref, acc_ref, *, t, ck, nchunks, lam_init):
    i = pl.program_id(2)
    c = ck // t
    q = q_ref[...]
    lo = lax.broadcasted_iota(jnp.int32, q.shape, 1) < B_HALF
    zero = jnp.zeros_like(q)
    qs_ref[0:t, :] = jnp.where(lo, q, zero)
    qs_ref[t:2 * t, :] = jnp.where(lo, zero, q)
    m_ref[...] = jnp.full(m_ref.shape, NEG_BIG, F32)
    l_ref[...] = jnp.zeros(l_ref.shape, F32)
    acc_ref[...] = jnp.zeros(acc_ref.shape, F32)

    def body(j, carry):
        rows = pl.ds(pl.multiple_of(j * ck, ck), ck)
        s = lax.dot_general(qs_ref[...], k_ref[rows, :], _NT, preferred_element_type=F32)
        lo_off, hi_off = _band_offsets(c)
        bias = band_ref[0, jnp.clip(c * j - i, lo_off, hi_off) - lo_off]
        for comp in range(2):
            part = slice(comp * t, (comp + 1) * t)
            _softmax_step(s[part] + bias, v_ref[rows, :], m_ref, l_ref, acc_ref, part)
        return carry

    lax.fori_loop(0, nchunks, body, 0, unroll=True)
    o = acc_ref[...] / l_ref[...]
    lq = lam_ref[...]
    lam = (jnp.exp(jnp.sum(lq[0:1] * lq[1:2], axis=-1, keepdims=True))
           - jnp.exp(jnp.sum(lq[2:3] * lq[3:4], axis=-1, keepdims=True)) + lam_init)
    ob = o[0:t] - lam * o[t:2 * t]
    o_ref[...] = (_head_norm(ob, gs_ref[...]) * (1.0 - lam_init)).astype(o_ref.dtype)


def _attn_b_call(qb, kb, vb, band, lam_qk, g_subln, merged, tok0, b, n, t, lam_init):
    qblk0 = tok0 // t
    kblk0 = tok0 // n
    nq = n // t
    ck = band.shape[3]
    return pl.pallas_call(
        functools.partial(_attn_b_kernel, t=t, ck=ck, nchunks=n // ck, lam_init=lam_init),
        grid=(B_HEADS, b, nq),
        in_specs=[
            pl.BlockSpec((t, HEAD_DIM), lambda h, bi, qi: (qblk0 + bi * nq + qi, h)),
            pl.BlockSpec((n, HEAD_DIM), lambda h, bi, qi: (kblk0 + bi, h)),
            pl.BlockSpec((n, HEAD_DIM), lambda h, bi, qi: (kblk0 + bi, h)),
            pl.BlockSpec((1,) + band.shape[1:], lambda h, bi, qi: (h, 0, 0, 0), pipeline_mode=pl.Buffered(1)),
            pl.BlockSpec((4, B_HALF), lambda h, bi, qi: (0, 0)),
            pl.BlockSpec((1, LANES), lambda h, bi, qi: (0, 0)),
            pl.BlockSpec(memory_space=pl.ANY),
        ],
        out_specs=pl.BlockSpec((t, HEAD_DIM), lambda h, bi, qi: (qblk0 + bi * nq + qi, A_Q // HEAD_DIM + h)),
        out_shape=jax.ShapeDtypeStruct(merged.shape, merged.dtype),
        input_output_aliases={6: 0},
        scratch_shapes=[
            pltpu.VMEM((2 * t, HEAD_DIM), BF16),
            pltpu.VMEM((2 * t, LANES), F32),
            pltpu.VMEM((2 * t, LANES), F32),
            pltpu.VMEM((2 * t, HEAD_DIM), F32),
        ],
        compiler_params=_params(("arbitrary", "arbitrary", "arbitrary")),
        name="attn_b",
    )(qb, kb, vb, band, lam_qk, g_subln, merged)


def _t5_bucket(rel):
    nb = NUM_BUCKETS // 2
    ret = (rel > 0).astype(jnp.int32) * nb
    n = jnp.abs(rel)
    max_exact = nb // 2
    nf = jnp.maximum(n, 1).astype(F32)
    large = max_exact + (jnp.log(nf / max_exact) / math.log(MAX_DISTANCE / max_exact)
                         * (nb - max_exact)).astype(jnp.int32)
    large = jnp.minimum(large, nb - 1)
    return ret + jnp.where(n < max_exact, n, large)


_T5_SATURATION = math.ceil((NUM_BUCKETS // 4) * (MAX_DISTANCE / (NUM_BUCKETS // 4)) ** (
    (NUM_BUCKETS // 2 - 1 - NUM_BUCKETS // 4) / (NUM_BUCKETS // 2 - NUM_BUCKETS // 4)))


def _band_offsets(c):
    return -(c + 1), 2


def _band_kernel(vec_ref, o_ref, *, t, ck):
    width = vec_ref.shape[2]
    x = jnp.broadcast_to(vec_ref[0], (t, width))
    o_ref[0, 0] = pltpu.roll(x, width - t + 1, 1, stride=1, stride_axis=0)[:, :ck]


def _bias_tables(rel_bias, t, c):
    assert t + 1 >= _T5_SATURATION
    heads = rel_bias.shape[1]
    lo_off, hi_off = _band_offsets(c)
    nd = hi_off - lo_off + 1
    ck = c * t
    width = (c + 2) * t
    rel = ((jnp.arange(nd, dtype=jnp.int32)[:, None] + lo_off) * t
           + jnp.arange(-(t - 1), width - (t - 1), dtype=jnp.int32)[None, :])
    vec = jnp.take(rel_bias, _t5_bucket(rel), axis=0).astype(F32)
    vec = (vec.transpose(2, 0, 1) * LOG2E).reshape(heads * nd, 1, width)
    return pl.pallas_call(
        functools.partial(_band_kernel, t=t, ck=ck),
        grid=(heads, nd),
        in_specs=[pl.BlockSpec((1, 1, width), lambda h, d: (h * nd + d, 0, 0))],
        out_specs=pl.BlockSpec((1, 1, t, ck), lambda h, d: (h, d, 0, 0)),
        out_shape=jax.ShapeDtypeStruct((heads, nd, t, ck), F32),
        compiler_params=_params(("arbitrary", "arbitrary")),
        name="band",
    )(vec)


_DFT_INNER = 64


def _dft_stage_tables(n):
    n2 = min(_DFT_INNER, n // 8)
    n1 = n // n2
    i1 = jnp.arange(n1, dtype=jnp.int32)
    ang1 = ((i1[:, None] * i1[None, :]) % n1).astype(F32) * (2.0 * math.pi / n1)
    c1 = jnp.cos(ang1) * n ** -0.5
    s1 = jnp.sin(ang1) * n ** -0.5
    stage1 = jnp.concatenate([jnp.concatenate([c1, -s1], axis=1), jnp.concatenate([s1, c1], axis=1)], axis=0)
    i2 = jnp.arange(n2, dtype=jnp.int32)
    k = i1[:, None, None] + n1 * i2[None, :, None]
    ang2 = ((k * i2[None, None, :]) % n).astype(F32) * (2.0 * math.pi / n)
    stage2 = jnp.concatenate([jnp.cos(ang2), -jnp.sin(ang2)], axis=2)
    return stage1.astype(BF16), stage2.astype(BF16)


def _fourier_kernel(p_ref, q_ref, s1_ref, s2_ref, w_ref, _merged_ref, o_ref, y_ref, f_ref):
    n1, n2 = s2_ref.shape[0], s2_ref.shape[1]
    lanes = p_ref.shape[1]
    cols = 4
    stage1 = s1_ref[...]
    for t2 in range(0, n2, cols):
        zr = [p_ref[pl.ds(t2 + j, n1, stride=n2), :] for j in range(cols)]
        zi = [q_ref[pl.ds(t2 + j, n1, stride=n2), :] for j in range(cols)]
        z = jnp.concatenate([jnp.concatenate(zr, axis=1), jnp.concatenate(zi, axis=1)], axis=0)
        y = jnp.dot(stage1, z.astype(BF16), preferred_element_type=F32)
        for j in range(cols):
            y_ref[0, pl.ds(t2 + j, n1, stride=n2), :] = y[:n1, j * lanes:(j + 1) * lanes]
            y_ref[1, pl.ds(t2 + j, n1, stride=n2), :] = y[n1:, j * lanes:(j + 1) * lanes]
    for k1 in range(n1):
        blk = jnp.concatenate([y_ref[0, k1 * n2:(k1 + 1) * n2, :], y_ref[1, k1 * n2:(k1 + 1) * n2, :]], axis=0)
        f_ref[pl.ds(k1, n2, stride=n1), :] = jnp.dot(s2_ref[k1], blk.astype(BF16), preferred_element_type=F32)
    o_ref[...] = jnp.dot(f_ref[...].astype(BF16), w_ref[0], preferred_element_type=F32).astype(o_ref.dtype)


def _fourier_call(stage1, stage2, p, q, wf, merged, tok0, b, n):
    kblk0 = tok0 // n
    col0 = (A_Q + B_WIDTH) // C_GROUP_W
    return pl.pallas_call(
        _fourier_kernel,
        grid=(b, C_GROUPS),
        in_specs=[
            pl.BlockSpec((n, C_GROUP_W), lambda bi, g: (kblk0 + bi, g)),
            pl.BlockSpec((n, C_GROUP_W), lambda bi, g: (kblk0 + bi, g)),
            pl.BlockSpec(stage1.shape, lambda bi, g: (0, 0)),
            pl.BlockSpec(stage2.shape, lambda bi, g: (0, 0, 0)),
            pl.BlockSpec((1, C_GROUP_W, C_GROUP_W), lambda bi, g: (g, 0, 0)),
            pl.BlockSpec(memory_space=pl.ANY),
        ],
        out_specs=pl.BlockSpec((n, C_GROUP_W), lambda bi, g: (kblk0 + bi, col0 + g)),
        out_shape=jax.ShapeDtypeStruct(merged.shape, merged.dtype),
        input_output_aliases={5: 0},
        scratch_shapes=[pltpu.VMEM((2, n, C_GROUP_W), F32), pltpu.VMEM((n, C_GROUP_W), F32)],
        compiler_params=_params(("arbitrary", "arbitrary")),
        name="fourier",
    )(p, q, stage1, stage2, wf, merged)


def _out_kernel(xp_ref, xs_ref, *rest, nt0):
    _split_apply(xp_ref, xs_ref, nt0, _out_body, *rest)


def _out_body(x_ref, mix_ref, mod_ref, w_ref, gn_ref, wr_ref, x1_ref, hn_ref, ids_ref):
    mix = jnp.dot(mix_ref[...], w_ref[...], preferred_element_type=F32)
    x1 = x_ref[...] + mod_ref[0, 2:3, :] * mix
    x1_ref[...] = x1
    y = x1 * lax.rsqrt(jnp.mean(x1 * x1, axis=-1, keepdims=True) + EPS) * gn_ref[...]
    hn = y * (1.0 + mod_ref[0, 4:5, :]) + mod_ref[0, 3:4, :]
    d = hn.shape[1]
    hn_ref[:, :d] = hn

    logits = jnp.dot(hn.astype(BF16), wr_ref[...], preferred_element_type=F32)
    lane = lax.broadcasted_iota(jnp.int32, logits.shape, 1)
    big = jnp.int32(LANES)
    is_g = lane < N_GROUPS
    gmax = jnp.max(jnp.where(is_g, logits, -jnp.inf), axis=-1, keepdims=True)
    g_sel = jnp.min(jnp.where(jnp.logical_and(is_g, logits == gmax), lane, big), axis=-1, keepdims=True)
    g_w = 1.0 / jnp.sum(jnp.where(is_g, jnp.exp(logits - gmax), 0.0), axis=-1, keepdims=True)
    lo_lane = N_GROUPS + g_sel * EXPERTS_PER_GROUP
    in_g = jnp.logical_and(lane >= lo_lane, lane < lo_lane + EXPERTS_PER_GROUP)
    v0 = jnp.max(jnp.where(in_g, logits, -jnp.inf), axis=-1, keepdims=True)
    i0 = jnp.min(jnp.where(jnp.logical_and(in_g, logits == v0), lane, big), axis=-1, keepdims=True)
    rest = jnp.logical_and(in_g, lane != i0)
    v1 = jnp.max(jnp.where(rest, logits, -jnp.inf), axis=-1, keepdims=True)
    i1 = jnp.min(jnp.where(jnp.logical_and(rest, logits == v1), lane, big), axis=-1, keepdims=True)
    e1 = jnp.exp(v1 - v0)
    w0 = g_w / (1.0 + e1)
    w1 = g_w * e1 / (1.0 + e1)
    swap = i1 < i0
    ea = jnp.where(swap, i1, i0) - N_GROUPS
    eb = jnp.where(swap, i0, i1) - N_GROUPS
    wa = jnp.where(swap, w1, w0)
    wb = jnp.where(swap, w0, w1)
    ml = lax.broadcasted_iota(jnp.int32, ids_ref.shape, 1)
    ids_ref[...] = jnp.where(ml == 0, ea, jnp.where(ml == 1, eb, 0))
    hn_ref[:, d:] = jnp.where(lane == 0, wa, jnp.where(lane == 1, wb, 0.0))


def _out_call(lay, mix, xp, xs, mod3, w_out, gn, w_router, tm):
    t, d = lay.t, xp.shape[1]
    row = lambda i: (i, 0)
    const = lambda i: (0, 0)
    meta = LANES
    return pl.pallas_call(
        functools.partial(_out_kernel, nt0=lay.t0 // tm),
        grid=(t // tm,),
        in_specs=lay.split_specs(tm, d) + [
            pl.BlockSpec((tm, mix.shape[1]), row),
            pl.BlockSpec((1, 6, d), lambda i: (lay.batch_of_tile(i, tm), 0, 0)),
            pl.BlockSpec((d, d), const),
            pl.BlockSpec((1, d), const),
            pl.BlockSpec((d, LANES), const),
        ],
        out_specs=[pl.BlockSpec((tm, d), row), pl.BlockSpec((tm, d + LANES), row),
                   pl.BlockSpec((tm, meta), row)],
        out_shape=[jax.ShapeDtypeStruct((t, d), F32), jax.ShapeDtypeStruct((t, d + LANES), F32),
                   jax.ShapeDtypeStruct((t, meta), jnp.int32)],
        compiler_params=_params(("arbitrary",)),
        name="out_proj",
    )(xp, xs, mix, mod3, w_out, gn, w_router)


def _bucket_onehot(ids_ref):
    ids = ids_ref[...]
    ea = ids[:, 0:1]
    eb = ids[:, 1:2]
    la = ea & (EXPERTS_PER_GROUP - 1)
    lb = eb & (EXPERTS_PER_GROUP - 1)
    pair = ((la * (2 * EXPERTS_PER_GROUP - 1 - la)) >> 1) + (lb - la - 1)
    bucket = (ea >> (EXPERTS_PER_GROUP.bit_length() - 1)) * len(_PAIRS) + pair
    lane = lax.broadcasted_iota(jnp.int32, (ids.shape[0], LANES), 1)
    return lane == bucket


def _positions_kernel(ids_ref, pos_ref, counts_ref, run_ref, base_ref, start_ref, earlier_ref, *, tm):
    p = pl.program_id(0)
    i = pl.program_id(1)
    tt = ids_ref.shape[0]
    onehot = _bucket_onehot(ids_ref)

    @pl.when(jnp.logical_and(p == 0, i == 0))
    def _():
        run_ref[...] = jnp.zeros(run_ref.shape, F32)

    @pl.when(p == 0)
    def _():
        base_ref[pl.ds(i, 1), :] = run_ref[...]
        run_ref[...] += jnp.sum(jnp.where(onehot, 1.0, 0.0), axis=0, keepdims=True)

    @pl.when(jnp.logical_and(p == 1, i == 0))
    def _():
        counts = run_ref[...]
        counts_ref[...] = counts
        tiles = jnp.floor((counts + (tm - 1)) * (1.0 / tm)).astype(BF16)
        a = lax.broadcasted_iota(jnp.int32, (LANES, LANES), 0)
        b = lax.broadcasted_iota(jnp.int32, (LANES, LANES), 1)
        before = jnp.where(a < b, 1.0, 0.0).astype(BF16)
        start_ref[...] = jnp.dot(tiles, before, preferred_element_type=F32) * tm
        r = lax.broadcasted_iota(jnp.int32, (tt, tt), 0)
        c = lax.broadcasted_iota(jnp.int32, (tt, tt), 1)
        earlier_ref[...] = jnp.where(c < r, 1.0, 0.0).astype(BF16)

    @pl.when(p == 1)
    def _():
        oh = jnp.where(onehot, 1.0, 0.0)
        rank = jnp.dot(earlier_ref[...], oh.astype(BF16), preferred_element_type=F32)
        val = start_ref[...] + base_ref[pl.ds(i, 1), :] + rank
        pos = jnp.sum(oh * val, axis=1, keepdims=True)
        pos_ref[...] = jnp.broadcast_to(pos, pos_ref.shape).astype(jnp.int32)


def _positions_call(ids, tm, tt):
    t, meta = ids.shape
    nt = t // tt
    assert t // tm + 1 <= 256 and tm & (tm - 1) == 0
    pos, counts = pl.pallas_call(
        functools.partial(_positions_kernel, tm=tm),
        grid=(2, nt),
        in_specs=[pl.BlockSpec((tt, meta), lambda p, i: (i, 0))],
        out_specs=[pl.BlockSpec((tt, meta), lambda p, i: (i * p, 0)),
                   pl.BlockSpec((1, LANES), lambda p, i: (0, 0))],
        out_shape=[jax.ShapeDtypeStruct((t, meta), jnp.int32), jax.ShapeDtypeStruct((1, LANES), F32)],
        scratch_shapes=[pltpu.VMEM((1, LANES), F32), pltpu.VMEM((nt, LANES), F32), pltpu.VMEM((1, LANES), F32),
                        pltpu.VMEM((tt, tt), BF16)],
        compiler_params=_params(("arbitrary", "arbitrary")),
        name="positions",
    )(ids)
    return pos[:, 0], counts[0, :N_BUCKETS].astype(jnp.int32)


def _route_plan(ids, tm, tt):
    t = ids.shape[0]
    pos, counts = _positions_call(ids, tm, tt)
    tiles = (counts + tm - 1) // tm
    tile_end = jnp.cumsum(tiles)
    tile_start = tile_end - tiles
    assert t % tm == 0
    n_tiles = t // tm + N_BUCKETS
    src = jnp.zeros((n_tiles * tm,), jnp.int32).at[pos].set(jnp.arange(t, dtype=jnp.int32))
    tile_ids = jnp.arange(n_tiles, dtype=jnp.int32)
    used = tile_end[-1]
    last = jnp.minimum(tile_ids, used - 1)
    tile_bucket = jnp.sum((tile_end[None, :] <= last[:, None]).astype(jnp.int32), axis=1)
    pairs = np.array(_PAIRS, np.int32)
    tgrp = tile_bucket // len(_PAIRS)
    tpair = tile_bucket % len(_PAIRS)
    tile_ea = tgrp * EXPERTS_PER_GROUP + jnp.take(jnp.asarray(pairs[:, 0]), tpair)
    tile_eb = tgrp * EXPERTS_PER_GROUP + jnp.take(jnp.asarray(pairs[:, 1]), tpair)
    rows_left = jnp.take(counts, tile_bucket) - (tile_ids - jnp.take(tile_start, tile_bucket)) * tm
    tile_rows = jnp.where(tile_ids < used, jnp.clip(rows_left, 0, tm), 0).astype(jnp.int32)
    return pos.astype(jnp.int32), src, tile_ea.astype(jnp.int32), tile_eb.astype(jnp.int32), tile_rows, used


def _row_copy(src_hbm, row, buf, slot, r, sem):
    return pltpu.make_async_copy(src_hbm.at[pl.ds(row, 1), :], buf.at[slot, pl.ds(r, 1), :], sem.at[slot])


def _gather_start(idx_ref, base, rows, src_hbm, buf, slot, sem):
    for r in range(rows):
        _row_copy(src_hbm, idx_ref[base + r], buf, slot, r, sem).start()


def _gather_wait(rows, src_hbm, buf, slot, sem):
    pltpu.make_async_copy(src_hbm.at[pl.ds(0, rows), :], buf.at[slot], sem.at[slot]).wait()


def _moe_kernel(src_ref, ea_ref, eb_ref, rows_ref, h_hbm, w1a_ref, w3a_ref, w2a_ref, w1b_ref, w3b_ref, w2b_ref,
                y_ref, buf, sem, *, tm):
    i = pl.program_id(0)
    nt = pl.num_programs(0)
    depth = buf.shape[0]
    ahead = depth - 1
    slot = i % depth
    used = rows_ref[i] > 0
    d = y_ref.shape[1]

    @pl.when(i == 0)
    def _():
        for j in range(ahead):
            _gather_start(src_ref, j * tm, tm, h_hbm, buf, j, sem)

    @pl.when(jnp.logical_and(used, i + ahead < nt))
    def _():
        _gather_start(src_ref, (i + ahead) * tm, tm, h_hbm, buf, (i + ahead) % depth, sem)

    @pl.when(jnp.logical_or(i < ahead, rows_ref[jnp.maximum(i - ahead, 0)] > 0))
    def _():
        _gather_wait(tm, h_hbm, buf, slot, sem)

    @pl.when(jnp.logical_not(used))
    def _():
        y_ref[...] = jnp.zeros(y_ref.shape, y_ref.dtype)

    @pl.when(used)
    def _():
        h = buf[slot, :, :d].astype(BF16)

        def expert(w1_ref, w3_ref, w2_ref):
            a = (jax.nn.silu(jnp.dot(h, w1_ref[0], preferred_element_type=F32))
                 * jnp.dot(h, w3_ref[0], preferred_element_type=F32))
            return jnp.dot(a.astype(BF16), w2_ref[0], preferred_element_type=F32)

        w = buf[slot, :, d:]
        y_ref[...] = (expert(w1a_ref, w3a_ref, w2a_ref) * w[:, 0:1]
                      + expert(w1b_ref, w3b_ref, w2b_ref) * w[:, 1:2])


_MOE_GATHER_DEPTH = 4


def _moe_call(src, tile_ea, tile_eb, tile_rows, hn, w1, w3, w2, tm):
    n_tiles = tile_ea.shape[0]
    d = w1.shape[1]
    f = w1.shape[2]
    wa = lambda i, src, ea, eb, rows: (ea[i], 0, 0)
    wb = lambda i, src, ea, eb, rows: (eb[i], 0, 0)
    grid_spec = pltpu.PrefetchScalarGridSpec(
        num_scalar_prefetch=4,
        grid=(n_tiles,),
        in_specs=[
            pl.BlockSpec(memory_space=pl.ANY),
            pl.BlockSpec((1, d, f), wa), pl.BlockSpec((1, d, f), wa), pl.BlockSpec((1, f, d), wa),
            pl.BlockSpec((1, d, f), wb), pl.BlockSpec((1, d, f), wb), pl.BlockSpec((1, f, d), wb),
        ],
        out_specs=pl.BlockSpec((tm, d), lambda i, *_: (i, 0)),
        scratch_shapes=[pltpu.VMEM((_MOE_GATHER_DEPTH, tm, hn.shape[1]), F32),
                        pltpu.SemaphoreType.DMA((_MOE_GATHER_DEPTH,))],
    )
    assert n_tiles >= _MOE_GATHER_DEPTH
    return pl.pallas_call(
        functools.partial(_moe_kernel, tm=tm),
        grid_spec=grid_spec,
        out_shape=jax.ShapeDtypeStruct((n_tiles * tm, d), F32),
        compiler_params=_params(("arbitrary",)),
        name="moe",
    )(src, tile_ea, tile_eb, tile_rows, hn, w1, w3, w2, w1, w3, w2)


def _combine_kernel(pos_ref, y_hbm, x1_ref, mod_ref, op_ref, os_ref, buf, sem, *, tm, nt0):
    i = pl.program_id(0)
    nt = pl.num_programs(0)
    slot = i % 2

    @pl.when(i == 0)
    def _():
        _gather_start(pos_ref, 0, tm, y_hbm, buf, 0, sem)

    _gather_wait(tm, y_hbm, buf, slot, sem)

    @pl.when(i + 1 < nt)
    def _():
        _gather_start(pos_ref, (i + 1) * tm, tm, y_hbm, buf, 1 - slot, sem)

    out = x1_ref[...] + mod_ref[0, 5:6, :] * buf[slot]

    @pl.when(i < nt0)
    def _():
        op_ref[...] = out

    @pl.when(i >= nt0)
    def _():
        os_ref[...] = out


def _combine_call(lay, pos, y_sorted, x1, mod3, tm):
    t, d = x1.shape
    grid_spec = pltpu.PrefetchScalarGridSpec(
        num_scalar_prefetch=1,
        grid=(t // tm,),
        in_specs=[
            pl.BlockSpec(memory_space=pl.ANY),
            pl.BlockSpec((tm, d), lambda i, pos: (i, 0)),
            pl.BlockSpec((1, 6, d), lambda i, pos: (lay.batch_of_tile(i, tm), 0, 0)),
        ],
        out_specs=lay.split_specs(tm, d),
        scratch_shapes=[pltpu.VMEM((2, tm, d), F32), pltpu.SemaphoreType.DMA((2,))],
    )
    return pl.pallas_call(
        functools.partial(_combine_kernel, tm=tm, nt0=lay.t0 // tm),
        grid_spec=grid_spec,
        out_shape=lay.split_shapes(d, F32),
        compiler_params=_params(("arbitrary",)),
        name="combine",
    )(pos, y_sorted, x1, mod3)


def _rope_tables(n):
    n_rows = n // GRID_W
    rows = jnp.repeat(jnp.arange(n_rows), GRID_W).astype(F32)
    cols = jnp.tile(jnp.arange(GRID_W), n_rows).astype(F32)
    half = HEAD_DIM // 2
    inv = ROPE_THETA ** (-jnp.arange(0, half, 2, dtype=F32) / half)
    ang = jnp.concatenate([rows[:, None] * inv, cols[:, None] * inv], axis=-1)
    sign = jnp.tile(jnp.array([-1.0, 1.0], F32), half)
    return jnp.repeat(jnp.cos(ang), 2, axis=-1), jnp.repeat(jnp.sin(ang), 2, axis=-1) * sign


def _lambda_init(layer_idx):
    return 0.8 - 0.6 * math.exp(-0.3 * layer_idx)


def _tiled_gain(g):
    return jnp.tile(g, LANES // g.shape[-1]).reshape(1, LANES).astype(F32)


def kernel(x_prompt, x_sample, c_prompt, c_sample, rel_bias, w_ada, b_ada, g_norm_mix, w_in, g_qa, g_ka,
           g_qb, g_kb, lam_qk, g_subln, w_fourier, w_out, g_norm_ffn, w_group, w_expert, w1, w3, w2):
    b0, n0, d = x_prompt.shape
    b1, n1, _ = x_sample.shape
    depth = w_in.shape[0]
    lay = _Layout(b0, n0, b1, n1)
    trunks = ((0, b0, n0), (lay.t0, b1, n1))
    n_max = max(n0, n1)

    tm = _pick(math.gcd(n0, n1), 512)
    tq_a = _pick(math.gcd(n0, n1), 256)
    tk_a = _pick(math.gcd(n0, n1), 2048)
    t_b = _pick(math.gcd(n0, n1), 512)
    tm_moe = 256

    xp = x_prompt.reshape(b0 * n0, d)
    xs = x_sample.reshape(b1 * n1, d)
    nb = b0 + b1
    bp = -(-nb // 8) * 8
    c_all = jnp.zeros((bp, d), F32).at[:nb].set(jnp.concatenate([c_prompt, c_sample], axis=0))
    mod = _ada_call(c_all, w_ada, b_ada).reshape(depth, bp, 6, d)

    cos_e, sin_e = _rope_tables(n_max)
    chunk_ratio = {n: 2 if n // t_b > 4 else 1 for n in (n0, n1)}
    bands = {c: _bias_tables(rel_bias, t_b, c) for c in sorted(set(chunk_ratio.values()))}
    dft = {n: _dft_stage_tables(n) for n in sorted({n0, n1})}
    cidx = jnp.arange(C_GROUP_W, dtype=jnp.int32)
    ang_c = ((cidx[:, None] * cidx[None, :]) % C_GROUP_W).astype(F32) * (2.0 * math.pi / C_GROUP_W)
    dft_c = (jnp.concatenate([jnp.cos(ang_c), jnp.sin(ang_c)], axis=1) * C_GROUP_W ** -0.5).astype(BF16)

    for l in range(depth):
        mod3 = mod[l]
        lam_init = _lambda_init(l)
        qa, ka, va, qb, kb, vb, p, q = _proj_call(
            lay, xp, xs, mod3, g_norm_mix[l].reshape(1, d), w_in[l].astype(BF16),
            _tiled_gain(g_qa[l]), _tiled_gain(g_ka[l]), _tiled_gain(g_qb[l]), _tiled_gain(g_kb[l]),
            cos_e, sin_e, dft_c, tm)

        wf = w_fourier[l].astype(BF16)
        mix = jnp.zeros((lay.t, A_Q + B_WIDTH + C_WIDTH), BF16)
        for tok0, b, n in trunks:
            mix = _attn_a_call(qa, ka, va, mix, tok0, b, n, tq_a, min(tk_a, max(n // 2, LANES)))
            mix = _attn_b_call(qb, kb, vb, bands[chunk_ratio[n]], lam_qk[l], _tiled_gain(g_subln[l]), mix,
                               tok0, b, n, t_b, lam_init)
            mix = _fourier_call(dft[n][0], dft[n][1], p, q, wf, mix, tok0, b, n)

        w_router = jnp.zeros((d, LANES), F32).at[:, :N_GROUPS].set(w_group[l])
        w_router = w_router.at[:, N_GROUPS:N_GROUPS + N_EXPERTS].set(w_expert[l]).astype(BF16)
        x1, hn2, ids = _out_call(lay, mix, xp, xs, mod3, w_out[l].astype(BF16),
                                 g_norm_ffn[l].reshape(1, d), w_router, tm)

        pos, src, tile_ea, tile_eb, tile_rows, _ = _route_plan(ids, tm_moe, _pick(math.gcd(n0, n1), 2048))
        y_sorted = _moe_call(src, tile_ea, tile_eb, tile_rows, hn2, w1[l].astype(BF16), w3[l].astype(BF16),
                             w2[l].astype(BF16), tm_moe)
        xp, xs = _combine_call(lay, pos, y_sorted, x1, mod3, tm)

    return (xp.reshape(b0, n0, d), xs.reshape(b1, n1, d))
```
